```python
import math
import jax, jax.numpy as jnp
from jax import lax
import numpy as np

D_MODEL = 1024
BATCH = 4
SEQ = 4096
DEPTH = 1

GLA_HEADS = 4
GLA_DK = (D_MODEL // 2) // GLA_HEADS
GLA_DV = D_MODEL // GLA_HEADS
GLA_GATE_RANK = 16
GLA_GATE_NORMALIZER = 16.0
HGRN_EXPAND = 128
HGRN_HEADS = D_MODEL // HGRN_EXPAND
HGRN_DK = HGRN_EXPAND
HGRN_DV = D_MODEL // HGRN_HEADS

GLA_QK = GLA_HEADS * GLA_DK
GLA_V = GLA_HEADS * GLA_DV
HG_K = HGRN_HEADS * HGRN_DK
HG_V = HGRN_HEADS * HGRN_DV
SPLIT_SIZES = (GLA_QK, GLA_QK, GLA_V, GLA_V, GLA_GATE_RANK, GLA_GATE_RANK,
               HG_K, HG_K, HG_K, HG_V, HG_V, D_MODEL, D_MODEL)
PROJ_DIM = sum(SPLIT_SIZES)

CHUNK = 64
RMS_EPS = 1e-6
LN_EPS = 1e-5
DEEPNORM_ALPHA = (2.0 * DEPTH) ** 0.25
DEEPNORM_BETA = (8.0 * DEPTH) ** -0.25

kernel_name = 'hybrid_gla_hgrn2_bidir_deepnorm'


def _split_columns(p):
    points = np.cumsum(np.array(SPLIT_SIZES))[:-1].tolist()
    return jnp.split(p, points, axis=-1)


def _heads(a, h):
    b, t, c = a.shape
    return a.reshape(b, t, h, c // h)


def chunk_gated_linear_attn(q, k, v, g):
    B, T, H, dk = q.shape
    dv = v.shape[-1]
    n = T // CHUNK

    def to_chunks(a):
        return a.reshape(B, n, CHUNK, H, a.shape[-1]).transpose(1, 0, 3, 2, 4)

    qc, kc, vc, gc = to_chunks(q), to_chunks(k), to_chunks(v), to_chunks(g)
    mask = jnp.tril(jnp.ones((CHUNK, CHUNK), dtype=bool))[None, None, :, :, None]

    def step(S, inp):
        qi, ki, vi, gi = inp
        b = jnp.cumsum(gi, axis=2)
        b_last = b[:, :, -1:, :]
        o_inter = jnp.einsum('bhcd,bhde->bhce', qi * jnp.exp(b), S)
        rel = jnp.where(mask, b[:, :, :, None, :] - b[:, :, None, :, :], -jnp.inf)
        scores = jnp.einsum('bhid,bhjd,bhijd->bhij', qi, ki, jnp.exp(rel))
        o = o_inter + jnp.einsum('bhij,bhje->bhie', scores, vi)
        S = (jnp.exp(b_last[:, :, 0, :])[..., None] * S
             + jnp.einsum('bhjd,bhje->bhde', ki * jnp.exp(b_last - b), vi))
        return S, o

    S0 = jnp.zeros((B, H, dk, dv), jnp.float32)
    _, o = lax.scan(step, S0, (qc, kc, vc, gc))
    return o.transpose(1, 0, 3, 2, 4).reshape(B, T, H, dv)


def bidirectional_scan(q, k_f, k_b, v, g_f, g_b):
    fwd = chunk_gated_linear_attn(q, k_f, v, g_f)
    flip = lambda a: jnp.flip(a, axis=1)
    bwd = flip(chunk_gated_linear_attn(flip(q), flip(k_b), flip(v), flip(g_b)))
    return fwd + bwd


def head_rmsnorm(o, gain):
    return o * lax.rsqrt(jnp.mean(o * o, axis=-1, keepdims=True) + RMS_EPS) * gain


def layer_norm(x, g, b):
    mu = jnp.mean(x, axis=-1, keepdims=True)
    xc = x - mu
    var = jnp.mean(xc * xc, axis=-1, keepdims=True)
    return xc * lax.rsqrt(var + LN_EPS) * g + b


def hybrid_layer(h, layer, w_in, gk_up_f, gk_bias_f, gk_up_b, gk_bias_b, gla_norm_g,
                 lb_logits_f, lb_logits_b, hgrn_norm_g, w_branch_gla, w_branch_hgrn,
                 w_out, ln_g, ln_b):
    f32 = jnp.float32
    B, T, _ = h.shape
    p = jnp.einsum('btd,dp->btp', h, w_in)
    (a_q, a_k, a_v, a_gate, a_lr_f, a_lr_b,
     h_q, h_f_f, h_f_b, h_i, h_gate, m_gla, m_hgrn) = _split_columns(p)

    g_f = jax.nn.log_sigmoid(a_lr_f @ gk_up_f + gk_bias_f) / GLA_GATE_NORMALIZER
    g_b = jax.nn.log_sigmoid(a_lr_b @ gk_up_b + gk_bias_b) / GLA_GATE_NORMALIZER
    qa = _heads(a_q, GLA_HEADS) * (GLA_DK ** -0.5)
    ka = _heads(a_k, GLA_HEADS)
    o_a = bidirectional_scan(qa, ka, ka, _heads(a_v, GLA_HEADS),
                             _heads(g_f, GLA_HEADS), _heads(g_b, GLA_HEADS))
    o_a = head_rmsnorm(o_a, gla_norm_g).reshape(B, T, GLA_V) * jax.nn.silu(a_gate)
    y_gla = o_a @ w_branch_gla

    lb_f = jnp.cumsum(jax.nn.softmax(lb_logits_f, axis=0), axis=0)[layer]
    lb_b = jnp.cumsum(jax.nn.softmax(lb_logits_b, axis=0), axis=0)[layer]
    f_f = lb_f + (1.0 - lb_f) * jax.nn.sigmoid(h_f_f)
    f_b = lb_b + (1.0 - lb_b) * jax.nn.sigmoid(h_f_b)
    qh = _heads(jax.nn.silu(h_q), HGRN_HEADS) * (HGRN_DK ** -0.5)
    o_h = bidirectional_scan(qh, _heads(1.0 - f_f, HGRN_HEADS), _heads(1.0 - f_b, HGRN_HEADS),
                             _heads(h_i, HGRN_HEADS),
                             _heads(jnp.log(f_f), HGRN_HEADS), _heads(jnp.log(f_b), HGRN_HEADS))
    o_h = head_rmsnorm(o_h, hgrn_norm_g).reshape(B, T, HG_V) * jax.nn.silu(h_gate)
    y_hgrn = o_h @ w_branch_hgrn

    y = jax.nn.sigmoid(m_gla) * y_gla + jax.nn.sigmoid(m_hgrn) * y_hgrn
    y = y @ w_out
    return layer_norm(DEEPNORM_ALPHA * h + y, ln_g, ln_b).astype(f32)


def setup_inputs(seed: int = 0) -> dict:
    key = jax.random.key(seed)
    ks = jax.random.split(key, 16)
    n = jax.random.normal
    D = D_MODEL
    return {
        'x': n(ks[0], (BATCH, SEQ, D), jnp.float32),
        'w_in': n(ks[1], (DEPTH, D, PROJ_DIM), jnp.float32) * D ** -0.5,
        'gla_gk_up_f': n(ks[2], (DEPTH, GLA_GATE_RANK, GLA_QK), jnp.float32) * GLA_GATE_RANK ** -0.5,
        'gla_gk_bias_f': n(ks[3], (DEPTH, GLA_QK), jnp.float32) * 0.02,
        'gla_gk_up_b': n(ks[4], (DEPTH, GLA_GATE_RANK, GLA_QK), jnp.float32) * GLA_GATE_RANK ** -0.5,
        'gla_gk_bias_b': n(ks[5], (DEPTH, GLA_QK), jnp.float32) * 0.02,
        'gla_norm_g': 1.0 + 0.02 * n(ks[6], (DEPTH, GLA_DV), jnp.float32),
        'hgrn_lb_logits_f': n(ks[7], (DEPTH + 1, HG_K), jnp.float32) * 0.1,
        'hgrn_lb_logits_b': n(ks[8], (DEPTH + 1, HG_K), jnp.float32) * 0.1,
        'hgrn_norm_g': 1.0 + 0.02 * n(ks[9], (DEPTH, HGRN_DV), jnp.float32),
        'w_branch_gla': n(ks[10], (DEPTH, GLA_V, D), jnp.float32) * (GLA_V ** -0.5) * DEEPNORM_BETA,
        'w_branch_hgrn': n(ks[11], (DEPTH, HG_V, D), jnp.float32) * (HG_V ** -0.5) * DEEPNORM_BETA,
        'w_out': n(ks[12], (DEPTH, D, D), jnp.float32) * (D ** -0.5) * DEEPNORM_BETA,
        'ln_g': 1.0 + 0.02 * n(ks[13], (DEPTH, D), jnp.float32),
        'ln_b': 0.02 * n(ks[14], (DEPTH, D), jnp.float32),
    }


def reference(x, w_in, gla_gk_up_f, gla_gk_bias_f, gla_gk_up_b, gla_gk_bias_b, gla_norm_g,
              hgrn_lb_logits_f, hgrn_lb_logits_b, hgrn_norm_g, w_branch_gla, w_branch_hgrn,
              w_out, ln_g, ln_b):
    f32 = jnp.float32
    c = lambda a: a.astype(f32)
    h = c(x)
    lbf = c(hgrn_lb_logits_f)
    lbb = c(hgrn_lb_logits_b)
    for layer in range(DEPTH):
        h = hybrid_layer(h, layer, c(w_in[layer]), c(gla_gk_up_f[layer]), c(gla_gk_bias_f[layer]),
                         c(gla_gk_up_b[layer]), c(gla_gk_bias_b[layer]), c(gla_norm_g[layer]),
                         lbf, lbb, c(hgrn_norm_g[layer]), c(w_branch_gla[layer]),
                         c(w_branch_hgrn[layer]), c(w_out[layer]), c(ln_g[layer]), c(ln_b[layer]))
    return h.astype(x.dtype)
```

```python
import functools
import math

import jax
import jax.numpy as jnp
import numpy as np
from jax import lax
from jax.experimental import pallas as pl
from jax.experimental.pallas import tpu as pltpu

F32 = jnp.float32
BF16 = jnp.bfloat16

D_MODEL = 1024
DEPTH = 1
GLA_HEADS = 4
GLA_DK = 128
GLA_DV = 256
GLA_GATE_RANK = 16
GLA_GATE_NORMALIZER = 16.0
HGRN_HEADS = 8
HGRN_DK = 128
HGRN_DV = 128
GLA_QK = GLA_HEADS * GLA_DK
GLA_V = GLA_HEADS * GLA_DV
HG_K = HGRN_HEADS * HGRN_DK
HG_V = HGRN_HEADS * HGRN_DV
RMS_EPS = 1e-6
LN_EPS = 1e-5
DEEPNORM_ALPHA = (2.0 * DEPTH) ** 0.25

LANES = 128

_ORIG_SIZES = (GLA_QK, GLA_QK, GLA_V, GLA_V, GLA_GATE_RANK, GLA_GATE_RANK,
               HG_K, HG_K, HG_K, HG_V, HG_V, D_MODEL, D_MODEL)
_ORIG_NAMES = ("a_q", "a_k", "a_v", "a_gate", "lr_f", "lr_b",
               "h_q", "h_ff", "h_fb", "h_i", "h_gate", "m_gla", "m_hgrn")
_NEW_ORDER = ("a_q", "a_k", "a_v", "a_gate", "h_q", "h_ff", "h_fb", "h_i", "h_gate",
              "m_gla", "m_hgrn", "lr_f", "lr_b")


def _column_plan():
    starts = dict(zip(_ORIG_NAMES, np.cumsum((0,) + _ORIG_SIZES[:-1]).tolist()))
    sizes = dict(zip(_ORIG_NAMES, _ORIG_SIZES))
    new_off = {}
    off = 0
    for name in _NEW_ORDER:
        new_off[name] = off
        off += sizes[name]
    padded = -(-off // LANES) * LANES
    return starts, sizes, new_off, off, padded


_ORIG_START, _SIZE, COL, _USED_COLS, PROJ_PAD = _column_plan()

CHUNK = 64
PROJ_TM = 1024
PROJ_TN = 1152
MERGE_TM = 512
VMEM_LIMIT = 56 * 1024 * 1024


def _proj_kernel(x_ref, w_ref, o_ref):
    o_ref[...] = jnp.dot(x_ref[...].astype(BF16), w_ref[...],
                         preferred_element_type=F32).astype(BF16)


def _project(x2d, w_pad):
    n = x2d.shape[0]
    assert n % PROJ_TM == 0 and PROJ_PAD % PROJ_TN == 0
    return pl.pallas_call(
        _proj_kernel,
        grid=(n // PROJ_TM, PROJ_PAD // PROJ_TN),
        in_specs=[pl.BlockSpec((PROJ_TM, D_MODEL), lambda i, j: (i, 0)),
                  pl.BlockSpec((D_MODEL, PROJ_TN), lambda i, j: (0, j))],
        out_specs=pl.BlockSpec((PROJ_TM, PROJ_TN), lambda i, j: (i, j)),
        out_shape=jax.ShapeDtypeStruct((n, PROJ_PAD), BF16),
        compiler_params=pltpu.CompilerParams(
            dimension_semantics=("arbitrary", "arbitrary"), vmem_limit_bytes=VMEM_LIMIT),
        name="proj",
    )(x2d, w_pad)


def _split_bf16(g):
    hi = g.astype(BF16)
    lo = (g - hi.astype(F32)).astype(BF16)
    return jnp.concatenate([hi, lo], axis=1)


def _dot_nt(a, b):
    return lax.dot_general(a, b, (((1,), (1,)), ((), ())), preferred_element_type=F32)


def _dot_tn(a, b):
    return lax.dot_general(a, b, (((0,), (0,)), ((), ())), preferred_element_type=F32)


def _scan_core(prep, v_ref, o_ref, qi_refs, kp_refs, a_refs, dec_refs, st_refs, oacc_ref,
               seq, dk, dv):
    c = CHUNK
    n = seq // c
    mid = c // 2
    row = lax.broadcasted_iota(jnp.int32, (c, c), 0)
    col = lax.broadcasted_iota(jnp.int32, (c, c), 1)
    masks = (col <= row, col >= row)
    cums = tuple(jnp.where(m, 1.0, 0.0).astype(BF16) for m in masks)
    end_rows = (c - 1, 0)

    def phase1(i, carry):
        rows = pl.ds(pl.multiple_of(i * c, c), c)
        q, kg_f, kg_b = prep(rows)
        for d, (k, g) in enumerate((kg_f, kg_b)):
            bb = jnp.dot(cums[d], _split_bf16(g), preferred_element_type=F32)
            b = bb[:, :dk] + bb[:, dk:]
            b_mid = b[mid:mid + 1, :]
            b_end = b[end_rows[d]:end_rows[d] + 1, :]
            rel = b - b_mid
            qm = q * jnp.exp(rel)
            km = k * jnp.exp(-rel)
            scores = _dot_nt(qm.astype(BF16), km.astype(BF16))
            a_refs[d][rows, :] = jnp.where(masks[d], scores, 0.0).astype(BF16)
            qi_refs[d][rows, :] = (qm * jnp.exp(b_mid)).astype(BF16)
            kp_refs[d][rows, :] = (km * jnp.exp(b_end - b_mid)).astype(BF16)
            dec_refs[d][i] = jnp.exp(b_end)
        return carry

    lax.fori_loop(0, n, phase1, 0, unroll=2)

    oacc_ref[...] = jnp.zeros_like(oacc_ref)
    for d in range(2):
        st_refs[d][...] = jnp.zeros_like(st_refs[d])

    def phase2(i, carry):
        for d, ci in enumerate((i, n - 1 - i)):
            rows = pl.ds(pl.multiple_of(ci * c, c), c)
            v = v_ref[0, rows, :]
            st = st_refs[d][...]
            o = _dot_nt(qi_refs[d][rows, :], st.astype(BF16))
            o = o + jnp.dot(a_refs[d][rows, :], v, preferred_element_type=F32)
            oacc_ref[rows, :] += o
            st_refs[d][...] = st * dec_refs[d][ci] + _dot_tn(v, kp_refs[d][rows, :])
        return carry

    lax.fori_loop(0, n, phase2, 0, unroll=2)
    o_ref[0] = oacc_ref[...].astype(o_ref.dtype)


def _scan_scratch(seq, dk, dv):
    per_dir = lambda shape, dt: [pltpu.VMEM(shape, dt), pltpu.VMEM(shape, dt)]
    return (per_dir((seq, dk), BF16)
            + per_dir((seq, dk), BF16)
            + per_dir((seq, CHUNK), BF16)
            + per_dir((seq // CHUNK, 1, dk), F32)
            + per_dir((dv, dk), F32)
            + [pltpu.VMEM((seq, dv), F32)])


def _log_sigmoid(z):
    return jnp.minimum(z, 0.0) - jnp.log(1.0 + jnp.exp(-jnp.abs(z)))


def _gla_kernel(q_ref, k_ref, v_ref, lr_ref, upf_ref, upb_ref, bf_ref, bb_ref, o_ref, *scratch,
                seq):
    qi = scratch[0:2]; kp = scratch[2:4]; a = scratch[4:6]; dec = scratch[6:8]
    st = scratch[8:10]; oacc = scratch[10]
    scale = GLA_DK ** -0.5
    ups = (upf_ref[0], upb_ref[0])
    biases = (bf_ref[0], bb_ref[0])

    def prep(rows):
        q = q_ref[0, rows, :].astype(F32) * scale
        k = k_ref[0, rows, :].astype(F32)
        lr = lr_ref[0, rows, :]
        out = []
        for d in range(2):
            z = jnp.dot(lr, ups[d], preferred_element_type=F32) + biases[d]
            out.append((k, _log_sigmoid(z) * (1.0 / GLA_GATE_NORMALIZER)))
        return q, out[0], out[1]

    _scan_core(prep, v_ref, o_ref, qi, kp, a, dec, st, oacc, seq, GLA_DK, GLA_DV)


def _gla_scan(p3, upf, upb, bias_f, bias_b):
    bsz, seq, _ = p3.shape
    kb = lambda name: COL[name] // GLA_DK
    in_specs = [
        pl.BlockSpec((1, seq, GLA_DK), lambda b, h: (b, 0, kb("a_q") + h)),
        pl.BlockSpec((1, seq, GLA_DK), lambda b, h: (b, 0, kb("a_k") + h)),
        pl.BlockSpec((1, seq, GLA_DV), lambda b, h: (b, 0, COL["a_v"] // GLA_DV + h)),
        pl.BlockSpec((1, seq, LANES), lambda b, h: (b, 0, COL["lr_f"] // LANES)),
        pl.BlockSpec((1, LANES, GLA_DK), lambda b, h: (h, 0, 0)),
        pl.BlockSpec((1, LANES, GLA_DK), lambda b, h: (h, 0, 0)),
        pl.BlockSpec((1, 1, GLA_DK), lambda b, h: (h, 0, 0)),
        pl.BlockSpec((1, 1, GLA_DK), lambda b, h: (h, 0, 0)),
    ]
    return pl.pallas_call(
        functools.partial(_gla_kernel, seq=seq),
        grid=(bsz, GLA_HEADS),
        in_specs=in_specs,
        out_specs=pl.BlockSpec((1, seq, GLA_DV), lambda b, h: (b, 0, h)),
        out_shape=jax.ShapeDtypeStruct((bsz, seq, GLA_V), BF16),
        scratch_shapes=_scan_scratch(seq, GLA_DK, GLA_DV),
        compiler_params=pltpu.CompilerParams(
            dimension_semantics=("arbitrary", "arbitrary"), vmem_limit_bytes=VMEM_LIMIT),
        name="gla_scan",
    )(p3, p3, p3, p3, upf, upb, bias_f, bias_b)


def _lower_bound(logits, layer):
    m = jnp.max(logits, axis=0, keepdims=True)
    e = jnp.exp(logits - m)
    return jnp.sum(e[:layer + 1], axis=0, keepdims=True) / jnp.sum(e, axis=0, keepdims=True)


def _hgrn_kernel(q_ref, ff_ref, fb_ref, v_ref, lf_ref, lb_ref, o_ref, *scratch, seq):
    qi = scratch[0:2]; kp = scratch[2:4]; a = scratch[4:6]; dec = scratch[6:8]
    st = scratch[8:10]; oacc = scratch[10]
    scale = HGRN_DK ** -0.5
    lbs = (_lower_bound(lf_ref[0], 0), _lower_bound(lb_ref[0], 0))
    f_refs = (ff_ref, fb_ref)

    def prep(rows):
        hq = q_ref[0, rows, :].astype(F32)
        q = hq * jax.nn.sigmoid(hq) * scale
        out = []
        for d in range(2):
            f = lbs[d] + (1.0 - lbs[d]) * jax.nn.sigmoid(f_refs[d][0, rows, :].astype(F32))
            out.append((1.0 - f, jnp.log(f)))
        return q, out[0], out[1]

    _scan_core(prep, v_ref, o_ref, qi, kp, a, dec, st, oacc, seq, HGRN_DK, HGRN_DV)


def _hgrn_scan(p3, logits_f, logits_b):
    bsz, seq, _ = p3.shape
    kb = lambda name: COL[name] // HGRN_DK
    nl = logits_f.shape[1]
    in_specs = [
        pl.BlockSpec((1, seq, HGRN_DK), lambda b, h: (b, 0, kb("h_q") + h)),
        pl.BlockSpec((1, seq, HGRN_DK), lambda b, h: (b, 0, kb("h_ff") + h)),
        pl.BlockSpec((1, seq, HGRN_DK), lambda b, h: (b, 0, kb("h_fb") + h)),
        pl.BlockSpec((1, seq, HGRN_DV), lambda b, h: (b, 0, kb("h_i") + h)),
        pl.BlockSpec((1, nl, HGRN_DK), lambda b, h: (h, 0, 0)),
        pl.BlockSpec((1, nl, HGRN_DK), lambda b, h: (h, 0, 0)),
    ]
    return pl.pallas_call(
        functools.partial(_hgrn_kernel, seq=seq),
        grid=(bsz, HGRN_HEADS),
        in_specs=in_specs,
        out_specs=pl.BlockSpec((1, seq, HGRN_DV), lambda b, h: (b, 0, h)),
        out_shape=jax.ShapeDtypeStruct((bsz, seq, HG_V), BF16),
        scratch_shapes=_scan_scratch(seq, HGRN_DK, HGRN_DV),
        compiler_params=pltpu.CompilerParams(
            dimension_semantics=("arbitrary", "arbitrary"), vmem_limit_bytes=VMEM_LIMIT),
        name="hgrn_scan",
    )(p3, p3, p3, p3, logits_f, logits_b)


def _head_rmsnorm_gate(o, gain, gate, heads, dv):
    parts = []
    for h in range(heads):
        seg = o[:, h * dv:(h + 1) * dv]
        ms = jnp.mean(seg * seg, axis=-1, keepdims=True)
        parts.append(seg * lax.rsqrt(ms + RMS_EPS) * gain)
    normed = jnp.concatenate(parts, axis=1)
    return (normed * (gate * jax.nn.sigmoid(gate))).astype(BF16)


def _merge_kernel(oa_ref, oh_ref, ag_ref, hg_ref, mg_ref, mh_ref, x_ref, wa_ref, wh_ref, wo_ref,
                  ga_ref, gh_ref, lng_ref, lnb_ref, out_ref):
    na = _head_rmsnorm_gate(oa_ref[...].astype(F32), ga_ref[...], ag_ref[...].astype(F32),
                            GLA_HEADS, GLA_DV)
    nh = _head_rmsnorm_gate(oh_ref[...].astype(F32), gh_ref[...], hg_ref[...].astype(F32),
                            HGRN_HEADS, HGRN_DV)
    y_gla = jnp.dot(na, wa_ref[...], preferred_element_type=F32)
    y_hgrn = jnp.dot(nh, wh_ref[...], preferred_element_type=F32)
    y = (jax.nn.sigmoid(mg_ref[...].astype(F32)) * y_gla
         + jax.nn.sigmoid(mh_ref[...].astype(F32)) * y_hgrn)
    y = jnp.dot(y.astype(BF16), wo_ref[...], preferred_element_type=F32)
    r = DEEPNORM_ALPHA * x_ref[...] + y
    mu = jnp.mean(r, axis=-1, keepdims=True)
    rc = r - mu
    var = jnp.mean(rc * rc, axis=-1, keepdims=True)
    out_ref[...] = rc * lax.rsqrt(var + LN_EPS) * lng_ref[...] + lnb_ref[...]


def _merge(o_gla, o_hgrn, p2, x2d, w_gla, w_hgrn, w_out, gain_a, gain_h, ln_g, ln_b):
    n = x2d.shape[0]
    tm = MERGE_TM
    wide = lambda name: pl.BlockSpec((tm, D_MODEL), lambda i, name=name: (i, COL[name] // D_MODEL))
    row = lambda: pl.BlockSpec((tm, D_MODEL), lambda i: (i, 0))
    full = lambda shape: pl.BlockSpec(shape, lambda i: (0,) * len(shape))
    return pl.pallas_call(
        _merge_kernel,
        grid=(n // tm,),
        in_specs=[row(), row(), wide("a_gate"), wide("h_gate"), wide("m_gla"), wide("m_hgrn"), row(),
                  full((GLA_V, D_MODEL)), full((HG_V, D_MODEL)), full((D_MODEL, D_MODEL)),
                  full((1, GLA_DV)), full((1, HGRN_DV)), full((1, D_MODEL)), full((1, D_MODEL))],
        out_specs=row(),
        out_shape=jax.ShapeDtypeStruct((n, D_MODEL), F32),
        compiler_params=pltpu.CompilerParams(
            dimension_semantics=("arbitrary",), vmem_limit_bytes=VMEM_LIMIT),
        name="merge",
    )(o_gla, o_hgrn, p2, p2, p2, p2, x2d, w_gla, w_hgrn, w_out, gain_a, gain_h, ln_g, ln_b)


def _regroup_w_in(w):
    cols = [w[:, _ORIG_START[nm]:_ORIG_START[nm] + _SIZE[nm]] for nm in _NEW_ORDER]
    cols.append(jnp.zeros((w.shape[0], PROJ_PAD - _USED_COLS), w.dtype))
    return jnp.concatenate(cols, axis=1).astype(BF16)


def _pad_up(up, row_off):
    rank = up.shape[0]
    per_head = up.reshape(rank, GLA_HEADS, GLA_DK).transpose(1, 0, 2)
    out = jnp.zeros((GLA_HEADS, LANES, GLA_DK), F32)
    return out.at[:, row_off:row_off + rank, :].set(per_head).astype(BF16)


def kernel(x, w_in, gla_gk_up_f, gla_gk_bias_f, gla_gk_up_b, gla_gk_bias_b, gla_norm_g,
           hgrn_lb_logits_f, hgrn_lb_logits_b, hgrn_norm_g, w_branch_gla, w_branch_hgrn,
           w_out, ln_g, ln_b):
    assert w_in.shape[0] == DEPTH == 1
    bsz, seq, d = x.shape
    assert d == D_MODEL and seq % CHUNK == 0
    x2d = x.astype(F32).reshape(bsz * seq, d)

    p2 = _project(x2d, _regroup_w_in(w_in[0].astype(F32)))
    p3 = p2.reshape(bsz, seq, PROJ_PAD)

    lr_off = COL["lr_f"] % LANES
    o_gla = _gla_scan(
        p3,
        _pad_up(gla_gk_up_f[0].astype(F32), lr_off),
        _pad_up(gla_gk_up_b[0].astype(F32), lr_off + GLA_GATE_RANK),
        gla_gk_bias_f[0].astype(F32).reshape(GLA_HEADS, 1, GLA_DK),
        gla_gk_bias_b[0].astype(F32).reshape(GLA_HEADS, 1, GLA_DK))

    to_heads = lambda l: l.astype(F32).reshape(DEPTH + 1, HGRN_HEADS, HGRN_DK).transpose(1, 0, 2)
    o_hgrn = _hgrn_scan(p3, to_heads(hgrn_lb_logits_f), to_heads(hgrn_lb_logits_b))

    out = _merge(
        o_gla.reshape(bsz * seq, GLA_V), o_hgrn.reshape(bsz * seq, HG_V), p2, x2d,
        w_branch_gla[0].astype(BF16), w_branch_hgrn[0].astype(BF16), w_out[0].astype(BF16),
        gla_norm_g[0].astype(F32).reshape(1, GLA_DV), hgrn_norm_g[0].astype(F32).reshape(1, HGRN_DV),
        ln_g[0].astype(F32).reshape(1, D_MODEL), ln_b[0].astype(F32).reshape(1, D_MODEL))
    return out.reshape(bsz, seq, d).astype(x.dtype)
```

```python
import functools

import jax
import jax.numpy as jnp
import numpy as np
from jax import lax
from jax.experimental import pallas as pl
from jax.experimental.pallas import tpu as pltpu

F32 = jnp.float32
BF16 = jnp.bfloat16

D_MODEL = 1024
DEPTH = 1
GLA_HEADS = 4
GLA_DK = 128
GLA_DV = 256
GLA_GATE_RANK = 16
GLA_GATE_NORMALIZER = 16.0
HGRN_HEADS = 8
HGRN_DK = 128
HGRN_DV = 128
GLA_QK = GLA_HEADS * GLA_DK
GLA_V = GLA_HEADS * GLA_DV
HG_K = HGRN_HEADS * HGRN_DK
HG_V = HGRN_HEADS * HGRN_DV
RMS_EPS = 1e-6
LN_EPS = 1e-5
DEEPNORM_ALPHA = (2.0 * DEPTH) ** 0.25

LANES = 128

_ORIG_SIZES = (GLA_QK, GLA_QK, GLA_V, GLA_V, GLA_GATE_RANK, GLA_GATE_RANK,
               HG_K, HG_K, HG_K, HG_V, HG_V, D_MODEL, D_MODEL)
_ORIG_NAMES = ("a_q", "a_k", "a_v", "a_gate", "lr_f", "lr_b",
               "h_q", "h_ff", "h_fb", "h_i", "h_gate", "m_gla", "m_hgrn")
_NEW_ORDER = ("a_q", "a_k", "a_v", "a_gate", "h_q", "h_ff", "h_fb", "h_i", "h_gate",
              "m_gla", "m_hgrn", "lr_f", "lr_b")


def _column_plan():
    starts = dict(zip(_ORIG_NAMES, np.cumsum((0,) + _ORIG_SIZES[:-1]).tolist()))
    sizes = dict(zip(_ORIG_NAMES, _ORIG_SIZES))
    new_off = {}
    off = 0
    for name in _NEW_ORDER:
        new_off[name] = off
        off += sizes[name]
    padded = -(-off // LANES) * LANES
    return starts, sizes, new_off, off, padded


_ORIG_START, _SIZE, COL, _USED_COLS, PROJ_PAD = _column_plan()

CHUNK = 64
SCAN_TILE = 256
PHASE2_UNROLL = 4
PROJ_TM = 1024
PROJ_TN = 1152
MERGE_TM = 512
VMEM_LIMIT = 56 * 1024 * 1024


def _proj_kernel(x_ref, w_ref, o_ref):
    o_ref[...] = jnp.dot(x_ref[...].astype(BF16), w_ref[...],
                         preferred_element_type=F32).astype(BF16)


def _project(x2d, w_pad):
    n = x2d.shape[0]
    assert n % PROJ_TM == 0 and PROJ_PAD % PROJ_TN == 0
    return pl.pallas_call(
        _proj_kernel,
        grid=(n // PROJ_TM, PROJ_PAD // PROJ_TN),
        in_specs=[pl.BlockSpec((PROJ_TM, D_MODEL), lambda i, j: (i, 0)),
                  pl.BlockSpec((D_MODEL, PROJ_TN), lambda i, j: (0, j))],
        out_specs=pl.BlockSpec((PROJ_TM, PROJ_TN), lambda i, j: (i, j)),
        out_shape=jax.ShapeDtypeStruct((n, PROJ_PAD), BF16),
        compiler_params=pltpu.CompilerParams(
            dimension_semantics=("arbitrary", "arbitrary"), vmem_limit_bytes=VMEM_LIMIT),
        name="proj",
    )(x2d, w_pad)


def _split_bf16(g):
    hi = g.astype(BF16)
    lo = (g - hi.astype(F32)).astype(BF16)
    return jnp.concatenate([hi, lo], axis=1)


def _dot_nt(a, b):
    return lax.dot_general(a, b, (((1,), (1,)), ((), ())), preferred_element_type=F32)


def _dot_tn(a, b):
    return lax.dot_general(a, b, (((0,), (0,)), ((), ())), preferred_element_type=F32)


def _block_diag_cumsum_mats(tile, c):
    row = lax.broadcasted_iota(jnp.int32, (tile, tile), 0)
    col = lax.broadcasted_iota(jnp.int32, (tile, tile), 1)
    same = (row // c) == (col // c)
    one = lambda m: jnp.where(m, 1.0, 0.0).astype(BF16)
    return one(same & (col <= row)), one(same & (col >= row))


def _scan_core(prep, v_ref, o_ref, scratch, seq, dk, dv):
    qi_refs, upd_refs, dec_refs, st_refs = scratch[0:2], scratch[2:4], scratch[4:6], scratch[6:8]
    oacc_ref, sq_ref, sk_refs, sb_refs = scratch[8], scratch[9], scratch[10:12], scratch[12:14]
    c = CHUNK
    tile = SCAN_TILE
    per_tile = tile // c
    n_tiles = seq // tile
    n = seq // c
    mid = c // 2
    row = lax.broadcasted_iota(jnp.int32, (c, c), 0)
    col = lax.broadcasted_iota(jnp.int32, (c, c), 1)
    masks = (col <= row, col >= row)
    cums = _block_diag_cumsum_mats(tile, c)
    end_rows = (c - 1, 0)

    def gates(t):
        return prep(pl.ds(pl.multiple_of(t * tile, tile), tile))

    def stage(q, kg_f, kg_b):
        sq_ref[...] = q
        for d, (k, g) in enumerate((kg_f, kg_b)):
            bb = jnp.dot(cums[d], _split_bf16(g), preferred_element_type=F32)
            sk_refs[d][...] = k
            sb_refs[d][...] = bb[:, :dk] + bb[:, dk:]

    stage(*gates(0))

    def phase1(i, carry):
        q = sq_ref[...]
        ks = (sk_refs[0][...], sk_refs[1][...])
        bs = (sb_refs[0][...], sb_refs[1][...])
        v = v_ref[0, pl.ds(pl.multiple_of(i * tile, tile), tile), :]
        slices = [slice(j * c, (j + 1) * c) for j in range(per_tile)]
        mixed, kps = [], []
        for j, sl in enumerate(slices):
            chunk_rows = pl.ds(pl.multiple_of(i * tile + j * c, c), c)
            acc, kp_pair = None, []
            for d in range(2):
                b = bs[d][sl]
                b_mid = b[mid:mid + 1, :]
                b_end = b[end_rows[d]:end_rows[d] + 1, :]
                rel = b - b_mid
                qm = q[sl] * jnp.exp(rel)
                km = ks[d][sl] * jnp.exp(-rel)
                scores = jnp.where(masks[d], _dot_nt(qm.astype(BF16), km.astype(BF16)), 0.0)
                acc = scores if acc is None else acc + scores
                qi_refs[d][chunk_rows, :] = (qm * jnp.exp(b_mid)).astype(BF16)
                kp_pair.append((km * jnp.exp(b_end - b_mid)).astype(BF16))
                dec_refs[d][i * per_tile + j] = jnp.exp(b_end)
            mixed.append(acc.astype(BF16))
            kps.append(jnp.concatenate(kp_pair, axis=1))
        next_gates = gates(jnp.minimum(i + 1, n_tiles - 1))
        for j, sl in enumerate(slices):
            upd = _dot_tn(v[sl], kps[j])
            upd_refs[0][i * per_tile + j] = upd[:, :dk]
            upd_refs[1][i * per_tile + j] = upd[:, dk:]
        stage(*next_gates)
        for j, sl in enumerate(slices):
            chunk_rows = pl.ds(pl.multiple_of(i * tile + j * c, c), c)
            oacc_ref[chunk_rows, :] = jnp.dot(mixed[j], v[sl], preferred_element_type=F32)
        return carry

    lax.fori_loop(0, n_tiles, phase1, 0)

    for d in range(2):
        st_refs[d][...] = jnp.zeros_like(st_refs[d])

    def phase2(i, carry):
        for d, ci in enumerate((i, n - 1 - i)):
            rows = pl.ds(pl.multiple_of(ci * c, c), c)
            st = st_refs[d][...]
            oacc_ref[rows, :] += _dot_nt(qi_refs[d][rows, :], st.astype(BF16))
            st_refs[d][...] = st * dec_refs[d][ci] + upd_refs[d][ci]
        return carry

    lax.fori_loop(0, n, phase2, 0, unroll=PHASE2_UNROLL)
    o_ref[0] = oacc_ref[...].astype(o_ref.dtype)


def _scan_scratch(seq, dk, dv):
    per_dir = lambda shape, dt: [pltpu.VMEM(shape, dt), pltpu.VMEM(shape, dt)]
    n = seq // CHUNK
    return (per_dir((seq, dk), BF16)
            + per_dir((n, dv, dk), F32)
            + per_dir((n, 1, dk), F32)
            + per_dir((dv, dk), F32)
            + [pltpu.VMEM((seq, dv), F32)]
            + [pltpu.VMEM((SCAN_TILE, dk), F32)]
            + per_dir((SCAN_TILE, dk), F32)
            + per_dir((SCAN_TILE, dk), F32))


def _log_sigmoid(z):
    return jnp.minimum(z, 0.0) - jnp.log(1.0 + jnp.exp(-jnp.abs(z)))


def _gla_kernel(q_ref, k_ref, v_ref, lr_ref, up_ref, bias_ref, o_ref, *scratch, seq):
    scale = GLA_DK ** -0.5
    up = up_ref[0]
    bias = bias_ref[0]

    def prep(rows):
        q = q_ref[0, rows, :].astype(F32) * scale
        k = k_ref[0, rows, :].astype(F32)
        z = jnp.dot(lr_ref[0, rows, :], up, preferred_element_type=F32) + bias
        g = _log_sigmoid(z) * (1.0 / GLA_GATE_NORMALIZER)
        return q, (k, g[:, :GLA_DK]), (k, g[:, GLA_DK:])

    _scan_core(prep, v_ref, o_ref, scratch, seq, GLA_DK, GLA_DV)


def _gla_scan(p3, up, bias):
    bsz, seq, _ = p3.shape
    kb = lambda name: COL[name] // GLA_DK
    in_specs = [
        pl.BlockSpec((1, seq, GLA_DK), lambda b, h: (b, 0, kb("a_q") + h)),
        pl.BlockSpec((1, seq, GLA_DK), lambda b, h: (b, 0, kb("a_k") + h)),
        pl.BlockSpec((1, seq, GLA_DV), lambda b, h: (b, 0, COL["a_v"] // GLA_DV + h)),
        pl.BlockSpec((1, seq, LANES), lambda b, h: (b, 0, COL["lr_f"] // LANES)),
        pl.BlockSpec((1, LANES, 2 * GLA_DK), lambda b, h: (h, 0, 0)),
        pl.BlockSpec((1, 1, 2 * GLA_DK), lambda b, h: (h, 0, 0)),
    ]
    return pl.pallas_call(
        functools.partial(_gla_kernel, seq=seq),
        grid=(bsz, GLA_HEADS),
        in_specs=in_specs,
        out_specs=pl.BlockSpec((1, seq, GLA_DV), lambda b, h: (b, 0, h)),
        out_shape=jax.ShapeDtypeStruct((bsz, seq, GLA_V), BF16),
        scratch_shapes=_scan_scratch(seq, GLA_DK, GLA_DV),
        compiler_params=pltpu.CompilerParams(
            dimension_semantics=("arbitrary", "arbitrary"), vmem_limit_bytes=VMEM_LIMIT),
        name="gla_scan",
    )(p3, p3, p3, p3, up, bias)


def _lower_bound(logits, layer):
    m = jnp.max(logits, axis=0, keepdims=True)
    e = jnp.exp(logits - m)
    return jnp.sum(e[:layer + 1], axis=0, keepdims=True) / jnp.sum(e, axis=0, keepdims=True)


def _hgrn_kernel(q_ref, ff_ref, fb_ref, v_ref, lf_ref, lb_ref, o_ref, *scratch, seq):
    scale = HGRN_DK ** -0.5
    lbs = (_lower_bound(lf_ref[0], 0), _lower_bound(lb_ref[0], 0))
    f_refs = (ff_ref, fb_ref)

    def prep(rows):
        hq = q_ref[0, rows, :].astype(F32)
        q = hq * jax.nn.sigmoid(hq) * scale
        out = []
        for d in range(2):
            f = lbs[d] + (1.0 - lbs[d]) * jax.nn.sigmoid(f_refs[d][0, rows, :].astype(F32))
            out.append((1.0 - f, jnp.log(f)))
        return q, out[0], out[1]

    _scan_core(prep, v_ref, o_ref, scratch, seq, HGRN_DK, HGRN_DV)


def _hgrn_scan(p3, logits_f, logits_b):
    bsz, seq, _ = p3.shape
    kb = lambda name: COL[name] // HGRN_DK
    nl = logits_f.shape[1]
    in_specs = [
        pl.BlockSpec((1, seq, HGRN_DK), lambda b, h: (b, 0, kb("h_q") + h)),
        pl.BlockSpec((1, seq, HGRN_DK), lambda b, h: (b, 0, kb("h_ff") + h)),
        pl.BlockSpec((1, seq, HGRN_DK), lambda b, h: (b, 0, kb("h_fb") + h)),
        pl.BlockSpec((1, seq, HGRN_DV), lambda b, h: (b, 0, kb("h_i") + h)),
        pl.BlockSpec((1, nl, HGRN_DK), lambda b, h: (h, 0, 0)),
        pl.BlockSpec((1, nl, HGRN_DK), lambda b, h: (h, 0, 0)),
    ]
    return pl.pallas_call(
        functools.partial(_hgrn_kernel, seq=seq),
        grid=(bsz, HGRN_HEADS),
        in_specs=in_specs,
        out_specs=pl.BlockSpec((1, seq, HGRN_DV), lambda b, h: (b, 0, h)),
        out_shape=jax.ShapeDtypeStruct((bsz, seq, HG_V), BF16),
        scratch_shapes=_scan_scratch(seq, HGRN_DK, HGRN_DV),
        compiler_params=pltpu.CompilerParams(
            dimension_semantics=("arbitrary", "arbitrary"), vmem_limit_bytes=VMEM_LIMIT),
        name="hgrn_scan",
    )(p3, p3, p3, p3, logits_f, logits_b)


def _head_rmsnorm_gate(o, gain, gate, heads, dv):
    parts = []
    for h in range(heads):
        seg = o[:, h * dv:(h + 1) * dv]
        ms = jnp.mean(seg * seg, axis=-1, keepdims=True)
        parts.append(seg * lax.rsqrt(ms + RMS_EPS) * gain)
    normed = jnp.concatenate(parts, axis=1)
    return (normed * (gate * jax.nn.sigmoid(gate))).astype(BF16)


def _merge_kernel(oa_ref, oh_ref, ag_ref, hg_ref, mg_ref, mh_ref, x_ref, wa_ref, wh_ref, wo_ref,
                  ga_ref, gh_ref, lng_ref, lnb_ref, out_ref):
    na = _head_rmsnorm_gate(oa_ref[...].astype(F32), ga_ref[...], ag_ref[...].astype(F32),
                            GLA_HEADS, GLA_DV)
    nh = _head_rmsnorm_gate(oh_ref[...].astype(F32), gh_ref[...], hg_ref[...].astype(F32),
                            HGRN_HEADS, HGRN_DV)
    y_gla = jnp.dot(na, wa_ref[...], preferred_element_type=F32)
    y_hgrn = jnp.dot(nh, wh_ref[...], preferred_element_type=F32)
    y = (jax.nn.sigmoid(mg_ref[...].astype(F32)) * y_gla
         + jax.nn.sigmoid(mh_ref[...].astype(F32)) * y_hgrn)
    y = jnp.dot(y.astype(BF16), wo_ref[...], preferred_element_type=F32)
    r = DEEPNORM_ALPHA * x_ref[...] + y
    mu = jnp.mean(r, axis=-1, keepdims=True)
    rc = r - mu
    var = jnp.mean(rc * rc, axis=-1, keepdims=True)
    out_ref[...] = rc * lax.rsqrt(var + LN_EPS) * lng_ref[...] + lnb_ref[...]


def _merge(o_gla, o_hgrn, p2, x2d, w_gla, w_hgrn, w_out, gain_a, gain_h, ln_g, ln_b):
    n = x2d.shape[0]
    tm = MERGE_TM
    wide = lambda name: pl.BlockSpec((tm, D_MODEL), lambda i, name=name: (i, COL[name] // D_MODEL))
    row = lambda: pl.BlockSpec((tm, D_MODEL), lambda i: (i, 0))
    full = lambda shape: pl.BlockSpec(shape, lambda i: (0,) * len(shape))
    return pl.pallas_call(
        _merge_kernel,
        grid=(n // tm,),
        in_specs=[row(), row(), wide("a_gate"), wide("h_gate"), wide("m_gla"), wide("m_hgrn"), row(),
                  full((GLA_V, D_MODEL)), full((HG_V, D_MODEL)), full((D_MODEL, D_MODEL)),
                  full((1, GLA_DV)), full((1, HGRN_DV)), full((1, D_MODEL)), full((1, D_MODEL))],
        out_specs=row(),
        out_shape=jax.ShapeDtypeStruct((n, D_MODEL), F32),
        compiler_params=pltpu.CompilerParams(
            dimension_semantics=("arbitrary",), vmem_limit_bytes=VMEM_LIMIT),
        name="merge",
    )(o_gla, o_hgrn, p2, p2, p2, p2, x2d, w_gla, w_hgrn, w_out, gain_a, gain_h, ln_g, ln_b)


def _regroup_w_in(w):
    w = w.astype(BF16)
    cols = [w[:, _ORIG_START[nm]:_ORIG_START[nm] + _SIZE[nm]] for nm in _NEW_ORDER]
    cols.append(jnp.zeros((w.shape[0], PROJ_PAD - _USED_COLS), w.dtype))
    return jnp.concatenate(cols, axis=1)


def _gla_gate_params(up_f, up_b, bias_f, bias_b, row_off):
    rank = up_f.shape[0]
    heads = lambda a: a.reshape(a.shape[0], GLA_HEADS, GLA_DK).transpose(1, 0, 2)
    up = jnp.zeros((GLA_HEADS, LANES, 2 * GLA_DK), F32)
    up = up.at[:, row_off:row_off + rank, :GLA_DK].set(heads(up_f))
    up = up.at[:, row_off + rank:row_off + 2 * rank, GLA_DK:].set(heads(up_b))
    bias = jnp.concatenate([heads(bias_f[None]), heads(bias_b[None])], axis=2)
    return up.astype(BF16), bias


def kernel(x, w_in, gla_gk_up_f, gla_gk_bias_f, gla_gk_up_b, gla_gk_bias_b, gla_norm_g,
           hgrn_lb_logits_f, hgrn_lb_logits_b, hgrn_norm_g, w_branch_gla, w_branch_hgrn,
           w_out, ln_g, ln_b):
    assert w_in.shape[0] == DEPTH == 1
    bsz, seq, d = x.shape
    assert d == D_MODEL and seq % SCAN_TILE == 0 and SCAN_TILE % CHUNK == 0
    x2d = x.astype(F32).reshape(bsz * seq, d)

    p2 = _project(x2d, _regroup_w_in(w_in[0]))
    p3 = p2.reshape(bsz, seq, PROJ_PAD)

    up, bias = _gla_gate_params(
        gla_gk_up_f[0].astype(F32), gla_gk_up_b[0].astype(F32),
        gla_gk_bias_f[0].astype(F32), gla_gk_bias_b[0].astype(F32), COL["lr_f"] % LANES)
    o_gla = _gla_scan(p3, up, bias)

    to_heads = lambda l: l.astype(F32).reshape(DEPTH + 1, HGRN_HEADS, HGRN_DK).transpose(1, 0, 2)
    o_hgrn = _hgrn_scan(p3, to_heads(hgrn_lb_logits_f), to_heads(hgrn_lb_logits_b))

    out = _merge(
        o_gla.reshape(bsz * seq, GLA_V), o_hgrn.reshape(bsz * seq, HG_V), p2, x2d,
        w_branch_gla[0].astype(BF16), w_branch_hgrn[0].astype(BF16), w_out[0].astype(BF16),
        gla_norm_g[0].astype(F32).reshape(1, GLA_DV), hgrn_norm_g[0].astype(F32).reshape(1, HGRN_DV),
        ln_g[0].astype(F32).reshape(1, D_MODEL), ln_b[0].astype(F32).reshape(1, D_MODEL))
    return out.reshape(bsz, seq, d).astype(x.dtype)
```

```python
import functools

import jax
import jax.numpy as jnp
import numpy as np
from jax import lax
from jax.experimental import pallas as pl
from jax.experimental.pallas import tpu as pltpu

F32 = jnp.float32
BF16 = jnp.bfloat16

D_MODEL = 1024
DEPTH = 1
GLA_HEADS = 4
GLA_DK = 128
GLA_DV = 256
GLA_GATE_RANK = 16
GLA_GATE_NORMALIZER = 16.0
HGRN_HEADS = 8
HGRN_DK = 128
HGRN_DV = 128
GLA_QK = GLA_HEADS * GLA_DK
GLA_V = GLA_HEADS * GLA_DV
HG_K = HGRN_HEADS * HGRN_DK
HG_V = HGRN_HEADS * HGRN_DV
RMS_EPS = 1e-6
LN_EPS = 1e-5
DEEPNORM_ALPHA = (2.0 * DEPTH) ** 0.25

LANES = 128

_ORIG_SIZES = (GLA_QK, GLA_QK, GLA_V, GLA_V, GLA_GATE_RANK, GLA_GATE_RANK,
               HG_K, HG_K, HG_K, HG_V, HG_V, D_MODEL, D_MODEL)
_ORIG_NAMES = ("a_q", "a_k", "a_v", "a_gate", "lr_f", "lr_b",
               "h_q", "h_ff", "h_fb", "h_i", "h_gate", "m_gla", "m_hgrn")
_NEW_ORDER = ("a_q", "a_k", "a_v", "a_gate", "h_q", "h_ff", "h_fb", "h_i", "h_gate",
              "m_gla", "m_hgrn", "lr_f", "lr_b")


def _column_plan():
    starts = dict(zip(_ORIG_NAMES, np.cumsum((0,) + _ORIG_SIZES[:-1]).tolist()))
    sizes = dict(zip(_ORIG_NAMES, _ORIG_SIZES))
    new_off = {}
    off = 0
    for name in _NEW_ORDER:
        new_off[name] = off
        off += sizes[name]
    padded = -(-off // LANES) * LANES
    return starts, sizes, new_off, off, padded


_ORIG_START, _SIZE, COL, _USED_COLS, PROJ_PAD = _column_plan()

CHUNK = 64
SCAN_TILE = 512
CUMSUM_ROWS = 256
PHASE2_UNROLL = 4
PROJ_TM = 1024
PROJ_TN = 1152
MERGE_TM = 512
VMEM_LIMIT = 56 * 1024 * 1024


def _proj_kernel(x_ref, w_ref, o_ref):
    o_ref[...] = jnp.dot(x_ref[...].astype(BF16), w_ref[...],
                         preferred_element_type=F32).astype(BF16)


def _project(x2d, w_pad):
    n = x2d.shape[0]
    assert n % PROJ_TM == 0 and PROJ_PAD % PROJ_TN == 0
    return pl.pallas_call(
        _proj_kernel,
        grid=(n // PROJ_TM, PROJ_PAD // PROJ_TN),
        in_specs=[pl.BlockSpec((PROJ_TM, D_MODEL), lambda i, j: (i, 0)),
                  pl.BlockSpec((D_MODEL, PROJ_TN), lambda i, j: (0, j))],
        out_specs=pl.BlockSpec((PROJ_TM, PROJ_TN), lambda i, j: (i, j)),
        out_shape=jax.ShapeDtypeStruct((n, PROJ_PAD), BF16),
        compiler_params=pltpu.CompilerParams(
            dimension_semantics=("arbitrary", "arbitrary"), vmem_limit_bytes=VMEM_LIMIT),
        name="proj",
    )(x2d, w_pad)


def _split_bf16(g):
    hi = g.astype(BF16)
    lo = (g - hi.astype(F32)).astype(BF16)
    return jnp.concatenate([hi, lo], axis=1)


def _dot_nt(a, b):
    return lax.dot_general(a, b, (((1,), (1,)), ((), ())), preferred_element_type=F32)


def _dot_tn(a, b):
    return lax.dot_general(a, b, (((0,), (0,)), ((), ())), preferred_element_type=F32)


def _block_diag_cumsum_mats(tile, c):
    row = lax.broadcasted_iota(jnp.int32, (tile, tile), 0)
    col = lax.broadcasted_iota(jnp.int32, (tile, tile), 1)
    same = (row // c) == (col // c)
    one = lambda m: jnp.where(m, 1.0, 0.0).astype(BF16)
    return one(same & (col <= row)), one(same & (col >= row))


def _scan_core(prep, v_ref, o_ref, scratch, seq, dk, dv):
    qi_refs, upd_refs, dec_refs, st_refs = scratch[0:2], scratch[2:4], scratch[4:6], scratch[6:8]
    oacc_ref, sq_ref, sk_refs, sb_refs = scratch[8], scratch[9], scratch[10:12], scratch[12:14]
    c = CHUNK
    tile = SCAN_TILE
    per_tile = tile // c
    n_tiles = seq // tile
    n = seq // c
    mid = c // 2
    row = lax.broadcasted_iota(jnp.int32, (c, c), 0)
    col = lax.broadcasted_iota(jnp.int32, (c, c), 1)
    masks = (col <= row, col >= row)
    cums = _block_diag_cumsum_mats(CUMSUM_ROWS, c)
    end_rows = (c - 1, 0)

    def gates(t):
        return prep(pl.ds(pl.multiple_of(t * tile, tile), tile))

    def stage(q, kg_f, kg_b):
        sq_ref[...] = q
        for d, (k, g) in enumerate((kg_f, kg_b)):
            sk_refs[d][...] = k
            for r in range(0, tile, CUMSUM_ROWS):
                bb = jnp.dot(cums[d], _split_bf16(g[r:r + CUMSUM_ROWS]), preferred_element_type=F32)
                sb_refs[d][r:r + CUMSUM_ROWS, :] = bb[:, :dk] + bb[:, dk:]

    stage(*gates(0))

    def phase1(i, carry):
        q = sq_ref[...]
        ks = (sk_refs[0][...], sk_refs[1][...])
        bs = (sb_refs[0][...], sb_refs[1][...])
        v = v_ref[0, pl.ds(pl.multiple_of(i * tile, tile), tile), :]
        slices = [slice(j * c, (j + 1) * c) for j in range(per_tile)]
        mixed, kps = [], []
        for j, sl in enumerate(slices):
            chunk_rows = pl.ds(pl.multiple_of(i * tile + j * c, c), c)
            acc, kp_pair = None, []
            for d in range(2):
                b = bs[d][sl]
                b_mid = b[mid:mid + 1, :]
                b_end = b[end_rows[d]:end_rows[d] + 1, :]
                rel = b - b_mid
                qm = q[sl] * jnp.exp(rel)
                km = ks[d][sl] * jnp.exp(-rel)
                scores = jnp.where(masks[d], _dot_nt(qm.astype(BF16), km.astype(BF16)), 0.0)
                acc = scores if acc is None else acc + scores
                qi_refs[d][chunk_rows, :] = (qm * jnp.exp(b_mid)).astype(BF16)
                kp_pair.append((km * jnp.exp(b_end - b_mid)).astype(BF16))
                dec_refs[d][i * per_tile + j] = jnp.broadcast_to(jnp.exp(b_end), (LANES, dk)).T
            mixed.append(acc.astype(BF16))
            kps.append(jnp.concatenate(kp_pair, axis=1))
        next_gates = gates(jnp.minimum(i + 1, n_tiles - 1))
        for j, sl in enumerate(slices):
            upd = _dot_tn(kps[j], v[sl])
            upd_refs[0][i * per_tile + j] = upd[:dk]
            upd_refs[1][i * per_tile + j] = upd[dk:]
        stage(*next_gates)
        for j, sl in enumerate(slices):
            chunk_rows = pl.ds(pl.multiple_of(i * tile + j * c, c), c)
            oacc_ref[chunk_rows, :] = jnp.dot(mixed[j], v[sl], preferred_element_type=F32)
        return carry

    lax.fori_loop(0, n_tiles, phase1, 0)

    for d in range(2):
        st_refs[d][...] = jnp.zeros_like(st_refs[d])

    def phase2(i, carry):
        for d, ci in enumerate((i, n - 1 - i)):
            rows = pl.ds(pl.multiple_of(ci * c, c), c)
            st = st_refs[d][...]
            oacc_ref[rows, :] += jnp.dot(qi_refs[d][rows, :], st.astype(BF16),
                                         preferred_element_type=F32)
            decay = jnp.concatenate([dec_refs[d][ci]] * (dv // LANES), axis=1)
            st_refs[d][...] = st * decay + upd_refs[d][ci]
        return carry

    lax.fori_loop(0, n, phase2, 0, unroll=PHASE2_UNROLL)
    o_ref[0] = oacc_ref[...].astype(o_ref.dtype)


def _scan_scratch(seq, dk, dv):
    per_dir = lambda shape, dt: [pltpu.VMEM(shape, dt), pltpu.VMEM(shape, dt)]
    n = seq // CHUNK
    return (per_dir((seq, dk), BF16)
            + per_dir((n, dk, dv), F32)
            + per_dir((n, dk, LANES), F32)
            + per_dir((dk, dv), F32)
            + [pltpu.VMEM((seq, dv), F32)]
            + [pltpu.VMEM((SCAN_TILE, dk), F32)]
            + per_dir((SCAN_TILE, dk), F32)
            + per_dir((SCAN_TILE, dk), F32))


def _log_sigmoid(z):
    return jnp.minimum(z, 0.0) - jnp.log(1.0 + jnp.exp(-jnp.abs(z)))


def _gla_kernel(q_ref, k_ref, v_ref, lr_ref, up_ref, bias_ref, o_ref, *scratch, seq):
    scale = GLA_DK ** -0.5
    up = up_ref[0]
    bias = bias_ref[0]

    def prep(rows):
        q = q_ref[0, rows, :].astype(F32) * scale
        k = k_ref[0, rows, :].astype(F32)
        z = jnp.dot(lr_ref[0, rows, :], up, preferred_element_type=F32) + bias
        g = _log_sigmoid(z) * (1.0 / GLA_GATE_NORMALIZER)
        return q, (k, g[:, :GLA_DK]), (k, g[:, GLA_DK:])

    _scan_core(prep, v_ref, o_ref, scratch, seq, GLA_DK, GLA_DV)


def _gla_scan(p3, up, bias):
    bsz, seq, _ = p3.shape
    kb = lambda name: COL[name] // GLA_DK
    in_specs = [
        pl.BlockSpec((1, seq, GLA_DK), lambda b, h: (b, 0, kb("a_q") + h)),
        pl.BlockSpec((1, seq, GLA_DK), lambda b, h: (b, 0, kb("a_k") + h)),
        pl.BlockSpec((1, seq, GLA_DV), lambda b, h: (b, 0, COL["a_v"] // GLA_DV + h)),
        pl.BlockSpec((1, seq, LANES), lambda b, h: (b, 0, COL["lr_f"] // LANES)),
        pl.BlockSpec((1, LANES, 2 * GLA_DK), lambda b, h: (h, 0, 0)),
        pl.BlockSpec((1, 1, 2 * GLA_DK), lambda b, h: (h, 0, 0)),
    ]
    return pl.pallas_call(
        functools.partial(_gla_kernel, seq=seq),
        grid=(bsz, GLA_HEADS),
        in_specs=in_specs,
        out_specs=pl.BlockSpec((1, seq, GLA_DV), lambda b, h: (b, 0, h)),
        out_shape=jax.ShapeDtypeStruct((bsz, seq, GLA_V), BF16),
        scratch_shapes=_scan_scratch(seq, GLA_DK, GLA_DV),
        compiler_params=pltpu.CompilerParams(
            dimension_semantics=("arbitrary", "arbitrary"), vmem_limit_bytes=VMEM_LIMIT),
        name="gla_scan",
    )(p3, p3, p3, p3, up, bias)


def _lower_bound(logits, layer):
    m = jnp.max(logits, axis=0, keepdims=True)
    e = jnp.exp(logits - m)
    return jnp.sum(e[:layer + 1], axis=0, keepdims=True) / jnp.sum(e, axis=0, keepdims=True)


def _hgrn_kernel(q_ref, ff_ref, fb_ref, v_ref, lf_ref, lb_ref, o_ref, *scratch, seq):
    scale = HGRN_DK ** -0.5
    lbs = (_lower_bound(lf_ref[0], 0), _lower_bound(lb_ref[0], 0))
    f_refs = (ff_ref, fb_ref)

    def prep(rows):
        hq = q_ref[0, rows, :].astype(F32)
        q = hq * jax.nn.sigmoid(hq) * scale
        out = []
        for d in range(2):
            f = lbs[d] + (1.0 - lbs[d]) * jax.nn.sigmoid(f_refs[d][0, rows, :].astype(F32))
            out.append((1.0 - f, jnp.log(f)))
        return q, out[0], out[1]

    _scan_core(prep, v_ref, o_ref, scratch, seq, HGRN_DK, HGRN_DV)


def _hgrn_scan(p3, logits_f, logits_b):
    bsz, seq, _ = p3.shape
    kb = lambda name: COL[name] // HGRN_DK
    nl = logits_f.shape[1]
    in_specs = [
        pl.BlockSpec((1, seq, HGRN_DK), lambda b, h: (b, 0, kb("h_q") + h)),
        pl.BlockSpec((1, seq, HGRN_DK), lambda b, h: (b, 0, kb("h_ff") + h)),
        pl.BlockSpec((1, seq, HGRN_DK), lambda b, h: (b, 0, kb("h_fb") + h)),
        pl.BlockSpec((1, seq, HGRN_DV), lambda b, h: (b, 0, kb("h_i") + h)),
        pl.BlockSpec((1, nl, HGRN_DK), lambda b, h: (h, 0, 0)),
        pl.BlockSpec((1, nl, HGRN_DK), lambda b, h: (h, 0, 0)),
    ]
    return pl.pallas_call(
        functools.partial(_hgrn_kernel, seq=seq),
        grid=(bsz, HGRN_HEADS),
        in_specs=in_specs,
        out_specs=pl.BlockSpec((1, seq, HGRN_DV), lambda b, h: (b, 0, h)),
        out_shape=jax.ShapeDtypeStruct((bsz, seq, HG_V), BF16),
        scratch_shapes=_scan_scratch(seq, HGRN_DK, HGRN_DV),
        compiler_params=pltpu.CompilerParams(
            dimension_semantics=("arbitrary", "arbitrary"), vmem_limit_bytes=VMEM_LIMIT),
        name="hgrn_scan",
    )(p3, p3, p3, p3, logits_f, logits_b)


def _head_rmsnorm_gate(o, gain, gate, heads, dv):
    parts = []
    for h in range(heads):
        seg = o[:, h * dv:(h + 1) * dv]
        ms = jnp.mean(seg * seg, axis=-1, keepdims=True)
        parts.append(seg * lax.rsqrt(ms + RMS_EPS) * gain)
    normed = jnp.concatenate(parts, axis=1)
    return (normed * (gate * jax.nn.sigmoid(gate))).astype(BF16)


def _merge_kernel(oa_ref, oh_ref, ag_ref, hg_ref, mg_ref, mh_ref, x_ref, wa_ref, wh_ref, wo_ref,
                  ga_ref, gh_ref, lng_ref, lnb_ref, out_ref):
    na = _head_rmsnorm_gate(oa_ref[...].astype(F32), ga_ref[...], ag_ref[...].astype(F32),
                            GLA_HEADS, GLA_DV)
    nh = _head_rmsnorm_gate(oh_ref[...].astype(F32), gh_ref[...], hg_ref[...].astype(F32),
                            HGRN_HEADS, HGRN_DV)
    y_gla = jnp.dot(na, wa_ref[...], preferred_element_type=F32)
    y_hgrn = jnp.dot(nh, wh_ref[...], preferred_element_type=F32)
    y = (jax.nn.sigmoid(mg_ref[...].astype(F32)) * y_gla
         + jax.nn.sigmoid(mh_ref[...].astype(F32)) * y_hgrn)
    y = jnp.dot(y.astype(BF16), wo_ref[...], preferred_element_type=F32)
    r = DEEPNORM_ALPHA * x_ref[...] + y
    mu = jnp.mean(r, axis=-1, keepdims=True)
    rc = r - mu
    var = jnp.mean(rc * rc, axis=-1, keepdims=True)
    out_ref[...] = rc * lax.rsqrt(var + LN_EPS) * lng_ref[...] + lnb_ref[...]


def _merge(o_gla, o_hgrn, p2, x2d, w_gla, w_hgrn, w_out, gain_a, gain_h, ln_g, ln_b):
    n = x2d.shape[0]
    tm = MERGE_TM
    wide = lambda name: pl.BlockSpec((tm, D_MODEL), lambda i, name=name: (i, COL[name] // D_MODEL))
    row = lambda: pl.BlockSpec((tm, D_MODEL), lambda i: (i, 0))
    full = lambda shape: pl.BlockSpec(shape, lambda i: (0,) * len(shape))
    return pl.pallas_call(
        _merge_kernel,
        grid=(n // tm,),
        in_specs=[row(), row(), wide("a_gate"), wide("h_gate"), wide("m_gla"), wide("m_hgrn"), row(),
                  full((GLA_V, D_MODEL)), full((HG_V, D_MODEL)), full((D_MODEL, D_MODEL)),
                  full((1, GLA_DV)), full((1, HGRN_DV)), full((1, D_MODEL)), full((1, D_MODEL))],
        out_specs=row(),
        out_shape=jax.ShapeDtypeStruct((n, D_MODEL), F32),
        compiler_params=pltpu.CompilerParams(
            dimension_semantics=("arbitrary",), vmem_limit_bytes=VMEM_LIMIT),
        name="merge",
    )(o_gla, o_hgrn, p2, p2, p2, p2, x2d, w_gla, w_hgrn, w_out, gain_a, gain_h, ln_g, ln_b)


def _regroup_w_in(w):
    lo, hi = _ORIG_START["lr_f"], _ORIG_START["lr_b"] + _SIZE["lr_b"]
    assert [COL[nm] for nm in _NEW_ORDER[:4]] == [_ORIG_START[nm] for nm in _NEW_ORDER[:4]]
    w = w.astype(BF16)
    pad = jnp.zeros((w.shape[0], PROJ_PAD - _USED_COLS), w.dtype)
    return jnp.concatenate([w[:, :lo], w[:, hi:], w[:, lo:hi], pad], axis=1)


def _gla_gate_params(up_f, up_b, bias_f, bias_b, row_off):
    rank = up_f.shape[0]
    heads = lambda a: a.reshape(a.shape[0], GLA_HEADS, GLA_DK).transpose(1, 0, 2)
    up = jnp.zeros((GLA_HEADS, LANES, 2 * GLA_DK), F32)
    up = up.at[:, row_off:row_off + rank, :GLA_DK].set(heads(up_f))
    up = up.at[:, row_off + rank:row_off + 2 * rank, GLA_DK:].set(heads(up_b))
    bias = jnp.concatenate([heads(bias_f[None]), heads(bias_b[None])], axis=2)
    return up.astype(BF16), bias


def kernel(x, w_in, gla_gk_up_f, gla_gk_bias_f, gla_gk_up_b, gla_gk_bias_b, gla_norm_g,
           hgrn_lb_logits_f, hgrn_lb_logits_b, hgrn_norm_g, w_branch_gla, w_branch_hgrn,
           w_out, ln_g, ln_b):
    assert w_in.shape[0] == DEPTH == 1
    bsz, seq, d = x.shape
    assert d == D_MODEL and seq % SCAN_TILE == 0 and SCAN_TILE % CHUNK == 0
    x2d = x.astype(F32).reshape(bsz * seq, d)

    p2 = _project(x2d, _regroup_w_in(w_in.reshape(w_in.shape[1:])))
    p3 = p2.reshape(bsz, seq, PROJ_PAD)

    up, bias = _gla_gate_params(
        gla_gk_up_f[0].astype(F32), gla_gk_up_b[0].astype(F32),
        gla_gk_bias_f[0].astype(F32), gla_gk_bias_b[0].astype(F32), COL["lr_f"] % LANES)
    o_gla = _gla_scan(p3, up, bias)

    to_heads = lambda l: l.astype(F32).reshape(DEPTH + 1, HGRN_HEADS, HGRN_DK).transpose(1, 0, 2)
    o_hgrn = _hgrn_scan(p3, to_heads(hgrn_lb_logits_f), to_heads(hgrn_lb_logits_b))

    out = _merge(
        o_gla.reshape(bsz * seq, GLA_V), o_hgrn.reshape(bsz * seq, HG_V), p2, x2d,
        w_branch_gla[0].astype(BF16), w_branch_hgrn[0].astype(BF16), w_out[0].astype(BF16),
        gla_norm_g[0].astype(F32).reshape(1, GLA_DV), hgrn_norm_g[0].astype(F32).reshape(1, HGRN_DV),
        ln_g[0].astype(F32).reshape(1, D_MODEL), ln_b[0].astype(F32).reshape(1, D_MODEL))
    return out.reshape(bsz, seq, d).astype(x.dtype)
```

```python
import functools

import jax
import jax.numpy as jnp
import numpy as np
from jax import lax
from jax.experimental import pallas as pl
from jax.experimental.pallas import tpu as pltpu

F32 = jnp.float32
BF16 = jnp.bfloat16

D_MODEL = 1024
DEPTH = 1
GLA_HEADS = 4
GLA_DK = 128
GLA_DV = 256
GLA_GATE_RANK = 16
GLA_GATE_NORMALIZER = 16.0
HGRN_HEADS = 8
HGRN_DK = 128
HGRN_DV = 128
GLA_QK = GLA_HEADS * GLA_DK
GLA_V = GLA_HEADS * GLA_DV
HG_K = HGRN_HEADS * HGRN_DK
HG_V = HGRN_HEADS * HGRN_DV
RMS_EPS = 1e-6
LN_EPS = 1e-5
DEEPNORM_ALPHA = (2.0 * DEPTH) ** 0.25
LOG2E = 1.4426950408889634

LANES = 128

_ORIG_SIZES = (GLA_QK, GLA_QK, GLA_V, GLA_V, GLA_GATE_RANK, GLA_GATE_RANK,
               HG_K, HG_K, HG_K, HG_V, HG_V, D_MODEL, D_MODEL)
_ORIG_NAMES = ("a_q", "a_k", "a_v", "a_gate", "lr_f", "lr_b",
               "h_q", "h_ff", "h_fb", "h_i", "h_gate", "m_gla", "m_hgrn")
_ORIG_START = dict(zip(_ORIG_NAMES, np.cumsum((0,) + _ORIG_SIZES[:-1]).tolist()))
_SIZE = dict(zip(_ORIG_NAMES, _ORIG_SIZES))
_LR_LO = _ORIG_START["lr_f"]
_LR_HI = _ORIG_START["lr_b"] + _SIZE["lr_b"]

_OUT_ORDER = ("a_q", "a_k", "a_v", "a_gate", "g_f", "g_b",
              "h_q", "h_ff", "h_fb", "h_i", "h_gate", "m_gla", "m_hgrn")
_OUT_SIZE = dict(_SIZE, g_f=GLA_QK, g_b=GLA_QK)
COL = dict(zip(_OUT_ORDER, np.cumsum([0] + [_OUT_SIZE[nm] for nm in _OUT_ORDER[:-1]]).tolist()))
PROJ_COLS = sum(_OUT_SIZE[nm] for nm in _OUT_ORDER)

CHUNK = 64
SCAN_TILE = 512
CUMSUM_ROWS = 256
PHASE2_UNROLL = 4
PROJ_TM = 512
MERGE_TM = 512
VMEM_LIMIT = 60 * 1024 * 1024


def _sigmoid(x):
    return 1.0 / (1.0 + jnp.exp2(x * (-LOG2E)))


def _lower_bound(logits, layer):
    m = jnp.max(logits, axis=0, keepdims=True)
    e = jnp.exp(logits - m)
    return jnp.sum(e[:layer + 1], axis=0, keepdims=True) / jnp.sum(e, axis=0, keepdims=True)


def _proj_kernel(x_ref, wh_ref, wt_ref, wlr_ref, up_ref, gbias_ref, lf_ref, lb_ref, o_ref):
    xb = x_ref[...].astype(BF16)
    mm = lambda w: jnp.dot(xb, w, preferred_element_type=F32)
    silu = lambda a: a * _sigmoid(a)

    def put(name, val):
        o_ref[:, COL[name]:COL[name] + val.shape[1]] = val.astype(BF16)

    def head(name):
        lo = _ORIG_START[name]
        return mm(wh_ref[:, lo:lo + _SIZE[name]])

    def tail(name):
        lo = _ORIG_START[name] - _LR_HI
        return mm(wt_ref[:, lo:lo + _SIZE[name]])

    def log2_forget(h, logits_ref):
        lb = _lower_bound(logits_ref[...], 0)
        return jnp.log2(lb + (1.0 - lb) * _sigmoid(h))

    put("a_q", head("a_q") * (GLA_DK ** -0.5))
    put("a_k", head("a_k"))
    put("a_v", head("a_v"))
    put("a_gate", silu(head("a_gate")))
    z = jnp.dot(mm(wlr_ref[...]).astype(BF16), up_ref[...], preferred_element_type=F32) + gbias_ref[...]
    log_sig = jnp.minimum(z, 0.0) - jnp.log2(1.0 + jnp.exp2(jnp.abs(z) * (-LOG2E))) * (1.0 / LOG2E)
    put("g_f", log_sig * (LOG2E / GLA_GATE_NORMALIZER))
    put("h_q", silu(tail("h_q")) * (HGRN_DK ** -0.5))
    put("h_ff", log2_forget(tail("h_ff"), lf_ref))
    put("h_fb", log2_forget(tail("h_fb"), lb_ref))
    put("h_i", tail("h_i"))
    put("h_gate", silu(tail("h_gate")))
    put("m_gla", _sigmoid(tail("m_gla")))
    put("m_hgrn", _sigmoid(tail("m_hgrn")))


def _project(x2d, w_head, w_tail, w_lr, up, gbias, logits_f, logits_b):
    n = x2d.shape[0]
    assert n % PROJ_TM == 0
    resident = lambda a: pl.BlockSpec(a.shape, lambda i: (0,) * a.ndim,
                                      pipeline_mode=pl.Buffered(1))
    consts = (w_head, w_tail, w_lr, up, gbias, logits_f, logits_b)
    return pl.pallas_call(
        _proj_kernel,
        grid=(n // PROJ_TM,),
        in_specs=[pl.BlockSpec((PROJ_TM, D_MODEL), lambda i: (i, 0))] + [resident(a) for a in consts],
        out_specs=pl.BlockSpec((PROJ_TM, PROJ_COLS), lambda i: (i, 0)),
        out_shape=jax.ShapeDtypeStruct((n, PROJ_COLS), BF16),
        compiler_params=pltpu.CompilerParams(
            dimension_semantics=("arbitrary",), vmem_limit_bytes=VMEM_LIMIT),
        name="proj",
    )(x2d, *consts)


def _dot_nt(a, b):
    return lax.dot_general(a, b, (((1,), (1,)), ((), ())), preferred_element_type=F32)


def _dot_tn(a, b):
    return lax.dot_general(a, b, (((0,), (0,)), ((), ())), preferred_element_type=F32)


def _block_diag_cumsum_mat(rows, c):
    row = lax.broadcasted_iota(jnp.int32, (rows, rows), 0)
    col = lax.broadcasted_iota(jnp.int32, (rows, rows), 1)
    return jnp.where(((row // c) == (col // c)) & (col <= row), 1.0, 0.0).astype(BF16)


def _scan_core(q_ref, k_of, g_refs, v_ref, o_ref, scratch, seq, dk, dv):
    qi_refs, upd_refs, dec_refs, st_refs = scratch[0:2], scratch[2:4], scratch[4:6], scratch[6:8]
    oacc_ref, cum_ref = scratch[8], scratch[9]
    c = CHUNK
    tile = SCAN_TILE
    per_tile = tile // c
    n_tiles = seq // tile
    n = seq // c
    mid = c // 2
    row = lax.broadcasted_iota(jnp.int32, (c, c), 0)
    col = lax.broadcasted_iota(jnp.int32, (c, c), 1)
    masks = (col <= row, col >= row)
    cum_mat = _block_diag_cumsum_mat(CUMSUM_ROWS, c)

    def tile_rows(t):
        return pl.ds(pl.multiple_of(t * tile, tile), tile)

    def stage(t):
        g_pair = jnp.concatenate([g_refs[0][0, tile_rows(t), :], g_refs[1][0, tile_rows(t), :]], axis=1)
        for r in range(0, tile, CUMSUM_ROWS):
            cum_ref[r:r + CUMSUM_ROWS, :] = jnp.dot(cum_mat, g_pair[r:r + CUMSUM_ROWS],
                                                    preferred_element_type=F32)

    stage(0)

    def phase1(i, carry):
        rows = tile_rows(i)
        q = q_ref[0, rows, :].astype(F32)
        gs = (g_refs[0][0, rows, :].astype(F32), g_refs[1][0, rows, :].astype(F32))
        ks = (k_of(0, gs[0], rows), k_of(1, gs[1], rows))
        cum = cum_ref[...]
        v = v_ref[0, rows, :]
        slices = [slice(j * c, (j + 1) * c) for j in range(per_tile)]
        mixed, kps = [], []
        for j, sl in enumerate(slices):
            chunk_rows = pl.ds(pl.multiple_of(i * tile + j * c, c), c)
            acc, kp_pair = None, []
            for d in range(2):
                if d == 0:
                    b = cum[sl, :dk]
                    b_mid = b[mid:mid + 1, :]
                    total = b[c - 1:c, :]
                    rel, lead, trail = b - b_mid, b_mid, total - b_mid
                else:
                    incl = cum[sl, dk:]
                    e = incl - gs[1][sl]
                    e_mid = e[mid:mid + 1, :]
                    total = incl[c - 1:c, :]
                    rel, lead, trail = e_mid - e, total - e_mid, e_mid
                qm = q[sl] * jnp.exp2(rel)
                km = ks[d][sl] * jnp.exp2(-rel)
                scores = jnp.where(masks[d], _dot_nt(qm.astype(BF16), km.astype(BF16)), 0.0)
                acc = scores if acc is None else acc + scores
                qi_refs[d][chunk_rows, :] = (qm * jnp.exp2(lead)).astype(BF16)
                kp_pair.append((km * jnp.exp2(trail)).astype(BF16))
                dec_refs[d][i * per_tile + j] = jnp.broadcast_to(jnp.exp2(total), (LANES, dk)).T
            mixed.append(acc.astype(BF16))
            kps.append(jnp.concatenate(kp_pair, axis=1))
        for j, sl in enumerate(slices):
            upd = _dot_tn(kps[j], v[sl])
            upd_refs[0][i * per_tile + j] = upd[:dk]
            upd_refs[1][i * per_tile + j] = upd[dk:]
        stage(jnp.minimum(i + 1, n_tiles - 1))
        for j, sl in enumerate(slices):
            chunk_rows = pl.ds(pl.multiple_of(i * tile + j * c, c), c)
            oacc_ref[chunk_rows, :] = jnp.dot(mixed[j], v[sl], preferred_element_type=F32)
        return carry

    lax.fori_loop(0, n_tiles, phase1, 0)

    for d in range(2):
        st_refs[d][...] = jnp.zeros_like(st_refs[d])

    def phase2(i, carry):
        for d, ci in enumerate((i, n - 1 - i)):
            rows = pl.ds(pl.multiple_of(ci * c, c), c)
            st = st_refs[d][...]
            oacc_ref[rows, :] += jnp.dot(qi_refs[d][rows, :], st.astype(BF16),
                                         preferred_element_type=F32)
            decay = jnp.concatenate([dec_refs[d][ci]] * (dv // LANES), axis=1)
            st_refs[d][...] = st * decay + upd_refs[d][ci]
        return carry

    lax.fori_loop(0, n, phase2, 0, unroll=PHASE2_UNROLL)
    o_ref[0] = oacc_ref[...].astype(o_ref.dtype)


def _scan_scratch(seq, dk, dv):
    per_dir = lambda shape, dt: [pltpu.VMEM(shape, dt), pltpu.VMEM(shape, dt)]
    n = seq // CHUNK
    return (per_dir((seq, dk), BF16)
            + per_dir((n, dk, dv), F32)
            + per_dir((n, dk, LANES), F32)
            + per_dir((dk, dv), F32)
            + [pltpu.VMEM((seq, dv), F32)]
            + [pltpu.VMEM((SCAN_TILE, 2 * dk), F32)])


def _scan_call(kernel_fn, name, p3, specs, heads, dk, dv):
    bsz, seq, _ = p3.shape
    return pl.pallas_call(
        functools.partial(kernel_fn, seq=seq),
        grid=(bsz, heads),
        in_specs=specs,
        out_specs=pl.BlockSpec((1, seq, dv), lambda b, h: (b, 0, h)),
        out_shape=jax.ShapeDtypeStruct((bsz, seq, heads * dv), BF16),
        scratch_shapes=_scan_scratch(seq, dk, dv),
        compiler_params=pltpu.CompilerParams(
            dimension_semantics=("arbitrary", "arbitrary"), vmem_limit_bytes=VMEM_LIMIT),
        name=name,
    )(*([p3] * len(specs)))


def _head_block(seq, name, width):
    first = COL[name] // width
    return pl.BlockSpec((1, seq, width), lambda b, h: (b, 0, first + h))


def _gla_kernel(q_ref, k_ref, gf_ref, gb_ref, v_ref, o_ref, *scratch, seq):
    k_of = lambda d, g, rows: k_ref[0, rows, :].astype(F32)
    _scan_core(q_ref, k_of, (gf_ref, gb_ref), v_ref, o_ref, scratch, seq, GLA_DK, GLA_DV)


def _gla_scan(p3):
    seq = p3.shape[1]
    specs = [_head_block(seq, "a_q", GLA_DK), _head_block(seq, "a_k", GLA_DK),
             _head_block(seq, "g_f", GLA_DK), _head_block(seq, "g_b", GLA_DK),
             _head_block(seq, "a_v", GLA_DV)]
    return _scan_call(_gla_kernel, "gla_scan", p3, specs, GLA_HEADS, GLA_DK, GLA_DV)


def _hgrn_kernel(q_ref, gf_ref, gb_ref, v_ref, o_ref, *scratch, seq):
    k_of = lambda d, g, rows: 1.0 - jnp.exp2(g)
    _scan_core(q_ref, k_of, (gf_ref, gb_ref), v_ref, o_ref, scratch, seq, HGRN_DK, HGRN_DV)


def _hgrn_scan(p3):
    seq = p3.shape[1]
    specs = [_head_block(seq, "h_q", HGRN_DK), _head_block(seq, "h_ff", HGRN_DK),
             _head_block(seq, "h_fb", HGRN_DK), _head_block(seq, "h_i", HGRN_DV)]
    return _scan_call(_hgrn_kernel, "hgrn_scan", p3, specs, HGRN_HEADS, HGRN_DK, HGRN_DV)


def _head_rmsnorm_gate(o, gain, gate, heads, dv):
    parts = []
    for h in range(heads):
        seg = o[:, h * dv:(h + 1) * dv]
        ms = jnp.mean(seg * seg, axis=-1, keepdims=True)
        parts.append(seg * lax.rsqrt(ms + RMS_EPS) * gain)
    return (jnp.concatenate(parts, axis=1) * gate).astype(BF16)


def _merge_kernel(oa_ref, oh_ref, ag_ref, hg_ref, mg_ref, mh_ref, x_ref, wa_ref, wh_ref, wo_ref,
                  ga_ref, gh_ref, lng_ref, lnb_ref, out_ref):
    na = _head_rmsnorm_gate(oa_ref[...].astype(F32), ga_ref[...], ag_ref[...].astype(F32),
                            GLA_HEADS, GLA_DV)
    nh = _head_rmsnorm_gate(oh_ref[...].astype(F32), gh_ref[...], hg_ref[...].astype(F32),
                            HGRN_HEADS, HGRN_DV)
    y_gla = jnp.dot(na, wa_ref[...], preferred_element_type=F32)
    y_hgrn = jnp.dot(nh, wh_ref[...], preferred_element_type=F32)
    y = mg_ref[...].astype(F32) * y_gla + mh_ref[...].astype(F32) * y_hgrn
    y = jnp.dot(y.astype(BF16), wo_ref[...], preferred_element_type=F32)
    r = DEEPNORM_ALPHA * x_ref[...] + y
    mu = jnp.mean(r, axis=-1, keepdims=True)
    rc = r - mu
    var = jnp.mean(rc * rc, axis=-1, keepdims=True)
    out_ref[...] = rc * lax.rsqrt(var + LN_EPS) * lng_ref[...] + lnb_ref[...]


def _merge(o_gla, o_hgrn, p2, x2d, w_gla, w_hgrn, w_out, gain_a, gain_h, ln_g, ln_b):
    n = x2d.shape[0]
    tm = MERGE_TM
    wide = lambda name: pl.BlockSpec((tm, D_MODEL), lambda i, name=name: (i, COL[name] // D_MODEL))
    row = lambda: pl.BlockSpec((tm, D_MODEL), lambda i: (i, 0))
    full = lambda shape: pl.BlockSpec(shape, lambda i: (0,) * len(shape))
    return pl.pallas_call(
        _merge_kernel,
        grid=(n // tm,),
        in_specs=[row(), row(), wide("a_gate"), wide("h_gate"), wide("m_gla"), wide("m_hgrn"), row(),
                  full((GLA_V, D_MODEL)), full((HG_V, D_MODEL)), full((D_MODEL, D_MODEL)),
                  full((1, GLA_DV)), full((1, HGRN_DV)), full((1, D_MODEL)), full((1, D_MODEL))],
        out_specs=row(),
        out_shape=jax.ShapeDtypeStruct((n, D_MODEL), F32),
        compiler_params=pltpu.CompilerParams(
            dimension_semantics=("arbitrary",), vmem_limit_bytes=VMEM_LIMIT),
        name="merge",
    )(o_gla, o_hgrn, p2, p2, p2, p2, x2d, w_gla, w_hgrn, w_out, gain_a, gain_h, ln_g, ln_b)


def _gate_params(w, up_f, up_b, bias_f, bias_b):
    rank = up_f.shape[0]
    w_lr = jnp.zeros((w.shape[0], LANES), F32).at[:, :2 * rank].set(w[:, _LR_LO:_LR_HI])
    up = jnp.zeros((LANES, 2 * GLA_QK), F32)
    up = up.at[:rank, :GLA_QK].set(up_f).at[rank:2 * rank, GLA_QK:].set(up_b)
    bias = jnp.concatenate([bias_f, bias_b])[None, :]
    return w_lr.astype(BF16), up.astype(BF16), bias


def kernel(x, w_in, gla_gk_up_f, gla_gk_bias_f, gla_gk_up_b, gla_gk_bias_b, gla_norm_g,
           hgrn_lb_logits_f, hgrn_lb_logits_b, hgrn_norm_g, w_branch_gla, w_branch_hgrn,
           w_out, ln_g, ln_b):
    assert w_in.shape[0] == DEPTH == 1
    bsz, seq, d = x.shape
    assert d == D_MODEL and seq % SCAN_TILE == 0 and SCAN_TILE % CUMSUM_ROWS == 0
    x2d = x.astype(F32).reshape(bsz * seq, d)

    w = w_in.reshape(w_in.shape[1:]).astype(F32)
    w_lr, up, gbias = _gate_params(w, gla_gk_up_f[0].astype(F32), gla_gk_up_b[0].astype(F32),
                                   gla_gk_bias_f[0].astype(F32), gla_gk_bias_b[0].astype(F32))
    p2 = _project(x2d, w[:, :_LR_LO].astype(BF16), w[:, _LR_HI:].astype(BF16), w_lr, up, gbias,
                  hgrn_lb_logits_f.astype(F32), hgrn_lb_logits_b.astype(F32))
    p3 = p2.reshape(bsz, seq, PROJ_COLS)

    o_gla = _gla_scan(p3)
    o_hgrn = _hgrn_scan(p3)

    out = _merge(
        o_gla.reshape(bsz * seq, GLA_V), o_hgrn.reshape(bsz * seq, HG_V), p2, x2d,
        w_branch_gla[0].astype(BF16), w_branch_hgrn[0].astype(BF16), w_out[0].astype(BF16),
        gla_norm_g[0].astype(F32).reshape(1, GLA_DV), hgrn_norm_g[0].astype(F32).reshape(1, HGRN_DV),
        ln_g[0].astype(F32).reshape(1, D_MODEL), ln_b[0].astype(F32).reshape(1, D_MODEL))
    return out.reshape(bsz, seq, d).astype(x.dtype)
```

```python
import functools

import jax
import jax.numpy as jnp
import numpy as np
from jax import lax
from jax.experimental import pallas as pl
from jax.experimental.pallas import tpu as pltpu

F32 = jnp.float32
BF16 = jnp.bfloat16

D_MODEL = 1024
DEPTH = 1
GLA_HEADS = 4
GLA_DK = 128
GLA_DV = 256
GLA_GATE_RANK = 16
GLA_GATE_NORMALIZER = 16.0
HGRN_HEADS = 8
HGRN_DK = 128
HGRN_DV = 128
GLA_QK = GLA_HEADS * GLA_DK
GLA_V = GLA_HEADS * GLA_DV
HG_K = HGRN_HEADS * HGRN_DK
HG_V = HGRN_HEADS * HGRN_DV
RMS_EPS = 1e-6
LN_EPS = 1e-5
DEEPNORM_ALPHA = (2.0 * DEPTH) ** 0.25
LOG2E = 1.4426950408889634

LANES = 128
VREG_ELEMS = 8 * LANES
STATE_VREG_BUDGET = 32

_ORIG_SIZES = (GLA_QK, GLA_QK, GLA_V, GLA_V, GLA_GATE_RANK, GLA_GATE_RANK,
               HG_K, HG_K, HG_K, HG_V, HG_V, D_MODEL, D_MODEL)
_ORIG_NAMES = ("a_q", "a_k", "a_v", "a_gate", "lr_f", "lr_b",
               "h_q", "h_ff", "h_fb", "h_i", "h_gate", "m_gla", "m_hgrn")
_ORIG_START = dict(zip(_ORIG_NAMES, np.cumsum((0,) + _ORIG_SIZES[:-1]).tolist()))
_SIZE = dict(zip(_ORIG_NAMES, _ORIG_SIZES))
_LR_LO = _ORIG_START["lr_f"]
_LR_HI = _ORIG_START["lr_b"] + _SIZE["lr_b"]

_OUT_ORDER = ("a_q", "a_k", "a_v", "a_gate", "g_f", "g_b",
              "h_q", "h_ff", "h_fb", "h_i", "h_gate", "m_gla", "m_hgrn")
_OUT_SIZE = dict(_SIZE, g_f=GLA_QK, g_b=GLA_QK)
COL = dict(zip(_OUT_ORDER, np.cumsum([0] + [_OUT_SIZE[nm] for nm in _OUT_ORDER[:-1]]).tolist()))
PROJ_COLS = sum(_OUT_SIZE[nm] for nm in _OUT_ORDER)

CHUNK = 64
SCAN_TILE = 512
CUMSUM_ROWS = 256
PHASE2_UNROLL = 16
PROJ_TM = 512
MERGE_TM = 512
VMEM_LIMIT = 60 * 1024 * 1024


def _sigmoid(x):
    return 1.0 / (1.0 + jnp.exp2(x * (-LOG2E)))


def _lower_bound(logits, layer):
    m = jnp.max(logits, axis=0, keepdims=True)
    e = jnp.exp(logits - m)
    return jnp.sum(e[:layer + 1], axis=0, keepdims=True) / jnp.sum(e, axis=0, keepdims=True)


def _proj_kernel(x_ref, wh_ref, wt_ref, wlr_ref, up_ref, gbias_ref, lf_ref, lb_ref, o_ref):
    xb = x_ref[...].astype(BF16)
    mm = lambda w: jnp.dot(xb, w, preferred_element_type=F32)
    silu = lambda a: a * _sigmoid(a)

    def put(name, val):
        o_ref[:, COL[name]:COL[name] + val.shape[1]] = val.astype(BF16)

    def head(name):
        lo = _ORIG_START[name]
        return mm(wh_ref[:, lo:lo + _SIZE[name]])

    def tail(name):
        lo = _ORIG_START[name] - _LR_HI
        return mm(wt_ref[:, lo:lo + _SIZE[name]])

    def log2_forget(h, logits_ref):
        lb = _lower_bound(logits_ref[...], 0)
        return jnp.log2(lb + (1.0 - lb) * _sigmoid(h))

    put("a_q", head("a_q") * (GLA_DK ** -0.5))
    put("a_k", head("a_k"))
    put("a_v", head("a_v"))
    put("a_gate", silu(head("a_gate")))
    z = jnp.dot(mm(wlr_ref[...]).astype(BF16), up_ref[...], preferred_element_type=F32) + gbias_ref[...]
    log_sig = jnp.minimum(z, 0.0) - jnp.log2(1.0 + jnp.exp2(jnp.abs(z) * (-LOG2E))) * (1.0 / LOG2E)
    put("g_f", log_sig * (LOG2E / GLA_GATE_NORMALIZER))
    put("h_q", silu(tail("h_q")) * (HGRN_DK ** -0.5))
    put("h_ff", log2_forget(tail("h_ff"), lf_ref))
    put("h_fb", log2_forget(tail("h_fb"), lb_ref))
    put("h_i", tail("h_i"))
    put("h_gate", silu(tail("h_gate")))
    put("m_gla", _sigmoid(tail("m_gla")))
    put("m_hgrn", _sigmoid(tail("m_hgrn")))


def _project(x2d, w_head, w_tail, w_lr, up, gbias, logits_f, logits_b):
    n = x2d.shape[0]
    assert n % PROJ_TM == 0
    resident = lambda a: pl.BlockSpec(a.shape, lambda i: (0,) * a.ndim,
                                      pipeline_mode=pl.Buffered(1))
    consts = (w_head, w_tail, w_lr, up, gbias, logits_f, logits_b)
    return pl.pallas_call(
        _proj_kernel,
        grid=(n // PROJ_TM,),
        in_specs=[pl.BlockSpec((PROJ_TM, D_MODEL), lambda i: (i, 0))] + [resident(a) for a in consts],
        out_specs=pl.BlockSpec((PROJ_TM, PROJ_COLS), lambda i: (i, 0)),
        out_shape=jax.ShapeDtypeStruct((n, PROJ_COLS), BF16),
        compiler_params=pltpu.CompilerParams(
            dimension_semantics=("arbitrary",), vmem_limit_bytes=VMEM_LIMIT),
        name="proj",
    )(x2d, *consts)


def _dot_nt(a, b):
    return lax.dot_general(a, b, (((1,), (1,)), ((), ())), preferred_element_type=F32)


def _dot_tn(a, b):
    return lax.dot_general(a, b, (((0,), (0,)), ((), ())), preferred_element_type=F32)


def _block_diag_cumsum_mat(rows, c):
    row = lax.broadcasted_iota(jnp.int32, (rows, rows), 0)
    col = lax.broadcasted_iota(jnp.int32, (rows, rows), 1)
    return jnp.where(((row // c) == (col // c)) & (col <= row), 1.0, 0.0).astype(BF16)


def _scan_core(q_ref, k_of, g_refs, v_ref, o_ref, scratch, seq, dk, dv):
    qi_refs, upd_refs, dec_refs, oacc_ref, cum_ref = (scratch[0:2], scratch[2:4], scratch[4:6],
                                                       scratch[6], scratch[7])
    c = CHUNK
    tile = SCAN_TILE
    per_tile = tile // c
    n_tiles = seq // tile
    n = seq // c
    mid = c // 2
    row = lax.broadcasted_iota(jnp.int32, (c, c), 0)
    col = lax.broadcasted_iota(jnp.int32, (c, c), 1)
    masks = (col <= row, col >= row)
    cum_mat = _block_diag_cumsum_mat(CUMSUM_ROWS, c)

    def tile_rows(t):
        return pl.ds(pl.multiple_of(t * tile, tile), tile)

    def stage(t):
        g_pair = jnp.concatenate([g_refs[0][0, tile_rows(t), :], g_refs[1][0, tile_rows(t), :]], axis=1)
        for r in range(0, tile, CUMSUM_ROWS):
            cum_ref[r:r + CUMSUM_ROWS, :] = jnp.dot(cum_mat, g_pair[r:r + CUMSUM_ROWS],
                                                    preferred_element_type=F32)

    stage(0)

    def phase1(i, carry):
        rows = tile_rows(i)
        q = q_ref[0, rows, :].astype(F32)
        gs = (g_refs[0][0, rows, :].astype(F32), g_refs[1][0, rows, :].astype(F32))
        ks = (k_of(0, gs[0], rows), k_of(1, gs[1], rows))
        cum = cum_ref[...]
        v = v_ref[0, rows, :]
        slices = [slice(j * c, (j + 1) * c) for j in range(per_tile)]
        mixed, kps = [], []
        for j, sl in enumerate(slices):
            chunk_rows = pl.ds(pl.multiple_of(i * tile + j * c, c), c)
            acc, kp_pair = None, []
            for d in range(2):
                if d == 0:
                    b = cum[sl, :dk]
                    b_mid = b[mid:mid + 1, :]
                    total = b[c - 1:c, :]
                    rel, lead, trail = b - b_mid, b_mid, total - b_mid
                else:
                    incl = cum[sl, dk:]
                    e = incl - gs[1][sl]
                    e_mid = e[mid:mid + 1, :]
                    total = incl[c - 1:c, :]
                    rel, lead, trail = e_mid - e, total - e_mid, e_mid
                qm = q[sl] * jnp.exp2(rel)
                km = ks[d][sl] * jnp.exp2(-rel)
                scores = jnp.where(masks[d], _dot_nt(qm.astype(BF16), km.astype(BF16)), 0.0)
                acc = scores if acc is None else acc + scores
                qi_refs[d][chunk_rows, :] = (qm * jnp.exp2(lead)).astype(BF16)
                kp_pair.append((km * jnp.exp2(trail)).astype(BF16))
                dec_refs[d][i * per_tile + j] = jnp.broadcast_to(jnp.exp2(total), (LANES, dk)).T
            mixed.append(acc.astype(BF16))
            kps.append(jnp.concatenate(kp_pair, axis=1))
        for j, sl in enumerate(slices):
            upd = _dot_tn(kps[j], v[sl])
            upd_refs[0][i * per_tile + j] = upd[:dk]
            upd_refs[1][i * per_tile + j] = upd[dk:]
        stage(jnp.minimum(i + 1, n_tiles - 1))
        for j, sl in enumerate(slices):
            chunk_rows = pl.ds(pl.multiple_of(i * tile + j * c, c), c)
            oacc_ref[chunk_rows, :] = jnp.dot(mixed[j], v[sl], preferred_element_type=F32)
        return carry

    lax.fori_loop(0, n_tiles, phase1, 0)

    def advance(d, ci, st, final):
        rows = pl.ds(pl.multiple_of(ci * c, c), c)
        o = oacc_ref[rows, :] + jnp.dot(qi_refs[d][rows, :], st.astype(BF16),
                                        preferred_element_type=F32)
        if final:
            o_ref[0, rows, :] = o.astype(o_ref.dtype)
        else:
            oacc_ref[rows, :] = o
        decay = jnp.concatenate([dec_refs[d][ci]] * (dv // LANES), axis=1)
        return st * decay + upd_refs[d][ci]

    zero = jnp.zeros((dk, dv), F32)
    if 2 * dk * dv <= STATE_VREG_BUDGET * VREG_ELEMS:
        both = lambda final: (lambda i, s: (advance(0, i, s[0], final),
                                            advance(1, n - 1 - i, s[1], final)))
        states = lax.fori_loop(0, n // 2, both(False), (zero, zero), unroll=PHASE2_UNROLL)
        lax.fori_loop(n // 2, n, both(True), states, unroll=PHASE2_UNROLL)
    else:
        lax.fori_loop(0, n, lambda i, s: advance(0, i, s, False), zero, unroll=PHASE2_UNROLL)
        lax.fori_loop(0, n, lambda i, s: advance(1, n - 1 - i, s, True), zero, unroll=PHASE2_UNROLL)


def _scan_scratch(seq, dk, dv):
    per_dir = lambda shape, dt: [pltpu.VMEM(shape, dt), pltpu.VMEM(shape, dt)]
    n = seq // CHUNK
    return (per_dir((seq, dk), BF16)
            + per_dir((n, dk, dv), F32)
            + per_dir((n, dk, LANES), F32)
            + [pltpu.VMEM((seq, dv), F32)]
            + [pltpu.VMEM((SCAN_TILE, 2 * dk), F32)])


def _scan_call(kernel_fn, name, p3, specs, heads, dk, dv):
    bsz, seq, _ = p3.shape
    return pl.pallas_call(
        functools.partial(kernel_fn, seq=seq),
        grid=(bsz, heads),
        in_specs=specs,
        out_specs=pl.BlockSpec((1, seq, dv), lambda b, h: (b, 0, h)),
        out_shape=jax.ShapeDtypeStruct((bsz, seq, heads * dv), BF16),
        scratch_shapes=_scan_scratch(seq, dk, dv),
        compiler_params=pltpu.CompilerParams(
            dimension_semantics=("arbitrary", "arbitrary"), vmem_limit_bytes=VMEM_LIMIT),
        name=name,
    )(*([p3] * len(specs)))


def _head_block(seq, name, width):
    first = COL[name] // width
    return pl.BlockSpec((1, seq, width), lambda b, h: (b, 0, first + h))


def _gla_kernel(q_ref, k_ref, gf_ref, gb_ref, v_ref, o_ref, *scratch, seq):
    k_of = lambda d, g, rows: k_ref[0, rows, :].astype(F32)
    _scan_core(q_ref, k_of, (gf_ref, gb_ref), v_ref, o_ref, scratch, seq, GLA_DK, GLA_DV)


def _gla_scan(p3):
    seq = p3.shape[1]
    specs = [_head_block(seq, "a_q", GLA_DK), _head_block(seq, "a_k", GLA_DK),
             _head_block(seq, "g_f", GLA_DK), _head_block(seq, "g_b", GLA_DK),
             _head_block(seq, "a_v", GLA_DV)]
    return _scan_call(_gla_kernel, "gla_scan", p3, specs, GLA_HEADS, GLA_DK, GLA_DV)


def _hgrn_kernel(q_ref, gf_ref, gb_ref, v_ref, o_ref, *scratch, seq):
    k_of = lambda d, g, rows: 1.0 - jnp.exp2(g)
    _scan_core(q_ref, k_of, (gf_ref, gb_ref), v_ref, o_ref, scratch, seq, HGRN_DK, HGRN_DV)


def _hgrn_scan(p3):
    seq = p3.shape[1]
    specs = [_head_block(seq, "h_q", HGRN_DK), _head_block(seq, "h_ff", HGRN_DK),
             _head_block(seq, "h_fb", HGRN_DK), _head_block(seq, "h_i", HGRN_DV)]
    return _scan_call(_hgrn_kernel, "hgrn_scan", p3, specs, HGRN_HEADS, HGRN_DK, HGRN_DV)


def _head_rmsnorm_gate(o, gain, gate, heads, dv):
    parts = []
    for h in range(heads):
        seg = o[:, h * dv:(h + 1) * dv]
        ms = jnp.mean(seg * seg, axis=-1, keepdims=True)
        parts.append(seg * lax.rsqrt(ms + RMS_EPS) * gain)
    return (jnp.concatenate(parts, axis=1) * gate).astype(BF16)


def _merge_kernel(oa_ref, oh_ref, ag_ref, hg_ref, mg_ref, mh_ref, x_ref, wa_ref, wh_ref, wo_ref,
                  ga_ref, gh_ref, lng_ref, lnb_ref, out_ref):
    na = _head_rmsnorm_gate(oa_ref[...].astype(F32), ga_ref[...], ag_ref[...].astype(F32),
                            GLA_HEADS, GLA_DV)
    nh = _head_rmsnorm_gate(oh_ref[...].astype(F32), gh_ref[...], hg_ref[...].astype(F32),
                            HGRN_HEADS, HGRN_DV)
    y_gla = jnp.dot(na, wa_ref[...], preferred_element_type=F32)
    y_hgrn = jnp.dot(nh, wh_ref[...], preferred_element_type=F32)
    y = mg_ref[...].astype(F32) * y_gla + mh_ref[...].astype(F32) * y_hgrn
    y = jnp.dot(y.astype(BF16), wo_ref[...], preferred_element_type=F32)
    r = DEEPNORM_ALPHA * x_ref[...] + y
    mu = jnp.mean(r, axis=-1, keepdims=True)
    rc = r - mu
    var = jnp.mean(rc * rc, axis=-1, keepdims=True)
    out_ref[...] = rc * lax.rsqrt(var + LN_EPS) * lng_ref[...] + lnb_ref[...]


def _merge(o_gla, o_hgrn, p2, x2d, w_gla, w_hgrn, w_out, gain_a, gain_h, ln_g, ln_b):
    n = x2d.shape[0]
    tm = MERGE_TM
    wide = lambda name: pl.BlockSpec((tm, D_MODEL), lambda i, name=name: (i, COL[name] // D_MODEL))
    row = lambda: pl.BlockSpec((tm, D_MODEL), lambda i: (i, 0))
    full = lambda shape: pl.BlockSpec(shape, lambda i: (0,) * len(shape))
    return pl.pallas_call(
        _merge_kernel,
        grid=(n // tm,),
        in_specs=[row(), row(), wide("a_gate"), wide("h_gate"), wide("m_gla"), wide("m_hgrn"), row(),
                  full((GLA_V, D_MODEL)), full((HG_V, D_MODEL)), full((D_MODEL, D_MODEL)),
                  full((1, GLA_DV)), full((1, HGRN_DV)), full((1, D_MODEL)), full((1, D_MODEL))],
        out_specs=row(),
        out_shape=jax.ShapeDtypeStruct((n, D_MODEL), F32),
        compiler_params=pltpu.CompilerParams(
            dimension_semantics=("arbitrary",), vmem_limit_bytes=VMEM_LIMIT),
        name="merge",
    )(o_gla, o_hgrn, p2, p2, p2, p2, x2d, w_gla, w_hgrn, w_out, gain_a, gain_h, ln_g, ln_b)


def _gate_params(w, up_f, up_b, bias_f, bias_b):
    rank = up_f.shape[0]
    w_lr = jnp.zeros((w.shape[0], LANES), F32).at[:, :2 * rank].set(w[:, _LR_LO:_LR_HI])
    up = jnp.zeros((LANES, 2 * GLA_QK), F32)
    up = up.at[:rank, :GLA_QK].set(up_f).at[rank:2 * rank, GLA_QK:].set(up_b)
    bias = jnp.concatenate([bias_f, bias_b])[None, :]
    return w_lr.astype(BF16), up.astype(BF16), bias


def kernel(x, w_in, gla_gk_up_f, gla_gk_bias_f, gla_gk_up_b, gla_gk_bias_b, gla_norm_g,
           hgrn_lb_logits_f, hgrn_lb_logits_b, hgrn_norm_g, w_branch_gla, w_branch_hgrn,
           w_out, ln_g, ln_b):
    assert w_in.shape[0] == DEPTH == 1
    bsz, seq, d = x.shape
    assert d == D_MODEL and seq % SCAN_TILE == 0 and SCAN_TILE % CUMSUM_ROWS == 0
    x2d = x.astype(F32).reshape(bsz * seq, d)

    w = w_in.reshape(w_in.shape[1:]).astype(F32)
    w_lr, up, gbias = _gate_params(w, gla_gk_up_f[0].astype(F32), gla_gk_up_b[0].astype(F32),
                                   gla_gk_bias_f[0].astype(F32), gla_gk_bias_b[0].astype(F32))
    p2 = _project(x2d, w[:, :_LR_LO].astype(BF16), w[:, _LR_HI:].astype(BF16), w_lr, up, gbias,
                  hgrn_lb_logits_f.astype(F32), hgrn_lb_logits_b.astype(F32))
    p3 = p2.reshape(bsz, seq, PROJ_COLS)

    o_gla = _gla_scan(p3)
    o_hgrn = _hgrn_scan(p3)

    out = _merge(
        o_gla.reshape(bsz * seq, GLA_V), o_hgrn.reshape(bsz * seq, HG_V), p2, x2d,
        w_branch_gla[0].astype(BF16), w_branch_hgrn[0].astype(BF16), w_out[0].astype(BF16),
        gla_norm_g[0].astype(F32).reshape(1, GLA_DV), hgrn_norm_g[0].astype(F32).reshape(1, HGRN_DV),
        ln_g[0].astype(F32).reshape(1, D_MODEL), ln_b[0].astype(F32).reshape(1, D_MODEL))
    return out.reshape(bsz, seq, d).astype(x.dtype)
```

```python
import functools

import jax
import jax.numpy as jnp
import numpy as np
from jax import lax
from jax.experimental import pallas as pl
from jax.experimental.pallas import tpu as pltpu

F32 = jnp.float32
BF16 = jnp.bfloat16

D_MODEL = 1024
DEPTH = 1
GLA_HEADS = 4
GLA_DK = 128
GLA_DV = 256
GLA_GATE_RANK = 16
GLA_GATE_NORMALIZER = 16.0
HGRN_HEADS = 8
HGRN_DK = 128
HGRN_DV = 128
GLA_QK = GLA_HEADS * GLA_DK
GLA_V = GLA_HEADS * GLA_DV
HG_K = HGRN_HEADS * HGRN_DK
HG_V = HGRN_HEADS * HGRN_DV
RMS_EPS = 1e-6
LN_EPS = 1e-5
DEEPNORM_ALPHA = (2.0 * DEPTH) ** 0.25
LOG2E = 1.4426950408889634

LANES = 128
VREG_ELEMS = 8 * LANES
STATE_VREG_BUDGET = 32

_ORIG_SIZES = (GLA_QK, GLA_QK, GLA_V, GLA_V, GLA_GATE_RANK, GLA_GATE_RANK,
               HG_K, HG_K, HG_K, HG_V, HG_V, D_MODEL, D_MODEL)
_ORIG_NAMES = ("a_q", "a_k", "a_v", "a_gate", "lr_f", "lr_b",
               "h_q", "h_ff", "h_fb", "h_i", "h_gate", "m_gla", "m_hgrn")
_ORIG_START = dict(zip(_ORIG_NAMES, np.cumsum((0,) + _ORIG_SIZES[:-1]).tolist()))
_SIZE = dict(zip(_ORIG_NAMES, _ORIG_SIZES))
_LR_LO = _ORIG_START["lr_f"]
_LR_HI = _ORIG_START["lr_b"] + _SIZE["lr_b"]

_OUT_ORDER = ("a_q", "a_k", "a_v", "a_gate", "g_f", "g_b",
              "h_q", "h_ff", "h_fb", "h_i", "h_gate", "m_gla", "m_hgrn")
_OUT_SIZE = dict(_SIZE, g_f=GLA_QK, g_b=GLA_QK)
COL = dict(zip(_OUT_ORDER, np.cumsum([0] + [_OUT_SIZE[nm] for nm in _OUT_ORDER[:-1]]).tolist()))
PROJ_COLS = sum(_OUT_SIZE[nm] for nm in _OUT_ORDER)

CHUNK = 64
SCAN_TILE = 512
CUMSUM_ROWS = 256
PHASE2_UNROLL = 16
PROJ_TM = 512
WPACK_TN = 1024
MERGE_TM = 512
VMEM_LIMIT = 60 * 1024 * 1024


def _sigmoid(x):
    return 1.0 / (1.0 + jnp.exp2(x * (-LOG2E)))


def _lower_bound(logits, layer):
    m = jnp.max(logits, axis=0, keepdims=True)
    e = jnp.exp(logits - m)
    return jnp.sum(e[:layer + 1], axis=0, keepdims=True) / jnp.sum(e, axis=0, keepdims=True)


def _pack_w_kernel(a_ref, b_ref, o_ref):
    j = pl.program_id(0)

    @pl.when(j < _LR_LO // WPACK_TN)
    def _():
        o_ref[...] = a_ref[...].astype(BF16)

    @pl.when(j >= _LR_LO // WPACK_TN)
    def _():
        window = jnp.concatenate([a_ref[...], b_ref[...]], axis=1)
        gap = _LR_HI - _LR_LO
        o_ref[...] = window[:, gap:gap + WPACK_TN].astype(BF16)


def _pack_w(w):
    d, width = w.shape[0], w.shape[1] - (_LR_HI - _LR_LO)
    assert _LR_LO % WPACK_TN == 0 and width % WPACK_TN == 0 and _LR_HI - _LR_LO <= LANES
    per = WPACK_TN // LANES
    return pl.pallas_call(
        _pack_w_kernel,
        grid=(width // WPACK_TN,),
        in_specs=[pl.BlockSpec((d, WPACK_TN), lambda j: (0, j)),
                  pl.BlockSpec((d, LANES), lambda j: (0, (j + 1) * per))],
        out_specs=pl.BlockSpec((d, WPACK_TN), lambda j: (0, j)),
        out_shape=jax.ShapeDtypeStruct((d, width), BF16),
        compiler_params=pltpu.CompilerParams(
            dimension_semantics=("arbitrary",), vmem_limit_bytes=VMEM_LIMIT),
        name="pack_w",
    )(w, w)


def _proj_kernel(x_ref, w_ref, wlr_ref, up_ref, gbias_ref, lf_ref, lb_ref, o_ref):
    xb = x_ref[...].astype(BF16)
    mm = lambda w: jnp.dot(xb, w, preferred_element_type=F32)
    silu = lambda a: a * _sigmoid(a)

    def put(name, val):
        o_ref[:, COL[name]:COL[name] + val.shape[1]] = val.astype(BF16)

    def head(name):
        lo = _ORIG_START[name]
        return mm(w_ref[:, lo:lo + _SIZE[name]])

    def tail(name):
        lo = _ORIG_START[name] - (_LR_HI - _LR_LO)
        return mm(w_ref[:, lo:lo + _SIZE[name]])

    def log2_forget(h, logits_ref):
        lb = _lower_bound(logits_ref[...], 0)
        return jnp.log2(lb + (1.0 - lb) * _sigmoid(h))

    put("a_q", head("a_q") * (GLA_DK ** -0.5))
    put("a_k", head("a_k"))
    put("a_v", head("a_v"))
    put("a_gate", silu(head("a_gate")))
    z = jnp.dot(mm(wlr_ref[...]).astype(BF16), up_ref[...], preferred_element_type=F32) + gbias_ref[...]
    log_sig = jnp.minimum(z, 0.0) - jnp.log2(1.0 + jnp.exp2(jnp.abs(z) * (-LOG2E))) * (1.0 / LOG2E)
    put("g_f", log_sig * (LOG2E / GLA_GATE_NORMALIZER))
    put("h_q", silu(tail("h_q")) * (HGRN_DK ** -0.5))
    put("h_ff", log2_forget(tail("h_ff"), lf_ref))
    put("h_fb", log2_forget(tail("h_fb"), lb_ref))
    put("h_i", tail("h_i"))
    put("h_gate", silu(tail("h_gate")))
    put("m_gla", _sigmoid(tail("m_gla")))
    put("m_hgrn", _sigmoid(tail("m_hgrn")))


def _project(x2d, w_packed, w_lr, up, gbias, logits_f, logits_b):
    n = x2d.shape[0]
    assert n % PROJ_TM == 0
    resident = lambda a: pl.BlockSpec(a.shape, lambda i: (0,) * a.ndim,
                                      pipeline_mode=pl.Buffered(1))
    consts = (w_packed, w_lr, up, gbias, logits_f, logits_b)
    return pl.pallas_call(
        _proj_kernel,
        grid=(n // PROJ_TM,),
        in_specs=[pl.BlockSpec((PROJ_TM, D_MODEL), lambda i: (i, 0))] + [resident(a) for a in consts],
        out_specs=pl.BlockSpec((PROJ_TM, PROJ_COLS), lambda i: (i, 0)),
        out_shape=jax.ShapeDtypeStruct((n, PROJ_COLS), BF16),
        compiler_params=pltpu.CompilerParams(
            dimension_semantics=("arbitrary",), vmem_limit_bytes=VMEM_LIMIT),
        name="proj",
    )(x2d, *consts)


def _dot_nt(a, b):
    return lax.dot_general(a, b, (((1,), (1,)), ((), ())), preferred_element_type=F32)


def _dot_tn(a, b):
    return lax.dot_general(a, b, (((0,), (0,)), ((), ())), preferred_element_type=F32)


def _block_diag_cumsum_mat(rows, c):
    row = lax.broadcasted_iota(jnp.int32, (rows, rows), 0)
    col = lax.broadcasted_iota(jnp.int32, (rows, rows), 1)
    return jnp.where(((row // c) == (col // c)) & (col <= row), 1.0, 0.0).astype(BF16)


def _scan_core(q_ref, k_of, g_refs, v_ref, o_ref, scratch, seq, dk, dv):
    qi_refs, upd_refs, dec_refs, oacc_ref, cum_ref = (scratch[0:2], scratch[2:4], scratch[4:6],
                                                       scratch[6], scratch[7])
    c = CHUNK
    tile = SCAN_TILE
    per_tile = tile // c
    n_tiles = seq // tile
    n = seq // c
    mid = c // 2
    row = lax.broadcasted_iota(jnp.int32, (c, c), 0)
    col = lax.broadcasted_iota(jnp.int32, (c, c), 1)
    masks = (col <= row, col >= row)
    cum_mat = _block_diag_cumsum_mat(CUMSUM_ROWS, c)

    def tile_rows(t):
        return pl.ds(pl.multiple_of(t * tile, tile), tile)

    def stage(t):
        g_pair = jnp.concatenate([g_refs[0][0, tile_rows(t), :], g_refs[1][0, tile_rows(t), :]], axis=1)
        for r in range(0, tile, CUMSUM_ROWS):
            cum_ref[r:r + CUMSUM_ROWS, :] = jnp.dot(cum_mat, g_pair[r:r + CUMSUM_ROWS],
                                                    preferred_element_type=F32)

    stage(0)

    def phase1(i, carry):
        rows = tile_rows(i)
        q = q_ref[0, rows, :].astype(F32)
        gs = (g_refs[0][0, rows, :].astype(F32), g_refs[1][0, rows, :].astype(F32))
        ks = (k_of(0, gs[0], rows), k_of(1, gs[1], rows))
        cum = cum_ref[...]
        v = v_ref[0, rows, :]
        slices = [slice(j * c, (j + 1) * c) for j in range(per_tile)]
        mixed, kps = [], []
        for j, sl in enumerate(slices):
            chunk_rows = pl.ds(pl.multiple_of(i * tile + j * c, c), c)
            acc, kp_pair = None, []
            for d in range(2):
                if d == 0:
                    b = cum[sl, :dk]
                    b_mid = b[mid:mid + 1, :]
                    total = b[c - 1:c, :]
                    rel, lead, trail = b - b_mid, b_mid, total - b_mid
                else:
                    incl = cum[sl, dk:]
                    e = incl - gs[1][sl]
                    e_mid = e[mid:mid + 1, :]
                    total = incl[c - 1:c, :]
                    rel, lead, trail = e_mid - e, total - e_mid, e_mid
                qm = q[sl] * jnp.exp2(rel)
                km = ks[d][sl] * jnp.exp2(-rel)
                scores = jnp.where(masks[d], _dot_nt(qm.astype(BF16), km.astype(BF16)), 0.0)
                acc = scores if acc is None else acc + scores
                qi_refs[d][chunk_rows, :] = (qm * jnp.exp2(lead)).astype(BF16)
                kp_pair.append((km * jnp.exp2(trail)).astype(BF16))
                dec_refs[d][i * per_tile + j] = jnp.broadcast_to(jnp.exp2(total), (LANES, dk)).T
            mixed.append(acc.astype(BF16))
            kps.append(jnp.concatenate(kp_pair, axis=1))
        for j, sl in enumerate(slices):
            upd = _dot_tn(kps[j], v[sl])
            upd_refs[0][i * per_tile + j] = upd[:dk]
            upd_refs[1][i * per_tile + j] = upd[dk:]
        stage(jnp.minimum(i + 1, n_tiles - 1))
        for j, sl in enumerate(slices):
            chunk_rows = pl.ds(pl.multiple_of(i * tile + j * c, c), c)
            oacc_ref[chunk_rows, :] = jnp.dot(mixed[j], v[sl], preferred_element_type=F32)
        return carry

    lax.fori_loop(0, n_tiles, phase1, 0)

    def advance(d, ci, st, final):
        rows = pl.ds(pl.multiple_of(ci * c, c), c)
        o = oacc_ref[rows, :] + jnp.dot(qi_refs[d][rows, :], st.astype(BF16),
                                        preferred_element_type=F32)
        if final:
            o_ref[0, rows, :] = o.astype(o_ref.dtype)
        else:
            oacc_ref[rows, :] = o
        decay = jnp.concatenate([dec_refs[d][ci]] * (dv // LANES), axis=1)
        return st * decay + upd_refs[d][ci]

    zero = jnp.zeros((dk, dv), F32)
    if 2 * dk * dv <= STATE_VREG_BUDGET * VREG_ELEMS:
        both = lambda final: (lambda i, s: (advance(0, i, s[0], final),
                                            advance(1, n - 1 - i, s[1], final)))
        states = lax.fori_loop(0, n // 2, both(False), (zero, zero), unroll=PHASE2_UNROLL)
        lax.fori_loop(n // 2, n, both(True), states, unroll=PHASE2_UNROLL)
    else:
        lax.fori_loop(0, n, lambda i, s: advance(0, i, s, False), zero, unroll=PHASE2_UNROLL)
        lax.fori_loop(0, n, lambda i, s: advance(1, n - 1 - i, s, True), zero, unroll=PHASE2_UNROLL)


def _scan_scratch(seq, dk, dv):
    per_dir = lambda shape, dt: [pltpu.VMEM(shape, dt), pltpu.VMEM(shape, dt)]
    n = seq // CHUNK
    return (per_dir((seq, dk), BF16)
            + per_dir((n, dk, dv), F32)
            + per_dir((n, dk, LANES), F32)
            + [pltpu.VMEM((seq, dv), F32)]
            + [pltpu.VMEM((SCAN_TILE, 2 * dk), F32)])


def _scan_call(kernel_fn, name, p3, specs, heads, dk, dv):
    bsz, seq, _ = p3.shape
    return pl.pallas_call(
        functools.partial(kernel_fn, seq=seq),
        grid=(bsz, heads),
        in_specs=specs,
        out_specs=pl.BlockSpec((1, seq, dv), lambda b, h: (b, 0, h)),
        out_shape=jax.ShapeDtypeStruct((bsz, seq, heads * dv), BF16),
        scratch_shapes=_scan_scratch(seq, dk, dv),
        compiler_params=pltpu.CompilerParams(
            dimension_semantics=("arbitrary", "arbitrary"), vmem_limit_bytes=VMEM_LIMIT),
        name=name,
    )(*([p3] * len(specs)))


def _head_block(seq, name, width):
    first = COL[name] // width
    return pl.BlockSpec((1, seq, width), lambda b, h: (b, 0, first + h))


def _gla_kernel(q_ref, k_ref, gf_ref, gb_ref, v_ref, o_ref, *scratch, seq):
    k_of = lambda d, g, rows: k_ref[0, rows, :].astype(F32)
    _scan_core(q_ref, k_of, (gf_ref, gb_ref), v_ref, o_ref, scratch, seq, GLA_DK, GLA_DV)


def _gla_scan(p3):
    seq = p3.shape[1]
    specs = [_head_block(seq, "a_q", GLA_DK), _head_block(seq, "a_k", GLA_DK),
             _head_block(seq, "g_f", GLA_DK), _head_block(seq, "g_b", GLA_DK),
             _head_block(seq, "a_v", GLA_DV)]
    return _scan_call(_gla_kernel, "gla_scan", p3, specs, GLA_HEADS, GLA_DK, GLA_DV)


def _hgrn_kernel(q_ref, gf_ref, gb_ref, v_ref, o_ref, *scratch, seq):
    k_of = lambda d, g, rows: 1.0 - jnp.exp2(g)
    _scan_core(q_ref, k_of, (gf_ref, gb_ref), v_ref, o_ref, scratch, seq, HGRN_DK, HGRN_DV)


def _hgrn_scan(p3):
    seq = p3.shape[1]
    specs = [_head_block(seq, "h_q", HGRN_DK), _head_block(seq, "h_ff", HGRN_DK),
             _head_block(seq, "h_fb", HGRN_DK), _head_block(seq, "h_i", HGRN_DV)]
    return _scan_call(_hgrn_kernel, "hgrn_scan", p3, specs, HGRN_HEADS, HGRN_DK, HGRN_DV)


def _head_rmsnorm_gate(o, gain, gate, heads, dv):
    parts = []
    for h in range(heads):
        seg = o[:, h * dv:(h + 1) * dv]
        ms = jnp.mean(seg * seg, axis=-1, keepdims=True)
        parts.append(seg * lax.rsqrt(ms + RMS_EPS) * gain)
    return (jnp.concatenate(parts, axis=1) * gate).astype(BF16)


def _merge_kernel(oa_ref, oh_ref, ag_ref, hg_ref, mg_ref, mh_ref, x_ref, wa_ref, wh_ref, wo_ref,
                  ga_ref, gh_ref, lng_ref, lnb_ref, out_ref):
    na = _head_rmsnorm_gate(oa_ref[...].astype(F32), ga_ref[...], ag_ref[...].astype(F32),
                            GLA_HEADS, GLA_DV)
    nh = _head_rmsnorm_gate(oh_ref[...].astype(F32), gh_ref[...], hg_ref[...].astype(F32),
                            HGRN_HEADS, HGRN_DV)
    y_gla = jnp.dot(na, wa_ref[...], preferred_element_type=F32)
    y_hgrn = jnp.dot(nh, wh_ref[...], preferred_element_type=F32)
    y = mg_ref[...].astype(F32) * y_gla + mh_ref[...].astype(F32) * y_hgrn
    y = jnp.dot(y.astype(BF16), wo_ref[...], preferred_element_type=F32)
    r = DEEPNORM_ALPHA * x_ref[...] + y
    mu = jnp.mean(r, axis=-1, keepdims=True)
    rc = r - mu
    var = jnp.mean(rc * rc, axis=-1, keepdims=True)
    out_ref[...] = rc * lax.rsqrt(var + LN_EPS) * lng_ref[...] + lnb_ref[...]


def _merge(o_gla, o_hgrn, p2, x2d, w_gla, w_hgrn, w_out, gain_a, gain_h, ln_g, ln_b):
    n = x2d.shape[0]
    tm = MERGE_TM
    wide = lambda name: pl.BlockSpec((tm, D_MODEL), lambda i, name=name: (i, COL[name] // D_MODEL))
    row = lambda: pl.BlockSpec((tm, D_MODEL), lambda i: (i, 0))
    full = lambda shape: pl.BlockSpec(shape, lambda i: (0,) * len(shape))
    return pl.pallas_call(
        _merge_kernel,
        grid=(n // tm,),
        in_specs=[row(), row(), wide("a_gate"), wide("h_gate"), wide("m_gla"), wide("m_hgrn"), row(),
                  full((GLA_V, D_MODEL)), full((HG_V, D_MODEL)), full((D_MODEL, D_MODEL)),
                  full((1, GLA_DV)), full((1, HGRN_DV)), full((1, D_MODEL)), full((1, D_MODEL))],
        out_specs=row(),
        out_shape=jax.ShapeDtypeStruct((n, D_MODEL), F32),
        compiler_params=pltpu.CompilerParams(
            dimension_semantics=("arbitrary",), vmem_limit_bytes=VMEM_LIMIT),
        name="merge",
    )(o_gla, o_hgrn, p2, p2, p2, p2, x2d, w_gla, w_hgrn, w_out, gain_a, gain_h, ln_g, ln_b)


def _gate_params(w_rank, up_f, up_b, bias_f, bias_b):
    rank = up_f.shape[0]
    w_lr = jnp.pad(w_rank, ((0, 0), (0, LANES - 2 * rank)))
    up = jnp.zeros((LANES, 2 * GLA_QK), F32)
    up = up.at[:rank, :GLA_QK].set(up_f).at[rank:2 * rank, GLA_QK:].set(up_b)
    bias = jnp.concatenate([bias_f, bias_b])[None, :]
    return w_lr.astype(BF16), up.astype(BF16), bias


def kernel(x, w_in, gla_gk_up_f, gla_gk_bias_f, gla_gk_up_b, gla_gk_bias_b, gla_norm_g,
           hgrn_lb_logits_f, hgrn_lb_logits_b, hgrn_norm_g, w_branch_gla, w_branch_hgrn,
           w_out, ln_g, ln_b):
    assert w_in.shape[0] == DEPTH == 1
    bsz, seq, d = x.shape
    assert d == D_MODEL and seq % SCAN_TILE == 0 and SCAN_TILE % CUMSUM_ROWS == 0
    x2d = x.astype(F32).reshape(bsz * seq, d)

    w = w_in.reshape(w_in.shape[1:])
    w_lr, up, gbias = _gate_params(w[:, _LR_LO:_LR_HI].astype(F32),
                                   gla_gk_up_f[0].astype(F32), gla_gk_up_b[0].astype(F32),
                                   gla_gk_bias_f[0].astype(F32), gla_gk_bias_b[0].astype(F32))
    p2 = _project(x2d, _pack_w(w.astype(F32)), w_lr, up, gbias,
                  hgrn_lb_logits_f.astype(F32), hgrn_lb_logits_b.astype(F32))
    p3 = p2.reshape(bsz, seq, PROJ_COLS)

    o_gla = _gla_scan(p3)
    o_hgrn = _hgrn_scan(p3)

    out = _merge(
        o_gla.reshape(bsz * seq, GLA_V), o_hgrn.reshape(bsz * seq, HG_V), p2, x2d,
        w_branch_gla[0].astype(BF16), w_branch_hgrn[0].astype(BF16), w_out[0].astype(BF16),
        gla_norm_g[0].astype(F32).reshape(1, GLA_DV), hgrn_norm_g[0].astype(F32).reshape(1, HGRN_DV),
        ln_g[0].astype(F32).reshape(1, D_MODEL), ln_b[0].astype(F32).reshape(1, D_MODEL))
    return out.reshape(bsz, seq, d).astype(x.dtype)
```

```python
import functools

import jax
import jax.numpy as jnp
import numpy as np
from jax import lax
from jax.experimental import pallas as pl
from jax.experimental.pallas import tpu as pltpu

F32 = jnp.float32
BF16 = jnp.bfloat16

D_MODEL = 1024
DEPTH = 1
GLA_HEADS = 4
GLA_DK = 128
GLA_DV = 256
GLA_GATE_RANK = 16
GLA_GATE_NORMALIZER = 16.0
HGRN_HEADS = 8
HGRN_DK = 128
HGRN_DV = 128
GLA_QK = GLA_HEADS * GLA_DK
GLA_V = GLA_HEADS * GLA_DV
HG_K = HGRN_HEADS * HGRN_DK
HG_V = HGRN_HEADS * HGRN_DV
RMS_EPS = 1e-6
LN_EPS = 1e-5
DEEPNORM_ALPHA = (2.0 * DEPTH) ** 0.25
LOG2E = 1.4426950408889634

LANES = 128
VREG_ELEMS = 8 * LANES
STATE_VREG_BUDGET = 32

_ORIG_SIZES = (GLA_QK, GLA_QK, GLA_V, GLA_V, GLA_GATE_RANK, GLA_GATE_RANK,
               HG_K, HG_K, HG_K, HG_V, HG_V, D_MODEL, D_MODEL)
_ORIG_NAMES = ("a_q", "a_k", "a_v", "a_gate", "lr_f", "lr_b",
               "h_q", "h_ff", "h_fb", "h_i", "h_gate", "m_gla", "m_hgrn")
_ORIG_START = dict(zip(_ORIG_NAMES, np.cumsum((0,) + _ORIG_SIZES[:-1]).tolist()))
_SIZE = dict(zip(_ORIG_NAMES, _ORIG_SIZES))
_LR_LO = _ORIG_START["lr_f"]
_LR_HI = _ORIG_START["lr_b"] + _SIZE["lr_b"]

_OUT_ORDER = ("a_q", "a_k", "a_v", "a_gate", "g_f", "g_b",
              "h_q", "h_ff", "h_fb", "h_i", "h_gate", "m_gla", "m_hgrn")
_OUT_SIZE = dict(_SIZE, g_f=GLA_QK, g_b=GLA_QK)
COL = dict(zip(_OUT_ORDER, np.cumsum([0] + [_OUT_SIZE[nm] for nm in _OUT_ORDER[:-1]]).tolist()))
PROJ_COLS = sum(_OUT_SIZE[nm] for nm in _OUT_ORDER)

CHUNK = 64
SCAN_TILE = 512
CUMSUM_ROWS = 256
PHASE2_UNROLL = 16
PROJ_TM = 512
WPACK_TN = 1024
MERGE_TM = 512
VMEM_LIMIT = 60 * 1024 * 1024


def _sigmoid(x):
    return 1.0 / (1.0 + jnp.exp2(x * (-LOG2E)))


def _lower_bound(logits, layer):
    m = jnp.max(logits, axis=0, keepdims=True)
    e = jnp.exp(logits - m)
    return jnp.sum(e[:layer + 1], axis=0, keepdims=True) / jnp.sum(e, axis=0, keepdims=True)


def _pack_w_kernel(a_ref, b_ref, o_ref, lr_ref):
    j = pl.program_id(0)
    gap = _LR_HI - _LR_LO
    first_shifted = _LR_LO // WPACK_TN

    @pl.when(j < first_shifted)
    def _():
        o_ref[...] = a_ref[...].T.astype(BF16)

    @pl.when(j == first_shifted - 1)
    def _():
        padded = jnp.concatenate([b_ref[...], jnp.zeros((LANES - gap, b_ref.shape[1]), F32)], axis=0)
        lr_ref[...] = padded.T.astype(BF16)

    @pl.when(j >= first_shifted)
    def _():
        rows = jnp.concatenate([a_ref[gap:, :], b_ref[...]], axis=0)
        o_ref[...] = rows.T.astype(BF16)


def _pack_w(w_t):
    d, gap = w_t.shape[1], _LR_HI - _LR_LO
    width = w_t.shape[0] - gap
    assert _LR_LO % WPACK_TN == 0 and _LR_LO > 0 and width % WPACK_TN == 0 and WPACK_TN % gap == 0
    return pl.pallas_call(
        _pack_w_kernel,
        grid=(width // WPACK_TN,),
        in_specs=[pl.BlockSpec((WPACK_TN, d), lambda j: (j, 0)),
                  pl.BlockSpec((gap, d), lambda j: ((j + 1) * (WPACK_TN // gap), 0))],
        out_specs=[pl.BlockSpec((d, WPACK_TN), lambda j: (0, j)),
                   pl.BlockSpec((d, LANES), lambda j: (0, 0))],
        out_shape=[jax.ShapeDtypeStruct((d, width), BF16), jax.ShapeDtypeStruct((d, LANES), BF16)],
        compiler_params=pltpu.CompilerParams(
            dimension_semantics=("arbitrary",), vmem_limit_bytes=VMEM_LIMIT),
        name="pack_w",
    )(w_t, w_t)


def _proj_kernel(x_ref, w_ref, wlr_ref, up_ref, gbias_ref, lf_ref, lb_ref, o_ref):
    xb = x_ref[...].astype(BF16)
    mm = lambda w: jnp.dot(xb, w, preferred_element_type=F32)
    silu = lambda a: a * _sigmoid(a)

    def put(name, val):
        o_ref[:, COL[name]:COL[name] + val.shape[1]] = val.astype(BF16)

    def head(name):
        lo = _ORIG_START[name]
        return mm(w_ref[:, lo:lo + _SIZE[name]])

    def tail(name):
        lo = _ORIG_START[name] - (_LR_HI - _LR_LO)
        return mm(w_ref[:, lo:lo + _SIZE[name]])

    def log2_forget(h, logits_ref):
        lb = _lower_bound(logits_ref[...], 0)
        return jnp.log2(lb + (1.0 - lb) * _sigmoid(h))

    put("a_q", head("a_q") * (GLA_DK ** -0.5))
    put("a_k", head("a_k"))
    put("a_v", head("a_v"))
    put("a_gate", silu(head("a_gate")))
    z = jnp.dot(mm(wlr_ref[...]).astype(BF16), up_ref[...], preferred_element_type=F32) + gbias_ref[...]
    log_sig = jnp.minimum(z, 0.0) - jnp.log2(1.0 + jnp.exp2(jnp.abs(z) * (-LOG2E))) * (1.0 / LOG2E)
    put("g_f", log_sig * (LOG2E / GLA_GATE_NORMALIZER))
    put("h_q", silu(tail("h_q")) * (HGRN_DK ** -0.5))
    put("h_ff", log2_forget(tail("h_ff"), lf_ref))
    put("h_fb", log2_forget(tail("h_fb"), lb_ref))
    put("h_i", tail("h_i"))
    put("h_gate", silu(tail("h_gate")))
    put("m_gla", _sigmoid(tail("m_gla")))
    put("m_hgrn", _sigmoid(tail("m_hgrn")))


def _project(x2d, w_packed, w_lr, up, gbias, logits_f, logits_b):
    n = x2d.shape[0]
    assert n % PROJ_TM == 0
    resident = lambda a: pl.BlockSpec(a.shape, lambda i: (0,) * a.ndim,
                                      pipeline_mode=pl.Buffered(1))
    consts = (w_packed, w_lr, up, gbias, logits_f, logits_b)
    return pl.pallas_call(
        _proj_kernel,
        grid=(n // PROJ_TM,),
        in_specs=[pl.BlockSpec((PROJ_TM, D_MODEL), lambda i: (i, 0))] + [resident(a) for a in consts],
        out_specs=pl.BlockSpec((PROJ_TM, PROJ_COLS), lambda i: (i, 0)),
        out_shape=jax.ShapeDtypeStruct((n, PROJ_COLS), BF16),
        compiler_params=pltpu.CompilerParams(
            dimension_semantics=("arbitrary",), vmem_limit_bytes=VMEM_LIMIT),
        name="proj",
    )(x2d, *consts)


def _dot_nt(a, b):
    return lax.dot_general(a, b, (((1,), (1,)), ((), ())), preferred_element_type=F32)


def _dot_tn(a, b):
    return lax.dot_general(a, b, (((0,), (0,)), ((), ())), preferred_element_type=F32)


def _block_diag_cumsum_mat(rows, c):
    row = lax.broadcasted_iota(jnp.int32, (rows, rows), 0)
    col = lax.broadcasted_iota(jnp.int32, (rows, rows), 1)
    return jnp.where(((row // c) == (col // c)) & (col <= row), 1.0, 0.0).astype(BF16)


def _scan_core(q_ref, k_of, g_refs, v_ref, o_ref, scratch, seq, dk, dv):
    qi_refs, upd_refs, dec_refs, oacc_ref, cum_ref = (scratch[0:2], scratch[2:4], scratch[4:6],
                                                       scratch[6], scratch[7])
    c = CHUNK
    tile = SCAN_TILE
    per_tile = tile // c
    n_tiles = seq // tile
    n = seq // c
    mid = c // 2
    row = lax.broadcasted_iota(jnp.int32, (c, c), 0)
    col = lax.broadcasted_iota(jnp.int32, (c, c), 1)
    masks = (col <= row, col >= row)
    cum_mat = _block_diag_cumsum_mat(CUMSUM_ROWS, c)

    def tile_rows(t):
        return pl.ds(pl.multiple_of(t * tile, tile), tile)

    def stage(t):
        g_pair = jnp.concatenate([g_refs[0][0, tile_rows(t), :], g_refs[1][0, tile_rows(t), :]], axis=1)
        for r in range(0, tile, CUMSUM_ROWS):
            cum_ref[r:r + CUMSUM_ROWS, :] = jnp.dot(cum_mat, g_pair[r:r + CUMSUM_ROWS],
                                                    preferred_element_type=F32)

    stage(0)

    def phase1(i, carry):
        rows = tile_rows(i)
        q = q_ref[0, rows, :].astype(F32)
        gs = (g_refs[0][0, rows, :].astype(F32), g_refs[1][0, rows, :].astype(F32))
        ks = (k_of(0, gs[0], rows), k_of(1, gs[1], rows))
        cum = cum_ref[...]
        v = v_ref[0, rows, :]
        slices = [slice(j * c, (j + 1) * c) for j in range(per_tile)]
        mixed, kps = [], []
        for j, sl in enumerate(slices):
            chunk_rows = pl.ds(pl.multiple_of(i * tile + j * c, c), c)
            acc, kp_pair = None, []
            for d in range(2):
                if d == 0:
                    b = cum[sl, :dk]
                    b_mid = b[mid:mid + 1, :]
                    total = b[c - 1:c, :]
                    rel, lead, trail = b - b_mid, b_mid, total - b_mid
                else:
                    incl = cum[sl, dk:]
                    e = incl - gs[1][sl]
                    e_mid = e[mid:mid + 1, :]
                    total = incl[c - 1:c, :]
                    rel, lead, trail = e_mid - e, total - e_mid, e_mid
                qm = q[sl] * jnp.exp2(rel)
                km = ks[d][sl] * jnp.exp2(-rel)
                scores = jnp.where(masks[d], _dot_nt(qm.astype(BF16), km.astype(BF16)), 0.0)
                acc = scores if acc is None else acc + scores
                qi_refs[d][chunk_rows, :] = (qm * jnp.exp2(lead)).astype(BF16)
                kp_pair.append((km * jnp.exp2(trail)).astype(BF16))
                dec_refs[d][i * per_tile + j] = jnp.broadcast_to(jnp.exp2(total), (LANES, dk)).T
            mixed.append(acc.astype(BF16))
            kps.append(jnp.concatenate(kp_pair, axis=1))
        for j, sl in enumerate(slices):
            upd = _dot_tn(kps[j], v[sl])
            upd_refs[0][i * per_tile + j] = upd[:dk]
            upd_refs[1][i * per_tile + j] = upd[dk:]
        stage(jnp.minimum(i + 1, n_tiles - 1))
        for j, sl in enumerate(slices):
            chunk_rows = pl.ds(pl.multiple_of(i * tile + j * c, c), c)
            oacc_ref[chunk_rows, :] = jnp.dot(mixed[j], v[sl], preferred_element_type=F32)
        return carry

    lax.fori_loop(0, n_tiles, phase1, 0)

    def advance(d, ci, st, final):
        rows = pl.ds(pl.multiple_of(ci * c, c), c)
        o = oacc_ref[rows, :] + jnp.dot(qi_refs[d][rows, :], st.astype(BF16),
                                        preferred_element_type=F32)
        if final:
            o_ref[0, rows, :] = o.astype(o_ref.dtype)
        else:
            oacc_ref[rows, :] = o
        decay = jnp.concatenate([dec_refs[d][ci]] * (dv // LANES), axis=1)
        return st * decay + upd_refs[d][ci]

    zero = jnp.zeros((dk, dv), F32)
    if 2 * dk * dv <= STATE_VREG_BUDGET * VREG_ELEMS:
        both = lambda final: (lambda i, s: (advance(0, i, s[0], final),
                                            advance(1, n - 1 - i, s[1], final)))
        states = lax.fori_loop(0, n // 2, both(False), (zero, zero), unroll=PHASE2_UNROLL)
        lax.fori_loop(n // 2, n, both(True), states, unroll=PHASE2_UNROLL)
    else:
        lax.fori_loop(0, n, lambda i, s: advance(0, i, s, False), zero, unroll=PHASE2_UNROLL)
        lax.fori_loop(0, n, lambda i, s: advance(1, n - 1 - i, s, True), zero, unroll=PHASE2_UNROLL)


def _scan_scratch(seq, dk, dv):
    per_dir = lambda shape, dt: [pltpu.VMEM(shape, dt), pltpu.VMEM(shape, dt)]
    n = seq // CHUNK
    return (per_dir((seq, dk), BF16)
            + per_dir((n, dk, dv), F32)
            + per_dir((n, dk, LANES), F32)
            + [pltpu.VMEM((seq, dv), F32)]
            + [pltpu.VMEM((SCAN_TILE, 2 * dk), F32)])


def _scan_call(kernel_fn, name, p3, specs, heads, dk, dv):
    bsz, seq, _ = p3.shape
    return pl.pallas_call(
        functools.partial(kernel_fn, seq=seq),
        grid=(bsz, heads),
        in_specs=specs,
        out_specs=pl.BlockSpec((1, seq, dv), lambda b, h: (b, 0, h)),
        out_shape=jax.ShapeDtypeStruct((bsz, seq, heads * dv), BF16),
        scratch_shapes=_scan_scratch(seq, dk, dv),
        compiler_params=pltpu.CompilerParams(
            dimension_semantics=("arbitrary", "arbitrary"), vmem_limit_bytes=VMEM_LIMIT),
        name=name,
    )(*([p3] * len(specs)))


def _head_block(seq, name, width):
    first = COL[name] // width
    return pl.BlockSpec((1, seq, width), lambda b, h: (b, 0, first + h))


def _gla_kernel(q_ref, k_ref, gf_ref, gb_ref, v_ref, o_ref, *scratch, seq):
    k_of = lambda d, g, rows: k_ref[0, rows, :].astype(F32)
    _scan_core(q_ref, k_of, (gf_ref, gb_ref), v_ref, o_ref, scratch, seq, GLA_DK, GLA_DV)


def _gla_scan(p3):
    seq = p3.shape[1]
    specs = [_head_block(seq, "a_q", GLA_DK), _head_block(seq, "a_k", GLA_DK),
             _head_block(seq, "g_f", GLA_DK), _head_block(seq, "g_b", GLA_DK),
             _head_block(seq, "a_v", GLA_DV)]
    return _scan_call(_gla_kernel, "gla_scan", p3, specs, GLA_HEADS, GLA_DK, GLA_DV)


def _hgrn_kernel(q_ref, gf_ref, gb_ref, v_ref, o_ref, *scratch, seq):
    k_of = lambda d, g, rows: 1.0 - jnp.exp2(g)
    _scan_core(q_ref, k_of, (gf_ref, gb_ref), v_ref, o_ref, scratch, seq, HGRN_DK, HGRN_DV)


def _hgrn_scan(p3):
    seq = p3.shape[1]
    specs = [_head_block(seq, "h_q", HGRN_DK), _head_block(seq, "h_ff", HGRN_DK),
             _head_block(seq, "h_fb", HGRN_DK), _head_block(seq, "h_i", HGRN_DV)]
    return _scan_call(_hgrn_kernel, "hgrn_scan", p3, specs, HGRN_HEADS, HGRN_DK, HGRN_DV)


def _head_rmsnorm_gate(o, gain, gate, heads, dv):
    parts = []
    for h in range(heads):
        seg = o[:, h * dv:(h + 1) * dv]
        ms = jnp.mean(seg * seg, axis=-1, keepdims=True)
        parts.append(seg * lax.rsqrt(ms + RMS_EPS) * gain)
    return (jnp.concatenate(parts, axis=1) * gate).astype(BF16)


def _merge_kernel(oa_ref, oh_ref, ag_ref, hg_ref, mg_ref, mh_ref, x_ref, wa_ref, wh_ref, wo_ref,
                  ga_ref, gh_ref, lng_ref, lnb_ref, out_ref):
    na = _head_rmsnorm_gate(oa_ref[...].astype(F32), ga_ref[...], ag_ref[...].astype(F32),
                            GLA_HEADS, GLA_DV)
    nh = _head_rmsnorm_gate(oh_ref[...].astype(F32), gh_ref[...], hg_ref[...].astype(F32),
                            HGRN_HEADS, HGRN_DV)
    y_gla = jnp.dot(na, wa_ref[...], preferred_element_type=F32)
    y_hgrn = jnp.dot(nh, wh_ref[...], preferred_element_type=F32)
    y = mg_ref[...].astype(F32) * y_gla + mh_ref[...].astype(F32) * y_hgrn
    y = jnp.dot(y.astype(BF16), wo_ref[...], preferred_element_type=F32)
    r = DEEPNORM_ALPHA * x_ref[...] + y
    mu = jnp.mean(r, axis=-1, keepdims=True)
    rc = r - mu
    var = jnp.mean(rc * rc, axis=-1, keepdims=True)
    out_ref[...] = rc * lax.rsqrt(var + LN_EPS) * lng_ref[...] + lnb_ref[...]


def _merge(o_gla, o_hgrn, p2, x2d, w_gla, w_hgrn, w_out, gain_a, gain_h, ln_g, ln_b):
    n = x2d.shape[0]
    tm = MERGE_TM
    wide = lambda name: pl.BlockSpec((tm, D_MODEL), lambda i, name=name: (i, COL[name] // D_MODEL))
    row = lambda: pl.BlockSpec((tm, D_MODEL), lambda i: (i, 0))
    full = lambda shape: pl.BlockSpec(shape, lambda i: (0,) * len(shape))
    return pl.pallas_call(
        _merge_kernel,
        grid=(n // tm,),
        in_specs=[row(), row(), wide("a_gate"), wide("h_gate"), wide("m_gla"), wide("m_hgrn"), row(),
                  full((GLA_V, D_MODEL)), full((HG_V, D_MODEL)), full((D_MODEL, D_MODEL)),
                  full((1, GLA_DV)), full((1, HGRN_DV)), full((1, D_MODEL)), full((1, D_MODEL))],
        out_specs=row(),
        out_shape=jax.ShapeDtypeStruct((n, D_MODEL), F32),
        compiler_params=pltpu.CompilerParams(
            dimension_semantics=("arbitrary",), vmem_limit_bytes=VMEM_LIMIT),
        name="merge",
    )(o_gla, o_hgrn, p2, p2, p2, p2, x2d, w_gla, w_hgrn, w_out, gain_a, gain_h, ln_g, ln_b)


def _gate_params(up_f, up_b, bias_f, bias_b):
    rank = up_f.shape[0]
    up = jnp.zeros((LANES, 2 * GLA_QK), F32)
    up = up.at[:rank, :GLA_QK].set(up_f).at[rank:2 * rank, GLA_QK:].set(up_b)
    bias = jnp.concatenate([bias_f, bias_b])[None, :]
    return up.astype(BF16), bias


def kernel(x, w_in, gla_gk_up_f, gla_gk_bias_f, gla_gk_up_b, gla_gk_bias_b, gla_norm_g,
           hgrn_lb_logits_f, hgrn_lb_logits_b, hgrn_norm_g, w_branch_gla, w_branch_hgrn,
           w_out, ln_g, ln_b):
    assert w_in.shape[0] == DEPTH == 1
    bsz, seq, d = x.shape
    assert d == D_MODEL and seq % SCAN_TILE == 0 and SCAN_TILE % CUMSUM_ROWS == 0
    x2d = x.astype(F32).reshape(bsz * seq, d)

    w_t = jnp.swapaxes(w_in, 1, 2).reshape(w_in.shape[2], w_in.shape[1]).astype(F32)
    w_packed, w_lr = _pack_w(w_t)
    up, gbias = _gate_params(gla_gk_up_f[0].astype(F32), gla_gk_up_b[0].astype(F32),
                             gla_gk_bias_f[0].astype(F32), gla_gk_bias_b[0].astype(F32))
    p2 = _project(x2d, w_packed, w_lr, up, gbias,
                  hgrn_lb_logits_f.astype(F32), hgrn_lb_logits_b.astype(F32))
    p3 = p2.reshape(bsz, seq, PROJ_COLS)

    o_gla = _gla_scan(p3)
    o_hgrn = _hgrn_scan(p3)

    out = _merge(
        o_gla.reshape(bsz * seq, GLA_V), o_hgrn.reshape(bsz * seq, HG_V), p2, x2d,
        w_branch_gla[0].astype(BF16), w_branch_hgrn[0].astype(BF16), w_out[0].astype(BF16),
        gla_norm_g[0].astype(F32).reshape(1, GLA_DV), hgrn_norm_g[0].astype(F32).reshape(1, HGRN_DV),
        ln_g[0].astype(F32).reshape(1, D_MODEL), ln_b[0].astype(F32).reshape(1, D_MODEL))
    return out.reshape(bsz, seq, d).astype(x.dtype)
```

```python
import functools

import jax
import jax.numpy as jnp
import numpy as np
from jax import lax
from jax.experimental import pallas as pl
from jax.experimental.pallas import tpu as pltpu

F32 = jnp.float32
BF16 = jnp.bfloat16

D_MODEL = 1024
DEPTH = 1
GLA_HEADS = 4
GLA_DK = 128
GLA_DV = 256
GLA_GATE_RANK = 16
GLA_GATE_NORMALIZER = 16.0
HGRN_HEADS = 8
HGRN_DK = 128
HGRN_DV = 128
GLA_QK = GLA_HEADS * GLA_DK
GLA_V = GLA_HEADS * GLA_DV
HG_K = HGRN_HEADS * HGRN_DK
HG_V = HGRN_HEADS * HGRN_DV
RMS_EPS = 1e-6
LN_EPS = 1e-5
DEEPNORM_ALPHA = (2.0 * DEPTH) ** 0.25
LOG2E = 1.4426950408889634

LANES = 128
VREG_ELEMS = 8 * LANES
STATE_VREG_BUDGET = 32

_ORIG_SIZES = (GLA_QK, GLA_QK, GLA_V, GLA_V, GLA_GATE_RANK, GLA_GATE_RANK,
               HG_K, HG_K, HG_K, HG_V, HG_V, D_MODEL, D_MODEL)
_ORIG_NAMES = ("a_q", "a_k", "a_v", "a_gate", "lr_f", "lr_b",
               "h_q", "h_ff", "h_fb", "h_i", "h_gate", "m_gla", "m_hgrn")
_ORIG_START = dict(zip(_ORIG_NAMES, np.cumsum((0,) + _ORIG_SIZES[:-1]).tolist()))
_SIZE = dict(zip(_ORIG_NAMES, _ORIG_SIZES))
_LR_LO = _ORIG_START["lr_f"]
_LR_HI = _ORIG_START["lr_b"] + _SIZE["lr_b"]

_OUT_ORDER = ("a_q", "a_k", "a_v", "a_gate", "g_f", "g_b",
              "h_q", "h_ff", "h_fb", "h_i", "h_gate", "m_gla", "m_hgrn")
_OUT_SIZE = dict(_SIZE, g_f=GLA_QK, g_b=GLA_QK)
COL = dict(zip(_OUT_ORDER, np.cumsum([0] + [_OUT_SIZE[nm] for nm in _OUT_ORDER[:-1]]).tolist()))
PROJ_COLS = sum(_OUT_SIZE[nm] for nm in _OUT_ORDER)

CHUNK = 64
SCAN_TILE = 512
CUMSUM_ROWS = 256
PHASE2_UNROLL = 16
PROJ_TM = 512
WPACK_TN = 1024
MERGE_TM = 512
VMEM_LIMIT = 60 * 1024 * 1024


def _sigmoid(x):
    return 1.0 / (1.0 + jnp.exp2(x * (-LOG2E)))


def _lower_bound(logits, layer):
    m = jnp.max(logits, axis=0, keepdims=True)
    e = jnp.exp(logits - m)
    return jnp.sum(e[:layer + 1], axis=0, keepdims=True) / jnp.sum(e, axis=0, keepdims=True)


def _pack_w_kernel(a_ref, b_ref, o_ref, lr_ref):
    j = pl.program_id(0)
    gap = _LR_HI - _LR_LO
    first_shifted = _LR_LO // WPACK_TN

    @pl.when(j < first_shifted)
    def _():
        o_ref[...] = a_ref[...].T.astype(BF16)

    @pl.when(j == first_shifted - 1)
    def _():
        padded = jnp.concatenate([b_ref[...], jnp.zeros((LANES - gap, b_ref.shape[1]), F32)], axis=0)
        lr_ref[...] = padded.T.astype(BF16)

    @pl.when(j >= first_shifted)
    def _():
        rows = jnp.concatenate([a_ref[gap:, :], b_ref[...]], axis=0)
        o_ref[...] = rows.T.astype(BF16)


def _pack_w(w_t):
    d, gap = w_t.shape[1], _LR_HI - _LR_LO
    width = w_t.shape[0] - gap
    assert _LR_LO % WPACK_TN == 0 and _LR_LO > 0 and width % WPACK_TN == 0 and WPACK_TN % gap == 0
    return pl.pallas_call(
        _pack_w_kernel,
        grid=(width // WPACK_TN,),
        in_specs=[pl.BlockSpec((WPACK_TN, d), lambda j: (j, 0)),
                  pl.BlockSpec((gap, d), lambda j: ((j + 1) * (WPACK_TN // gap), 0))],
        out_specs=[pl.BlockSpec((d, WPACK_TN), lambda j: (0, j)),
                   pl.BlockSpec((d, LANES), lambda j: (0, 0))],
        out_shape=[jax.ShapeDtypeStruct((d, width), BF16), jax.ShapeDtypeStruct((d, LANES), BF16)],
        compiler_params=pltpu.CompilerParams(
            dimension_semantics=("arbitrary",), vmem_limit_bytes=VMEM_LIMIT),
        name="pack_w",
    )(w_t, w_t)


def _proj_kernel(x_ref, w_ref, wlr_ref, up_ref, gbias_ref, lf_ref, lb_ref, o_ref):
    xb = x_ref[...].astype(BF16)
    mm = lambda w: jnp.dot(xb, w, preferred_element_type=F32)
    silu = lambda a: a * _sigmoid(a)

    def put(name, val):
        o_ref[:, COL[name]:COL[name] + val.shape[1]] = val.astype(BF16)

    def head(name):
        lo = _ORIG_START[name]
        return mm(w_ref[:, lo:lo + _SIZE[name]])

    def tail(name):
        lo = _ORIG_START[name] - (_LR_HI - _LR_LO)
        return mm(w_ref[:, lo:lo + _SIZE[name]])

    def log2_forget(h, logits_ref):
        lb = _lower_bound(logits_ref[...], 0)
        return jnp.log2(lb + (1.0 - lb) * _sigmoid(h))

    put("a_q", head("a_q") * (GLA_DK ** -0.5))
    put("a_k", head("a_k"))
    put("a_v", head("a_v"))
    put("a_gate", silu(head("a_gate")))
    z = jnp.dot(mm(wlr_ref[...]).astype(BF16), up_ref[...], preferred_element_type=F32) + gbias_ref[...]
    log_sig = jnp.minimum(z, 0.0) - jnp.log2(1.0 + jnp.exp2(jnp.abs(z) * (-LOG2E))) * (1.0 / LOG2E)
    put("g_f", log_sig * (LOG2E / GLA_GATE_NORMALIZER))
    put("h_q", silu(tail("h_q")) * (HGRN_DK ** -0.5))
    put("h_ff", log2_forget(tail("h_ff"), lf_ref))
    put("h_fb", log2_forget(tail("h_fb"), lb_ref))
    put("h_i", tail("h_i"))
    put("h_gate", silu(tail("h_gate")))
    put("m_gla", _sigmoid(tail("m_gla")))
    put("m_hgrn", _sigmoid(tail("m_hgrn")))


def _project(x2d, w_packed, w_lr, up, gbias, logits_f, logits_b):
    n = x2d.shape[0]
    assert n % PROJ_TM == 0
    resident = lambda a: pl.BlockSpec(a.shape, lambda i: (0,) * a.ndim,
                                      pipeline_mode=pl.Buffered(1))
    consts = (w_packed, w_lr, up, gbias, logits_f, logits_b)
    return pl.pallas_call(
        _proj_kernel,
        grid=(n // PROJ_TM,),
        in_specs=[pl.BlockSpec((PROJ_TM, D_MODEL), lambda i: (i, 0))] + [resident(a) for a in consts],
        out_specs=pl.BlockSpec((PROJ_TM, PROJ_COLS), lambda i: (i, 0)),
        out_shape=jax.ShapeDtypeStruct((n, PROJ_COLS), BF16),
        compiler_params=pltpu.CompilerParams(
            dimension_semantics=("arbitrary",), vmem_limit_bytes=VMEM_LIMIT),
        name="proj",
    )(x2d, *consts)


def _dot_nt(a, b):
    return lax.dot_general(a, b, (((1,), (1,)), ((), ())), preferred_element_type=F32)


def _dot_tn(a, b):
    return lax.dot_general(a, b, (((0,), (0,)), ((), ())), preferred_element_type=F32)


def _block_diag_cumsum_mat(rows, c):
    row = lax.broadcasted_iota(jnp.int32, (rows, rows), 0)
    col = lax.broadcasted_iota(jnp.int32, (rows, rows), 1)
    return jnp.where(((row // c) == (col // c)) & (col <= row), 1.0, 0.0).astype(BF16)


def _scan_core(q_ref, k_of, g_refs, v_ref, o_ref, scratch, seq, dk, dv):
    qi_refs, upd_refs, dec_refs, oacc_ref, cum_ref = (scratch[0:2], scratch[2:4], scratch[4:6],
                                                       scratch[6], scratch[7])
    c = CHUNK
    pair = 2 * c
    tile = SCAN_TILE
    pairs_per_tile = tile // pair
    n_tiles = seq // tile
    n = seq // pair
    mid = c // 2
    row = lax.broadcasted_iota(jnp.int32, (pair, pair), 0)
    col = lax.broadcasted_iota(jnp.int32, (pair, pair), 1)
    masks = (col <= row, col >= row)
    cum_mat = _block_diag_cumsum_mat(CUMSUM_ROWS, c)

    def tile_rows(t):
        return pl.ds(pl.multiple_of(t * tile, tile), tile)

    def stage(t):
        g_pair = jnp.concatenate([g_refs[0][0, tile_rows(t), :], g_refs[1][0, tile_rows(t), :]], axis=1)
        for r in range(0, tile, CUMSUM_ROWS):
            cum_ref[r:r + CUMSUM_ROWS, :] = jnp.dot(cum_mat, g_pair[r:r + CUMSUM_ROWS],
                                                    preferred_element_type=F32)

    stage(0)

    def chunk_terms(d, q, k, g, cum):
        if d == 0:
            b_mid = cum[mid:mid + 1, :]
            total = cum[c - 1:c, :]
            rel, lead, trail = cum - b_mid, b_mid, total - b_mid
        else:
            e = cum - g
            e_mid = e[mid:mid + 1, :]
            total = cum[c - 1:c, :]
            rel, lead, trail = e_mid - e, total - e_mid, e_mid
        return q * jnp.exp2(rel), k * jnp.exp2(-rel), lead, trail, total

    def phase1(i, carry):
        rows = tile_rows(i)
        q = q_ref[0, rows, :].astype(F32)
        gs = (g_refs[0][0, rows, :].astype(F32), g_refs[1][0, rows, :].astype(F32))
        ks = (k_of(0, gs[0], rows), k_of(1, gs[1], rows))
        cum = cum_ref[...]
        v = v_ref[0, rows, :]
        bf = lambda a: a.astype(BF16)
        mixed, kps = [], []
        for p in range(pairs_per_tile):
            pair_rows = pl.ds(pl.multiple_of(i * tile + p * pair, pair), pair)
            sa, sb = slice(p * pair, p * pair + c), slice(p * pair + c, (p + 1) * pair)
            acc, kp_pair = None, []
            for d in range(2):
                lanes = slice(d * dk, (d + 1) * dk)
                qa, ka, lead_a, trail_a, tot_a = chunk_terms(d, q[sa], ks[d][sa], gs[d][sa], cum[sa, lanes])
                qb, kb, lead_b, trail_b, tot_b = chunk_terms(d, q[sb], ks[d][sb], gs[d][sb], cum[sb, lanes])
                if d == 0:
                    keys_a = jnp.concatenate([bf(ka), bf(kb)], axis=0)
                    keys_b = jnp.concatenate([bf(ka * jnp.exp2(trail_a + lead_b)), bf(kb)], axis=0)
                    qi = jnp.concatenate([qa * jnp.exp2(lead_a), qb * jnp.exp2(lead_b + tot_a)], axis=0)
                    kp = jnp.concatenate([ka * jnp.exp2(trail_a + tot_b), kb * jnp.exp2(trail_b)], axis=0)
                else:
                    keys_a = jnp.concatenate([bf(ka), bf(kb * jnp.exp2(trail_b + lead_a))], axis=0)
                    keys_b = jnp.concatenate([bf(ka), bf(kb)], axis=0)
                    qi = jnp.concatenate([qa * jnp.exp2(lead_a + tot_b), qb * jnp.exp2(lead_b)], axis=0)
                    kp = jnp.concatenate([ka * jnp.exp2(trail_a), kb * jnp.exp2(trail_b + tot_a)], axis=0)
                scores = jnp.concatenate([_dot_nt(bf(qa), keys_a), _dot_nt(bf(qb), keys_b)], axis=0)
                scores = jnp.where(masks[d], scores, 0.0)
                acc = scores if acc is None else acc + scores
                qi_refs[d][pair_rows, :] = bf(qi)
                kp_pair.append(bf(kp))
                dec_refs[d][i * pairs_per_tile + p] = jnp.broadcast_to(
                    jnp.exp2(tot_a + tot_b), (LANES, dk)).T
            mixed.append(bf(acc))
            kps.append(jnp.concatenate(kp_pair, axis=1))
        pair_slices = [slice(p * pair, (p + 1) * pair) for p in range(pairs_per_tile)]
        for p, sl in enumerate(pair_slices):
            upd = _dot_tn(kps[p], v[sl])
            upd_refs[0][i * pairs_per_tile + p] = upd[:dk]
            upd_refs[1][i * pairs_per_tile + p] = upd[dk:]
        stage(jnp.minimum(i + 1, n_tiles - 1))
        for p, sl in enumerate(pair_slices):
            pair_rows = pl.ds(pl.multiple_of(i * tile + p * pair, pair), pair)
            oacc_ref[pair_rows, :] = jnp.dot(mixed[p], v[sl], preferred_element_type=F32)
        return carry

    lax.fori_loop(0, n_tiles, phase1, 0)

    def advance(d, ci, st, final):
        rows = pl.ds(pl.multiple_of(ci * pair, pair), pair)
        o = oacc_ref[rows, :] + jnp.dot(qi_refs[d][rows, :], st.astype(BF16),
                                        preferred_element_type=F32)
        if final:
            o_ref[0, rows, :] = o.astype(o_ref.dtype)
        else:
            oacc_ref[rows, :] = o
        decay = jnp.concatenate([dec_refs[d][ci]] * (dv // LANES), axis=1)
        return st * decay + upd_refs[d][ci]

    zero = jnp.zeros((dk, dv), F32)
    if 2 * dk * dv <= STATE_VREG_BUDGET * VREG_ELEMS:
        both = lambda final: (lambda i, s: (advance(0, i, s[0], final),
                                            advance(1, n - 1 - i, s[1], final)))
        states = lax.fori_loop(0, n // 2, both(False), (zero, zero), unroll=PHASE2_UNROLL)
        lax.fori_loop(n // 2, n, both(True), states, unroll=PHASE2_UNROLL)
    else:
        lax.fori_loop(0, n, lambda i, s: advance(0, i, s, False), zero, unroll=PHASE2_UNROLL)
        lax.fori_loop(0, n, lambda i, s: advance(1, n - 1 - i, s, True), zero, unroll=PHASE2_UNROLL)


def _scan_scratch(seq, dk, dv):
    per_dir = lambda shape, dt: [pltpu.VMEM(shape, dt), pltpu.VMEM(shape, dt)]
    n = seq // (2 * CHUNK)
    return (per_dir((seq, dk), BF16)
            + per_dir((n, dk, dv), F32)
            + per_dir((n, dk, LANES), F32)
            + [pltpu.VMEM((seq, dv), F32)]
            + [pltpu.VMEM((SCAN_TILE, 2 * dk), F32)])


def _scan_call(kernel_fn, name, p3, specs, heads, dk, dv):
    bsz, seq, _ = p3.shape
    return pl.pallas_call(
        functools.partial(kernel_fn, seq=seq),
        grid=(bsz, heads),
        in_specs=specs,
        out_specs=pl.BlockSpec((1, seq, dv), lambda b, h: (b, 0, h)),
        out_shape=jax.ShapeDtypeStruct((bsz, seq, heads * dv), BF16),
        scratch_shapes=_scan_scratch(seq, dk, dv),
        compiler_params=pltpu.CompilerParams(
            dimension_semantics=("arbitrary", "arbitrary"), vmem_limit_bytes=VMEM_LIMIT),
        name=name,
    )(*([p3] * len(specs)))


def _head_block(seq, name, width):
    first = COL[name] // width
    return pl.BlockSpec((1, seq, width), lambda b, h: (b, 0, first + h))


def _gla_kernel(q_ref, k_ref, gf_ref, gb_ref, v_ref, o_ref, *scratch, seq):
    k_of = lambda d, g, rows: k_ref[0, rows, :].astype(F32)
    _scan_core(q_ref, k_of, (gf_ref, gb_ref), v_ref, o_ref, scratch, seq, GLA_DK, GLA_DV)


def _gla_scan(p3):
    seq = p3.shape[1]
    specs = [_head_block(seq, "a_q", GLA_DK), _head_block(seq, "a_k", GLA_DK),
             _head_block(seq, "g_f", GLA_DK), _head_block(seq, "g_b", GLA_DK),
             _head_block(seq, "a_v", GLA_DV)]
    return _scan_call(_gla_kernel, "gla_scan", p3, specs, GLA_HEADS, GLA_DK, GLA_DV)


def _hgrn_kernel(q_ref, gf_ref, gb_ref, v_ref, o_ref, *scratch, seq):
    k_of = lambda d, g, rows: 1.0 - jnp.exp2(g)
    _scan_core(q_ref, k_of, (gf_ref, gb_ref), v_ref, o_ref, scratch, seq, HGRN_DK, HGRN_DV)


def _hgrn_scan(p3):
    seq = p3.shape[1]
    specs = [_head_block(seq, "h_q", HGRN_DK), _head_block(seq, "h_ff", HGRN_DK),
             _head_block(seq, "h_fb", HGRN_DK), _head_block(seq, "h_i", HGRN_DV)]
    return _scan_call(_hgrn_kernel, "hgrn_scan", p3, specs, HGRN_HEADS, HGRN_DK, HGRN_DV)


def _head_rmsnorm_gate(o, gain, gate, heads, dv):
    parts = []
    for h in range(heads):
        seg = o[:, h * dv:(h + 1) * dv]
        ms = jnp.mean(seg * seg, axis=-1, keepdims=True)
        parts.append(seg * lax.rsqrt(ms + RMS_EPS) * gain)
    return (jnp.concatenate(parts, axis=1) * gate).astype(BF16)


def _merge_kernel(oa_ref, oh_ref, ag_ref, hg_ref, mg_ref, mh_ref, x_ref, wa_ref, wh_ref, wo_ref,
                  ga_ref, gh_ref, lng_ref, lnb_ref, out_ref):
    na = _head_rmsnorm_gate(oa_ref[...].astype(F32), ga_ref[...], ag_ref[...].astype(F32),
                            GLA_HEADS, GLA_DV)
    nh = _head_rmsnorm_gate(oh_ref[...].astype(F32), gh_ref[...], hg_ref[...].astype(F32),
                            HGRN_HEADS, HGRN_DV)
    y_gla = jnp.dot(na, wa_ref[...], preferred_element_type=F32)
    y_hgrn = jnp.dot(nh, wh_ref[...], preferred_element_type=F32)
    y = mg_ref[...].astype(F32) * y_gla + mh_ref[...].astype(F32) * y_hgrn
    y = jnp.dot(y.astype(BF16), wo_ref[...], preferred_element_type=F32)
    r = DEEPNORM_ALPHA * x_ref[...] + y
    mu = jnp.mean(r, axis=-1, keepdims=True)
    rc = r - mu
    var = jnp.mean(rc * rc, axis=-1, keepdims=True)
    out_ref[...] = rc * lax.rsqrt(var + LN_EPS) * lng_ref[...] + lnb_ref[...]


def _merge(o_gla, o_hgrn, p2, x2d, w_gla, w_hgrn, w_out, gain_a, gain_h, ln_g, ln_b):
    n = x2d.shape[0]
    tm = MERGE_TM
    wide = lambda name: pl.BlockSpec((tm, D_MODEL), lambda i, name=name: (i, COL[name] // D_MODEL))
    row = lambda: pl.BlockSpec((tm, D_MODEL), lambda i: (i, 0))
    full = lambda shape: pl.BlockSpec(shape, lambda i: (0,) * len(shape))
    return pl.pallas_call(
        _merge_kernel,
        grid=(n // tm,),
        in_specs=[row(), row(), wide("a_gate"), wide("h_gate"), wide("m_gla"), wide("m_hgrn"), row(),
                  full((GLA_V, D_MODEL)), full((HG_V, D_MODEL)), full((D_MODEL, D_MODEL)),
                  full((1, GLA_DV)), full((1, HGRN_DV)), full((1, D_MODEL)), full((1, D_MODEL))],
        out_specs=row(),
        out_shape=jax.ShapeDtypeStruct((n, D_MODEL), F32),
        compiler_params=pltpu.CompilerParams(
            dimension_semantics=("arbitrary",), vmem_limit_bytes=VMEM_LIMIT),
        name="merge",
    )(o_gla, o_hgrn, p2, p2, p2, p2, x2d, w_gla, w_hgrn, w_out, gain_a, gain_h, ln_g, ln_b)


def _gate_params(up_f, up_b, bias_f, bias_b):
    rank = up_f.shape[0]
    up = jnp.zeros((LANES, 2 * GLA_QK), F32)
    up = up.at[:rank, :GLA_QK].set(up_f).at[rank:2 * rank, GLA_QK:].set(up_b)
    bias = jnp.concatenate([bias_f, bias_b])[None, :]
    return up.astype(BF16), bias


def kernel(x, w_in, gla_gk_up_f, gla_gk_bias_f, gla_gk_up_b, gla_gk_bias_b, gla_norm_g,
           hgrn_lb_logits_f, hgrn_lb_logits_b, hgrn_norm_g, w_branch_gla, w_branch_hgrn,
           w_out, ln_g, ln_b):
    assert w_in.shape[0] == DEPTH == 1
    bsz, seq, d = x.shape
    assert d == D_MODEL and seq % SCAN_TILE == 0 and SCAN_TILE % CUMSUM_ROWS == 0
    x2d = x.astype(F32).reshape(bsz * seq, d)

    w_t = jnp.swapaxes(w_in, 1, 2).reshape(w_in.shape[2], w_in.shape[1]).astype(F32)
    w_packed, w_lr = _pack_w(w_t)
    up, gbias = _gate_params(gla_gk_up_f[0].astype(F32), gla_gk_up_b[0].astype(F32),
                             gla_gk_bias_f[0].astype(F32), gla_gk_bias_b[0].astype(F32))
    p2 = _project(x2d, w_packed, w_lr, up, gbias,
                  hgrn_lb_logits_f.astype(F32), hgrn_lb_logits_b.astype(F32))
    p3 = p2.reshape(bsz, seq, PROJ_COLS)

    o_gla = _gla_scan(p3)
    o_hgrn = _hgrn_scan(p3)

    out = _merge(
        o_gla.reshape(bsz * seq, GLA_V), o_hgrn.reshape(bsz * seq, HG_V), p2, x2d,
        w_branch_gla[0].astype(BF16), w_branch_hgrn[0].astype(BF16), w_out[0].astype(BF16),
        gla_norm_g[0].astype(F32).reshape(1, GLA_DV), hgrn_norm_g[0].astype(F32).reshape(1, HGRN_DV),
        ln_g[0].astype(F32).reshape(1, D_MODEL), ln_b[0].astype(F32).reshape(1, D_MODEL))
    return out.reshape(bsz, seq, d).astype(x.dtype)
```

```python
import functools

import jax
import jax.numpy as jnp
import numpy as np
from jax import lax
from jax.experimental import pallas as pl
from jax.experimental.pallas import tpu as pltpu

F32 = jnp.float32
BF16 = jnp.bfloat16

D_MODEL = 1024
DEPTH = 1
GLA_HEADS = 4
GLA_DK = 128
GLA_DV = 256
GLA_GATE_RANK = 16
GLA_GATE_NORMALIZER = 16.0
HGRN_HEADS = 8
HGRN_DK = 128
HGRN_DV = 128
GLA_QK = GLA_HEADS * GLA_DK
GLA_V = GLA_HEADS * GLA_DV
HG_K = HGRN_HEADS * HGRN_DK
HG_V = HGRN_HEADS * HGRN_DV
RMS_EPS = 1e-6
LN_EPS = 1e-5
DEEPNORM_ALPHA = (2.0 * DEPTH) ** 0.25
LOG2E = 1.4426950408889634

LANES = 128
VREG_ELEMS = 8 * LANES
STATE_VREG_BUDGET = 32

_ORIG_SIZES = (GLA_QK, GLA_QK, GLA_V, GLA_V, GLA_GATE_RANK, GLA_GATE_RANK,
               HG_K, HG_K, HG_K, HG_V, HG_V, D_MODEL, D_MODEL)
_ORIG_NAMES = ("a_q", "a_k", "a_v", "a_gate", "lr_f", "lr_b",
               "h_q", "h_ff", "h_fb", "h_i", "h_gate", "m_gla", "m_hgrn")
_ORIG_START = dict(zip(_ORIG_NAMES, np.cumsum((0,) + _ORIG_SIZES[:-1]).tolist()))
_SIZE = dict(zip(_ORIG_NAMES, _ORIG_SIZES))
_LR_LO = _ORIG_START["lr_f"]
_LR_HI = _ORIG_START["lr_b"] + _SIZE["lr_b"]

_OUT_ORDER = ("a_q", "a_k", "a_v", "a_gate", "g_f", "g_b",
              "h_q", "h_ff", "h_fb", "h_i", "h_gate", "m_gla", "m_hgrn")
_OUT_SIZE = dict(_SIZE, g_f=GLA_QK, g_b=GLA_QK)
COL = dict(zip(_OUT_ORDER, np.cumsum([0] + [_OUT_SIZE[nm] for nm in _OUT_ORDER[:-1]]).tolist()))
PROJ_COLS = sum(_OUT_SIZE[nm] for nm in _OUT_ORDER)

CHUNK = 64
SCAN_TILE = 512
CUMSUM_ROWS = 256
PHASE2_UNROLL = 16
PROJ_TM = 512
WPACK_TN = 1024
MERGE_TM = 1024
MERGE_SUB = 256
VMEM_LIMIT = 60 * 1024 * 1024


def _sigmoid(x):
    return 1.0 / (1.0 + jnp.exp2(x * (-LOG2E)))


def _lower_bound(logits, layer):
    m = jnp.max(logits, axis=0, keepdims=True)
    e = jnp.exp(logits - m)
    return jnp.sum(e[:layer + 1], axis=0, keepdims=True) / jnp.sum(e, axis=0, keepdims=True)


def _pack_w_kernel(a_ref, b_ref, o_ref, lr_ref):
    j = pl.program_id(0)
    gap = _LR_HI - _LR_LO
    first_shifted = _LR_LO // WPACK_TN

    @pl.when(j < first_shifted)
    def _():
        o_ref[...] = a_ref[...].T.astype(BF16)

    @pl.when(j == first_shifted - 1)
    def _():
        padded = jnp.concatenate([b_ref[...], jnp.zeros((LANES - gap, b_ref.shape[1]), F32)], axis=0)
        lr_ref[...] = padded.T.astype(BF16)

    @pl.when(j >= first_shifted)
    def _():
        rows = jnp.concatenate([a_ref[gap:, :], b_ref[...]], axis=0)
        o_ref[...] = rows.T.astype(BF16)


def _pack_w(w_t):
    d, gap = w_t.shape[1], _LR_HI - _LR_LO
    width = w_t.shape[0] - gap
    assert _LR_LO % WPACK_TN == 0 and _LR_LO > 0 and width % WPACK_TN == 0 and WPACK_TN % gap == 0
    return pl.pallas_call(
        _pack_w_kernel,
        grid=(width // WPACK_TN,),
        in_specs=[pl.BlockSpec((WPACK_TN, d), lambda j: (j, 0)),
                  pl.BlockSpec((gap, d), lambda j: ((j + 1) * (WPACK_TN // gap), 0))],
        out_specs=[pl.BlockSpec((d, WPACK_TN), lambda j: (0, j)),
                   pl.BlockSpec((d, LANES), lambda j: (0, 0))],
        out_shape=[jax.ShapeDtypeStruct((d, width), BF16), jax.ShapeDtypeStruct((d, LANES), BF16)],
        compiler_params=pltpu.CompilerParams(
            dimension_semantics=("arbitrary",), vmem_limit_bytes=VMEM_LIMIT),
        name="pack_w",
    )(w_t, w_t)


def _proj_kernel(x_ref, w_ref, wlr_ref, up_ref, gbias_ref, lf_ref, lb_ref, o_ref):
    xb = x_ref[...].astype(BF16)
    mm = lambda w: jnp.dot(xb, w, preferred_element_type=F32)
    silu = lambda a: a * _sigmoid(a)

    def put(name, val):
        o_ref[:, COL[name]:COL[name] + val.shape[1]] = val.astype(BF16)

    def head(name):
        lo = _ORIG_START[name]
        return mm(w_ref[:, lo:lo + _SIZE[name]])

    def tail(name):
        lo = _ORIG_START[name] - (_LR_HI - _LR_LO)
        return mm(w_ref[:, lo:lo + _SIZE[name]])

    def log2_forget(h, logits_ref):
        lb = _lower_bound(logits_ref[...], 0)
        return jnp.log2(lb + (1.0 - lb) * _sigmoid(h))

    put("a_q", head("a_q") * (GLA_DK ** -0.5))
    put("a_k", head("a_k"))
    put("a_v", head("a_v"))
    put("a_gate", silu(head("a_gate")))
    z = jnp.dot(mm(wlr_ref[...]).astype(BF16), up_ref[...], preferred_element_type=F32) + gbias_ref[...]
    log_sig = jnp.minimum(z, 0.0) - jnp.log2(1.0 + jnp.exp2(jnp.abs(z) * (-LOG2E))) * (1.0 / LOG2E)
    put("g_f", log_sig * (LOG2E / GLA_GATE_NORMALIZER))
    put("h_q", silu(tail("h_q")) * (HGRN_DK ** -0.5))
    put("h_ff", log2_forget(tail("h_ff"), lf_ref))
    put("h_fb", log2_forget(tail("h_fb"), lb_ref))
    put("h_i", tail("h_i"))
    put("h_gate", silu(tail("h_gate")))
    put("m_gla", _sigmoid(tail("m_gla")))
    put("m_hgrn", _sigmoid(tail("m_hgrn")))


def _project(x2d, w_packed, w_lr, up, gbias, logits_f, logits_b):
    n = x2d.shape[0]
    assert n % PROJ_TM == 0
    resident = lambda a: pl.BlockSpec(a.shape, lambda i: (0,) * a.ndim,
                                      pipeline_mode=pl.Buffered(1))
    consts = (w_packed, w_lr, up, gbias, logits_f, logits_b)
    return pl.pallas_call(
        _proj_kernel,
        grid=(n // PROJ_TM,),
        in_specs=[pl.BlockSpec((PROJ_TM, D_MODEL), lambda i: (i, 0))] + [resident(a) for a in consts],
        out_specs=pl.BlockSpec((PROJ_TM, PROJ_COLS), lambda i: (i, 0)),
        out_shape=jax.ShapeDtypeStruct((n, PROJ_COLS), BF16),
        compiler_params=pltpu.CompilerParams(
            dimension_semantics=("arbitrary",), vmem_limit_bytes=VMEM_LIMIT),
        name="proj",
    )(x2d, *consts)


def _dot_nt(a, b):
    return lax.dot_general(a, b, (((1,), (1,)), ((), ())), preferred_element_type=F32)


def _dot_tn(a, b):
    return lax.dot_general(a, b, (((0,), (0,)), ((), ())), preferred_element_type=F32)


def _block_diag_cumsum_mat(rows, c):
    row = lax.broadcasted_iota(jnp.int32, (rows, rows), 0)
    col = lax.broadcasted_iota(jnp.int32, (rows, rows), 1)
    return jnp.where(((row // c) == (col // c)) & (col <= row), 1.0, 0.0).astype(BF16)


def _scan_core(q_ref, k_of, g_refs, v_ref, o_ref, scratch, seq, dk, dv):
    qi_refs, upd_refs, dec_refs, oacc_ref, cum_ref = (scratch[0:2], scratch[2:4], scratch[4:6],
                                                       scratch[6], scratch[7])
    c = CHUNK
    pair = 2 * c
    tile = SCAN_TILE
    pairs_per_tile = tile // pair
    n_tiles = seq // tile
    n = seq // pair
    mid = c // 2
    row = lax.broadcasted_iota(jnp.int32, (pair, pair), 0)
    col = lax.broadcasted_iota(jnp.int32, (pair, pair), 1)
    masks = (col <= row, col >= row)
    cum_mat = _block_diag_cumsum_mat(CUMSUM_ROWS, c)

    def tile_rows(t):
        return pl.ds(pl.multiple_of(t * tile, tile), tile)

    def stage(t):
        g_pair = jnp.concatenate([g_refs[0][0, tile_rows(t), :], g_refs[1][0, tile_rows(t), :]], axis=1)
        for r in range(0, tile, CUMSUM_ROWS):
            cum_ref[r:r + CUMSUM_ROWS, :] = jnp.dot(cum_mat, g_pair[r:r + CUMSUM_ROWS],
                                                    preferred_element_type=F32)

    stage(0)

    def chunk_terms(d, q, k, g, cum):
        if d == 0:
            b_mid = cum[mid:mid + 1, :]
            total = cum[c - 1:c, :]
            rel, lead, trail = cum - b_mid, b_mid, total - b_mid
        else:
            e = cum - g
            e_mid = e[mid:mid + 1, :]
            total = cum[c - 1:c, :]
            rel, lead, trail = e_mid - e, total - e_mid, e_mid
        return q * jnp.exp2(rel), k * jnp.exp2(-rel), lead, trail, total

    def phase1(i, carry):
        rows = tile_rows(i)
        q = q_ref[0, rows, :].astype(F32)
        gs = (g_refs[0][0, rows, :].astype(F32), g_refs[1][0, rows, :].astype(F32))
        ks = (k_of(0, gs[0], rows), k_of(1, gs[1], rows))
        cum = cum_ref[...]
        v = v_ref[0, rows, :]
        bf = lambda a: a.astype(BF16)
        mixed, kps = [], []
        for p in range(pairs_per_tile):
            pair_rows = pl.ds(pl.multiple_of(i * tile + p * pair, pair), pair)
            sa, sb = slice(p * pair, p * pair + c), slice(p * pair + c, (p + 1) * pair)
            acc, kp_pair = None, []
            for d in range(2):
                lanes = slice(d * dk, (d + 1) * dk)
                qa, ka, lead_a, trail_a, tot_a = chunk_terms(d, q[sa], ks[d][sa], gs[d][sa], cum[sa, lanes])
                qb, kb, lead_b, trail_b, tot_b = chunk_terms(d, q[sb], ks[d][sb], gs[d][sb], cum[sb, lanes])
                if d == 0:
                    keys_a = jnp.concatenate([bf(ka), bf(kb)], axis=0)
                    keys_b = jnp.concatenate([bf(ka * jnp.exp2(trail_a + lead_b)), bf(kb)], axis=0)
                    qi = jnp.concatenate([qa * jnp.exp2(lead_a), qb * jnp.exp2(lead_b + tot_a)], axis=0)
                    kp = jnp.concatenate([ka * jnp.exp2(trail_a + tot_b), kb * jnp.exp2(trail_b)], axis=0)
                else:
                    keys_a = jnp.concatenate([bf(ka), bf(kb * jnp.exp2(trail_b + lead_a))], axis=0)
                    keys_b = jnp.concatenate([bf(ka), bf(kb)], axis=0)
                    qi = jnp.concatenate([qa * jnp.exp2(lead_a + tot_b), qb * jnp.exp2(lead_b)], axis=0)
                    kp = jnp.concatenate([ka * jnp.exp2(trail_a), kb * jnp.exp2(trail_b + tot_a)], axis=0)
                scores = jnp.concatenate([_dot_nt(bf(qa), keys_a), _dot_nt(bf(qb), keys_b)], axis=0)
                scores = jnp.where(masks[d], scores, 0.0)
                acc = scores if acc is None else acc + scores
                qi_refs[d][pair_rows, :] = bf(qi)
                kp_pair.append(bf(kp))
                dec_refs[d][i * pairs_per_tile + p] = jnp.broadcast_to(
                    jnp.exp2(tot_a + tot_b), (LANES, dk)).T
            mixed.append(bf(acc))
            kps.append(jnp.concatenate(kp_pair, axis=1))
        pair_slices = [slice(p * pair, (p + 1) * pair) for p in range(pairs_per_tile)]
        for p, sl in enumerate(pair_slices):
            upd = _dot_tn(kps[p], v[sl])
            upd_refs[0][i * pairs_per_tile + p] = upd[:dk]
            upd_refs[1][i * pairs_per_tile + p] = upd[dk:]
        stage(jnp.minimum(i + 1, n_tiles - 1))
        for p, sl in enumerate(pair_slices):
            pair_rows = pl.ds(pl.multiple_of(i * tile + p * pair, pair), pair)
            oacc_ref[pair_rows, :] = jnp.dot(mixed[p], v[sl], preferred_element_type=F32)
        return carry

    lax.fori_loop(0, n_tiles, phase1, 0)

    def advance(d, ci, st, final):
        rows = pl.ds(pl.multiple_of(ci * pair, pair), pair)
        o = oacc_ref[rows, :] + jnp.dot(qi_refs[d][rows, :], st.astype(BF16),
                                        preferred_element_type=F32)
        if final:
            o_ref[0, rows, :] = o.astype(o_ref.dtype)
        else:
            oacc_ref[rows, :] = o
        decay = jnp.concatenate([dec_refs[d][ci]] * (dv // LANES), axis=1)
        return st * decay + upd_refs[d][ci]

    zero = jnp.zeros((dk, dv), F32)
    if 2 * dk * dv <= STATE_VREG_BUDGET * VREG_ELEMS:
        both = lambda final: (lambda i, s: (advance(0, i, s[0], final),
                                            advance(1, n - 1 - i, s[1], final)))
        states = lax.fori_loop(0, n // 2, both(False), (zero, zero), unroll=PHASE2_UNROLL)
        lax.fori_loop(n // 2, n, both(True), states, unroll=PHASE2_UNROLL)
    else:
        lax.fori_loop(0, n, lambda i, s: advance(0, i, s, False), zero, unroll=PHASE2_UNROLL)
        lax.fori_loop(0, n, lambda i, s: advance(1, n - 1 - i, s, True), zero, unroll=PHASE2_UNROLL)


def _scan_scratch(seq, dk, dv):
    per_dir = lambda shape, dt: [pltpu.VMEM(shape, dt), pltpu.VMEM(shape, dt)]
    n = seq // (2 * CHUNK)
    return (per_dir((seq, dk), BF16)
            + per_dir((n, dk, dv), F32)
            + per_dir((n, dk, LANES), F32)
            + [pltpu.VMEM((seq, dv), F32)]
            + [pltpu.VMEM((SCAN_TILE, 2 * dk), F32)])


def _scan_call(kernel_fn, name, p3, specs, heads, dk, dv):
    bsz, seq, _ = p3.shape
    return pl.pallas_call(
        functools.partial(kernel_fn, seq=seq),
        grid=(bsz, heads),
        in_specs=specs,
        out_specs=pl.BlockSpec((1, seq, dv), lambda b, h: (b, 0, h)),
        out_shape=jax.ShapeDtypeStruct((bsz, seq, heads * dv), BF16),
        scratch_shapes=_scan_scratch(seq, dk, dv),
        compiler_params=pltpu.CompilerParams(
            dimension_semantics=("arbitrary", "arbitrary"), vmem_limit_bytes=VMEM_LIMIT),
        name=name,
    )(*([p3] * len(specs)))


def _head_block(seq, name, width):
    first = COL[name] // width
    return pl.BlockSpec((1, seq, width), lambda b, h: (b, 0, first + h))


def _gla_kernel(q_ref, k_ref, gf_ref, gb_ref, v_ref, o_ref, *scratch, seq):
    k_of = lambda d, g, rows: k_ref[0, rows, :].astype(F32)
    _scan_core(q_ref, k_of, (gf_ref, gb_ref), v_ref, o_ref, scratch, seq, GLA_DK, GLA_DV)


def _gla_scan(p3):
    seq = p3.shape[1]
    specs = [_head_block(seq, "a_q", GLA_DK), _head_block(seq, "a_k", GLA_DK),
             _head_block(seq, "g_f", GLA_DK), _head_block(seq, "g_b", GLA_DK),
             _head_block(seq, "a_v", GLA_DV)]
    return _scan_call(_gla_kernel, "gla_scan", p3, specs, GLA_HEADS, GLA_DK, GLA_DV)


def _hgrn_kernel(q_ref, gf_ref, gb_ref, v_ref, o_ref, *scratch, seq):
    k_of = lambda d, g, rows: 1.0 - jnp.exp2(g)
    _scan_core(q_ref, k_of, (gf_ref, gb_ref), v_ref, o_ref, scratch, seq, HGRN_DK, HGRN_DV)


def _hgrn_scan(p3):
    seq = p3.shape[1]
    specs = [_head_block(seq, "h_q", HGRN_DK), _head_block(seq, "h_ff", HGRN_DK),
             _head_block(seq, "h_fb", HGRN_DK), _head_block(seq, "h_i", HGRN_DV)]
    return _scan_call(_hgrn_kernel, "hgrn_scan", p3, specs, HGRN_HEADS, HGRN_DK, HGRN_DV)


def _head_rmsnorm_gate(o, gain, gate, heads, dv):
    parts = []
    for h in range(heads):
        seg = o[:, h * dv:(h + 1) * dv]
        ms = jnp.mean(seg * seg, axis=-1, keepdims=True)
        parts.append(seg * lax.rsqrt(ms + RMS_EPS) * gain)
    return (jnp.concatenate(parts, axis=1) * gate).astype(BF16)


def _merge_kernel(oa_ref, oh_ref, ag_ref, hg_ref, mg_ref, mh_ref, x_ref, wa_ref, wh_ref, wo_ref,
                  ga_ref, gh_ref, lng_ref, lnb_ref, out_ref):
    subs = [slice(r, r + MERGE_SUB) for r in range(0, out_ref.shape[0], MERGE_SUB)]
    mixed = []
    for sl in subs:
        na = _head_rmsnorm_gate(oa_ref[sl, :].astype(F32), ga_ref[...], ag_ref[sl, :].astype(F32),
                                GLA_HEADS, GLA_DV)
        nh = _head_rmsnorm_gate(oh_ref[sl, :].astype(F32), gh_ref[...], hg_ref[sl, :].astype(F32),
                                HGRN_HEADS, HGRN_DV)
        y_gla = jnp.dot(na, wa_ref[...], preferred_element_type=F32)
        y_hgrn = jnp.dot(nh, wh_ref[...], preferred_element_type=F32)
        mixed.append((mg_ref[sl, :].astype(F32) * y_gla
                      + mh_ref[sl, :].astype(F32) * y_hgrn).astype(BF16))
    for sl, y in zip(subs, mixed):
        r = DEEPNORM_ALPHA * x_ref[sl, :] + jnp.dot(y, wo_ref[...], preferred_element_type=F32)
        mu = jnp.mean(r, axis=-1, keepdims=True)
        rc = r - mu
        var = jnp.mean(rc * rc, axis=-1, keepdims=True)
        out_ref[sl, :] = rc * lax.rsqrt(var + LN_EPS) * lng_ref[...] + lnb_ref[...]


def _merge(o_gla, o_hgrn, p2, x2d, w_gla, w_hgrn, w_out, gain_a, gain_h, ln_g, ln_b):
    n = x2d.shape[0]
    tm = MERGE_TM
    wide = lambda name: pl.BlockSpec((tm, D_MODEL), lambda i, name=name: (i, COL[name] // D_MODEL))
    row = lambda: pl.BlockSpec((tm, D_MODEL), lambda i: (i, 0))
    full = lambda shape: pl.BlockSpec(shape, lambda i: (0,) * len(shape),
                                      pipeline_mode=pl.Buffered(1))
    return pl.pallas_call(
        _merge_kernel,
        grid=(n // tm,),
        in_specs=[row(), row(), wide("a_gate"), wide("h_gate"), wide("m_gla"), wide("m_hgrn"), row(),
                  full((GLA_V, D_MODEL)), full((HG_V, D_MODEL)), full((D_MODEL, D_MODEL)),
                  full((1, GLA_DV)), full((1, HGRN_DV)), full((1, D_MODEL)), full((1, D_MODEL))],
        out_specs=row(),
        out_shape=jax.ShapeDtypeStruct((n, D_MODEL), F32),
        compiler_params=pltpu.CompilerParams(
            dimension_semantics=("arbitrary",), vmem_limit_bytes=VMEM_LIMIT),
        name="merge",
    )(o_gla, o_hgrn, p2, p2, p2, p2, x2d, w_gla, w_hgrn, w_out, gain_a, gain_h, ln_g, ln_b)


def _gate_params(up_f, up_b, bias_f, bias_b):
    rank = up_f.shape[0]
    up = jnp.zeros((LANES, 2 * GLA_QK), F32)
    up = up.at[:rank, :GLA_QK].set(up_f).at[rank:2 * rank, GLA_QK:].set(up_b)
    bias = jnp.concatenate([bias_f, bias_b])[None, :]
    return up.astype(BF16), bias


def kernel(x, w_in, gla_gk_up_f, gla_gk_bias_f, gla_gk_up_b, gla_gk_bias_b, gla_norm_g,
           hgrn_lb_logits_f, hgrn_lb_logits_b, hgrn_norm_g, w_branch_gla, w_branch_hgrn,
           w_out, ln_g, ln_b):
    assert w_in.shape[0] == DEPTH == 1
    bsz, seq, d = x.shape
    assert d == D_MODEL and seq % SCAN_TILE == 0 and SCAN_TILE % CUMSUM_ROWS == 0
    x2d = x.astype(F32).reshape(bsz * seq, d)

    w_t = jnp.swapaxes(w_in, 1, 2).reshape(w_in.shape[2], w_in.shape[1]).astype(F32)
    w_packed, w_lr = _pack_w(w_t)
    up, gbias = _gate_params(gla_gk_up_f[0].astype(F32), gla_gk_up_b[0].astype(F32),
                             gla_gk_bias_f[0].astype(F32), gla_gk_bias_b[0].astype(F32))
    p2 = _project(x2d, w_packed, w_lr, up, gbias,
                  hgrn_lb_logits_f.astype(F32), hgrn_lb_logits_b.astype(F32))
    p3 = p2.reshape(bsz, seq, PROJ_COLS)

    o_gla = _gla_scan(p3)
    o_hgrn = _hgrn_scan(p3)

    out = _merge(
        o_gla.reshape(bsz * seq, GLA_V), o_hgrn.reshape(bsz * seq, HG_V), p2, x2d,
        w_branch_gla[0].astype(BF16), w_branch_hgrn[0].astype(BF16), w_out[0].astype(BF16),
        gla_norm_g[0].astype(F32).reshape(1, GLA_DV), hgrn_norm_g[0].astype(F32).reshape(1, HGRN_DV),
        ln_g[0].astype(F32).reshape(1, D_MODEL), ln_b[0].astype(F32).reshape(1, D_MODEL))
    return out.reshape(bsz, seq, d).astype(x.dtype)
```

```python
import functools

import jax
import jax.numpy as jnp
import numpy as np
from jax import lax
from jax.experimental import pallas as pl
from jax.experimental.pallas import tpu as pltpu

F32 = jnp.float32
BF16 = jnp.bfloat16

D_MODEL = 1024
DEPTH = 1
GLA_HEADS = 4
GLA_DK = 128
GLA_DV = 256
GLA_GATE_RANK = 16
GLA_GATE_NORMALIZER = 16.0
HGRN_HEADS = 8
HGRN_DK = 128
HGRN_DV = 128
GLA_QK = GLA_HEADS * GLA_DK
GLA_V = GLA_HEADS * GLA_DV
HG_K = HGRN_HEADS * HGRN_DK
HG_V = HGRN_HEADS * HGRN_DV
RMS_EPS = 1e-6
LN_EPS = 1e-5
DEEPNORM_ALPHA = (2.0 * DEPTH) ** 0.25
LOG2E = 1.4426950408889634

LANES = 128
VREG_ELEMS = 8 * LANES
STATE_VREG_BUDGET = 32

_ORIG_SIZES = (GLA_QK, GLA_QK, GLA_V, GLA_V, GLA_GATE_RANK, GLA_GATE_RANK,
               HG_K, HG_K, HG_K, HG_V, HG_V, D_MODEL, D_MODEL)
_ORIG_NAMES = ("a_q", "a_k", "a_v", "a_gate", "lr_f", "lr_b",
               "h_q", "h_ff", "h_fb", "h_i", "h_gate", "m_gla", "m_hgrn")
_ORIG_START = dict(zip(_ORIG_NAMES, np.cumsum((0,) + _ORIG_SIZES[:-1]).tolist()))
_SIZE = dict(zip(_ORIG_NAMES, _ORIG_SIZES))
_LR_LO = _ORIG_START["lr_f"]
_LR_HI = _ORIG_START["lr_b"] + _SIZE["lr_b"]

_OUT_ORDER = ("a_q", "a_k", "a_v", "a_gate", "g_f", "g_b",
              "h_q", "h_ff", "h_fb", "h_i", "h_gate", "m_gla", "m_hgrn")
_OUT_SIZE = dict(_SIZE, g_f=GLA_QK, g_b=GLA_QK)
COL = dict(zip(_OUT_ORDER, np.cumsum([0] + [_OUT_SIZE[nm] for nm in _OUT_ORDER[:-1]]).tolist()))
PROJ_COLS = sum(_OUT_SIZE[nm] for nm in _OUT_ORDER)

CHUNK = 64
SCAN_TILE = 512
CUMSUM_ROWS = 256
PHASE1_UNROLL = 4
PHASE2_UNROLL = 16
PROJ_TM = 512
WPACK_TN = 1024
MERGE_TM = 1024
MERGE_SUB = 256
VMEM_LIMIT = 60 * 1024 * 1024


def _sigmoid(x):
    return 1.0 / (1.0 + jnp.exp2(x * (-LOG2E)))


def _lower_bound(logits, layer):
    m = jnp.max(logits, axis=0, keepdims=True)
    e = jnp.exp(logits - m)
    return jnp.sum(e[:layer + 1], axis=0, keepdims=True) / jnp.sum(e, axis=0, keepdims=True)


def _pack_w_kernel(a_ref, b_ref, o_ref, lr_ref):
    j = pl.program_id(0)
    gap = _LR_HI - _LR_LO
    first_shifted = _LR_LO // WPACK_TN

    @pl.when(j < first_shifted)
    def _():
        o_ref[...] = a_ref[...].T.astype(BF16)

    @pl.when(j == first_shifted - 1)
    def _():
        padded = jnp.concatenate([b_ref[...], jnp.zeros((LANES - gap, b_ref.shape[1]), F32)], axis=0)
        lr_ref[...] = padded.T.astype(BF16)

    @pl.when(j >= first_shifted)
    def _():
        rows = jnp.concatenate([a_ref[gap:, :], b_ref[...]], axis=0)
        o_ref[...] = rows.T.astype(BF16)


def _pack_w(w_t):
    d, gap = w_t.shape[1], _LR_HI - _LR_LO
    width = w_t.shape[0] - gap
    assert _LR_LO % WPACK_TN == 0 and _LR_LO > 0 and width % WPACK_TN == 0 and WPACK_TN % gap == 0
    return pl.pallas_call(
        _pack_w_kernel,
        grid=(width // WPACK_TN,),
        in_specs=[pl.BlockSpec((WPACK_TN, d), lambda j: (j, 0)),
                  pl.BlockSpec((gap, d), lambda j: ((j + 1) * (WPACK_TN // gap), 0))],
        out_specs=[pl.BlockSpec((d, WPACK_TN), lambda j: (0, j)),
                   pl.BlockSpec((d, LANES), lambda j: (0, 0))],
        out_shape=[jax.ShapeDtypeStruct((d, width), BF16), jax.ShapeDtypeStruct((d, LANES), BF16)],
        compiler_params=pltpu.CompilerParams(
            dimension_semantics=("arbitrary",), vmem_limit_bytes=VMEM_LIMIT),
        name="pack_w",
    )(w_t, w_t)


def _proj_kernel(x_ref, w_ref, wlr_ref, up_ref, gbias_ref, lf_ref, lb_ref, o_ref):
    xb = x_ref[...].astype(BF16)
    mm = lambda w: jnp.dot(xb, w, preferred_element_type=F32)
    silu = lambda a: a * _sigmoid(a)

    def put(name, val):
        o_ref[:, COL[name]:COL[name] + val.shape[1]] = val.astype(BF16)

    def head(name):
        lo = _ORIG_START[name]
        return mm(w_ref[:, lo:lo + _SIZE[name]])

    def tail(name):
        lo = _ORIG_START[name] - (_LR_HI - _LR_LO)
        return mm(w_ref[:, lo:lo + _SIZE[name]])

    def log2_forget(h, logits_ref):
        lb = _lower_bound(logits_ref[...], 0)
        return jnp.log2(lb + (1.0 - lb) * _sigmoid(h))

    put("a_q", head("a_q") * (GLA_DK ** -0.5))
    put("a_k", head("a_k"))
    put("a_v", head("a_v"))
    put("a_gate", silu(head("a_gate")))
    z = jnp.dot(mm(wlr_ref[...]).astype(BF16), up_ref[...], preferred_element_type=F32) + gbias_ref[...]
    log_sig = jnp.minimum(z, 0.0) - jnp.log2(1.0 + jnp.exp2(jnp.abs(z) * (-LOG2E))) * (1.0 / LOG2E)
    put("g_f", log_sig * (LOG2E / GLA_GATE_NORMALIZER))
    put("h_q", silu(tail("h_q")) * (HGRN_DK ** -0.5))
    put("h_ff", log2_forget(tail("h_ff"), lf_ref))
    put("h_fb", log2_forget(tail("h_fb"), lb_ref))
    put("h_i", tail("h_i"))
    put("h_gate", silu(tail("h_gate")))
    put("m_gla", _sigmoid(tail("m_gla")))
    put("m_hgrn", _sigmoid(tail("m_hgrn")))


def _project(x2d, w_packed, w_lr, up, gbias, logits_f, logits_b):
    n = x2d.shape[0]
    assert n % PROJ_TM == 0
    resident = lambda a: pl.BlockSpec(a.shape, lambda i: (0,) * a.ndim,
                                      pipeline_mode=pl.Buffered(1))
    consts = (w_packed, w_lr, up, gbias, logits_f, logits_b)
    return pl.pallas_call(
        _proj_kernel,
        grid=(n // PROJ_TM,),
        in_specs=[pl.BlockSpec((PROJ_TM, D_MODEL), lambda i: (i, 0))] + [resident(a) for a in consts],
        out_specs=pl.BlockSpec((PROJ_TM, PROJ_COLS), lambda i: (i, 0)),
        out_shape=jax.ShapeDtypeStruct((n, PROJ_COLS), BF16),
        compiler_params=pltpu.CompilerParams(
            dimension_semantics=("arbitrary",), vmem_limit_bytes=VMEM_LIMIT),
        name="proj",
    )(x2d, *consts)


def _dot_nt(a, b):
    return lax.dot_general(a, b, (((1,), (1,)), ((), ())), preferred_element_type=F32)


def _dot_tn(a, b):
    return lax.dot_general(a, b, (((0,), (0,)), ((), ())), preferred_element_type=F32)


def _block_diag_cumsum_mat(rows, c):
    row = lax.broadcasted_iota(jnp.int32, (rows, rows), 0)
    col = lax.broadcasted_iota(jnp.int32, (rows, rows), 1)
    return jnp.where(((row // c) == (col // c)) & (col <= row), 1.0, 0.0).astype(BF16)


def _scan_core(q_ref, k_of, g_refs, v_ref, o_ref, scratch, seq, dk, dv):
    qi_refs, upd_refs, dec_refs, oacc_ref, cum_ref = (scratch[0:2], scratch[2:4], scratch[4:6],
                                                       scratch[6], scratch[7])
    c = CHUNK
    pair = 2 * c
    tile = SCAN_TILE
    pairs_per_tile = tile // pair
    n_tiles = seq // tile
    n = seq // pair
    mid = c // 2
    row = lax.broadcasted_iota(jnp.int32, (pair, pair), 0)
    col = lax.broadcasted_iota(jnp.int32, (pair, pair), 1)
    masks = (col <= row, col >= row)
    cum_mat = _block_diag_cumsum_mat(CUMSUM_ROWS, c)

    def tile_rows(t):
        return pl.ds(pl.multiple_of(t * tile, tile), tile)

    def stage(t):
        g_pair = jnp.concatenate([g_refs[0][0, tile_rows(t), :], g_refs[1][0, tile_rows(t), :]], axis=1)
        for r in range(0, tile, CUMSUM_ROWS):
            cum_ref[r:r + CUMSUM_ROWS, :] = jnp.dot(cum_mat, g_pair[r:r + CUMSUM_ROWS],
                                                    preferred_element_type=F32)

    stage(0)

    def chunk_terms(d, q, k, g, cum):
        if d == 0:
            b_mid = cum[mid:mid + 1, :]
            total = cum[c - 1:c, :]
            rel, lead, trail = cum - b_mid, b_mid, total - b_mid
        else:
            e = cum - g
            e_mid = e[mid:mid + 1, :]
            total = cum[c - 1:c, :]
            rel, lead, trail = e_mid - e, total - e_mid, e_mid
        return q * jnp.exp2(rel), k * jnp.exp2(-rel), lead, trail, total

    def phase1(i, carry):
        rows = tile_rows(i)
        q = q_ref[0, rows, :].astype(F32)
        gs = (g_refs[0][0, rows, :].astype(F32), g_refs[1][0, rows, :].astype(F32))
        ks = (k_of(0, gs[0], rows), k_of(1, gs[1], rows))
        cum = cum_ref[...]
        v = v_ref[0, rows, :]
        bf = lambda a: a.astype(BF16)

        def scores_of(p):
            pair_rows = pl.ds(pl.multiple_of(i * tile + p * pair, pair), pair)
            sa, sb = slice(p * pair, p * pair + c), slice(p * pair + c, (p + 1) * pair)
            acc, kp_pair = None, []
            for d in range(2):
                lanes = slice(d * dk, (d + 1) * dk)
                qa, ka, lead_a, trail_a, tot_a = chunk_terms(d, q[sa], ks[d][sa], gs[d][sa], cum[sa, lanes])
                qb, kb, lead_b, trail_b, tot_b = chunk_terms(d, q[sb], ks[d][sb], gs[d][sb], cum[sb, lanes])
                if d == 0:
                    keys_a = jnp.concatenate([bf(ka), bf(kb)], axis=0)
                    keys_b = jnp.concatenate([bf(ka * jnp.exp2(trail_a + lead_b)), bf(kb)], axis=0)
                    qi = jnp.concatenate([qa * jnp.exp2(lead_a), qb * jnp.exp2(lead_b + tot_a)], axis=0)
                    kp = jnp.concatenate([ka * jnp.exp2(trail_a + tot_b), kb * jnp.exp2(trail_b)], axis=0)
                else:
                    keys_a = jnp.concatenate([bf(ka), bf(kb * jnp.exp2(trail_b + lead_a))], axis=0)
                    keys_b = jnp.concatenate([bf(ka), bf(kb)], axis=0)
                    qi = jnp.concatenate([qa * jnp.exp2(lead_a + tot_b), qb * jnp.exp2(lead_b)], axis=0)
                    kp = jnp.concatenate([ka * jnp.exp2(trail_a), kb * jnp.exp2(trail_b + tot_a)], axis=0)
                scores = jnp.concatenate([_dot_nt(bf(qa), keys_a), _dot_nt(bf(qb), keys_b)], axis=0)
                scores = jnp.where(masks[d], scores, 0.0)
                acc = scores if acc is None else acc + scores
                qi_refs[d][pair_rows, :] = bf(qi)
                kp_pair.append(bf(kp))
                dec_refs[d][i * pairs_per_tile + p] = jnp.broadcast_to(
                    jnp.exp2(tot_a + tot_b), (LANES, dk)).T
            return bf(acc), jnp.concatenate(kp_pair, axis=1)

        def increment(p, kp):
            upd = _dot_tn(kp, v[p * pair:(p + 1) * pair])
            upd_refs[0][i * pairs_per_tile + p] = upd[:dk]
            upd_refs[1][i * pairs_per_tile + p] = upd[dk:]

        def output(p, scores):
            pair_rows = pl.ds(pl.multiple_of(i * tile + p * pair, pair), pair)
            oacc_ref[pair_rows, :] = jnp.dot(scores, v[p * pair:(p + 1) * pair],
                                             preferred_element_type=F32)

        done = {}
        for s in range(pairs_per_tile + 2):
            if s < pairs_per_tile:
                done[s] = scores_of(s)
            if 1 <= s <= pairs_per_tile:
                increment(s - 1, done[s - 1][1])
            if s == pairs_per_tile - 1:
                stage(jnp.minimum(i + 1, n_tiles - 1))
            if s >= 2:
                output(s - 2, done[s - 2][0])
        return carry

    lax.fori_loop(0, n_tiles, phase1, 0, unroll=PHASE1_UNROLL)

    def advance(d, ci, st, final):
        rows = pl.ds(pl.multiple_of(ci * pair, pair), pair)
        o = oacc_ref[rows, :] + jnp.dot(qi_refs[d][rows, :], st.astype(BF16),
                                        preferred_element_type=F32)
        if final:
            o_ref[0, rows, :] = o.astype(o_ref.dtype)
        else:
            oacc_ref[rows, :] = o
        decay = jnp.concatenate([dec_refs[d][ci]] * (dv // LANES), axis=1)
        return st * decay + upd_refs[d][ci]

    zero = jnp.zeros((dk, dv), F32)
    if 2 * dk * dv <= STATE_VREG_BUDGET * VREG_ELEMS:
        both = lambda final: (lambda i, s: (advance(0, i, s[0], final),
                                            advance(1, n - 1 - i, s[1], final)))
        states = lax.fori_loop(0, n // 2, both(False), (zero, zero), unroll=PHASE2_UNROLL)
        lax.fori_loop(n // 2, n, both(True), states, unroll=PHASE2_UNROLL)
    else:
        lax.fori_loop(0, n, lambda i, s: advance(0, i, s, False), zero, unroll=PHASE2_UNROLL)
        lax.fori_loop(0, n, lambda i, s: advance(1, n - 1 - i, s, True), zero, unroll=PHASE2_UNROLL)


def _scan_scratch(seq, dk, dv):
    per_dir = lambda shape, dt: [pltpu.VMEM(shape, dt), pltpu.VMEM(shape, dt)]
    n = seq // (2 * CHUNK)
    return (per_dir((seq, dk), BF16)
            + per_dir((n, dk, dv), F32)
            + per_dir((n, dk, LANES), F32)
            + [pltpu.VMEM((seq, dv), F32)]
            + [pltpu.VMEM((SCAN_TILE, 2 * dk), F32)])


def _scan_call(kernel_fn, name, p3, specs, heads, dk, dv):
    bsz, seq, _ = p3.shape
    return pl.pallas_call(
        functools.partial(kernel_fn, seq=seq),
        grid=(bsz, heads),
        in_specs=specs,
        out_specs=pl.BlockSpec((1, seq, dv), lambda b, h: (b, 0, h)),
        out_shape=jax.ShapeDtypeStruct((bsz, seq, heads * dv), BF16),
        scratch_shapes=_scan_scratch(seq, dk, dv),
        compiler_params=pltpu.CompilerParams(
            dimension_semantics=("arbitrary", "arbitrary"), vmem_limit_bytes=VMEM_LIMIT),
        name=name,
    )(*([p3] * len(specs)))


def _head_block(seq, name, width):
    first = COL[name] // width
    return pl.BlockSpec((1, seq, width), lambda b, h: (b, 0, first + h))


def _gla_kernel(q_ref, k_ref, gf_ref, gb_ref, v_ref, o_ref, *scratch, seq):
    k_of = lambda d, g, rows: k_ref[0, rows, :].astype(F32)
    _scan_core(q_ref, k_of, (gf_ref, gb_ref), v_ref, o_ref, scratch, seq, GLA_DK, GLA_DV)


def _gla_scan(p3):
    seq = p3.shape[1]
    specs = [_head_block(seq, "a_q", GLA_DK), _head_block(seq, "a_k", GLA_DK),
             _head_block(seq, "g_f", GLA_DK), _head_block(seq, "g_b", GLA_DK),
             _head_block(seq, "a_v", GLA_DV)]
    return _scan_call(_gla_kernel, "gla_scan", p3, specs, GLA_HEADS, GLA_DK, GLA_DV)


def _hgrn_kernel(q_ref, gf_ref, gb_ref, v_ref, o_ref, *scratch, seq):
    k_of = lambda d, g, rows: 1.0 - jnp.exp2(g)
    _scan_core(q_ref, k_of, (gf_ref, gb_ref), v_ref, o_ref, scratch, seq, HGRN_DK, HGRN_DV)


def _hgrn_scan(p3):
    seq = p3.shape[1]
    specs = [_head_block(seq, "h_q", HGRN_DK), _head_block(seq, "h_ff", HGRN_DK),
             _head_block(seq, "h_fb", HGRN_DK), _head_block(seq, "h_i", HGRN_DV)]
    return _scan_call(_hgrn_kernel, "hgrn_scan", p3, specs, HGRN_HEADS, HGRN_DK, HGRN_DV)


def _head_rmsnorm_gate(o, gain, gate, heads, dv):
    parts = []
    for h in range(heads):
        seg = o[:, h * dv:(h + 1) * dv]
        ms = jnp.mean(seg * seg, axis=-1, keepdims=True)
        parts.append(seg * lax.rsqrt(ms + RMS_EPS) * gain)
    return (jnp.concatenate(parts, axis=1) * gate).astype(BF16)


def _merge_kernel(oa_ref, oh_ref, ag_ref, hg_ref, mg_ref, mh_ref, x_ref, wa_ref, wh_ref, wo_ref,
                  ga_ref, gh_ref, lng_ref, lnb_ref, out_ref):
    subs = [slice(r, r + MERGE_SUB) for r in range(0, out_ref.shape[0], MERGE_SUB)]
    mixed = []
    for sl in subs:
        na = _head_rmsnorm_gate(oa_ref[sl, :].astype(F32), ga_ref[...], ag_ref[sl, :].astype(F32),
                                GLA_HEADS, GLA_DV)
        nh = _head_rmsnorm_gate(oh_ref[sl, :].astype(F32), gh_ref[...], hg_ref[sl, :].astype(F32),
                                HGRN_HEADS, HGRN_DV)
        y_gla = jnp.dot(na, wa_ref[...], preferred_element_type=F32)
        y_hgrn = jnp.dot(nh, wh_ref[...], preferred_element_type=F32)
        mixed.append((mg_ref[sl, :].astype(F32) * y_gla
                      + mh_ref[sl, :].astype(F32) * y_hgrn).astype(BF16))
    for sl, y in zip(subs, mixed):
        r = DEEPNORM_ALPHA * x_ref[sl, :] + jnp.dot(y, wo_ref[...], preferred_element_type=F32)
        mu = jnp.mean(r, axis=-1, keepdims=True)
        rc = r - mu
        var = jnp.mean(rc * rc, axis=-1, keepdims=True)
        out_ref[sl, :] = rc * lax.rsqrt(var + LN_EPS) * lng_ref[...] + lnb_ref[...]


def _merge(o_gla, o_hgrn, p2, x2d, w_gla, w_hgrn, w_out, gain_a, gain_h, ln_g, ln_b):
    n = x2d.shape[0]
    tm = MERGE_TM
    wide = lambda name: pl.BlockSpec((tm, D_MODEL), lambda i, name=name: (i, COL[name] // D_MODEL))
    row = lambda: pl.BlockSpec((tm, D_MODEL), lambda i: (i, 0))
    full = lambda shape: pl.BlockSpec(shape, lambda i: (0,) * len(shape),
                                      pipeline_mode=pl.Buffered(1))
    return pl.pallas_call(
        _merge_kernel,
        grid=(n // tm,),
        in_specs=[row(), row(), wide("a_gate"), wide("h_gate"), wide("m_gla"), wide("m_hgrn"), row(),
                  full((GLA_V, D_MODEL)), full((HG_V, D_MODEL)), full((D_MODEL, D_MODEL)),
                  full((1, GLA_DV)), full((1, HGRN_DV)), full((1, D_MODEL)), full((1, D_MODEL))],
        out_specs=row(),
        out_shape=jax.ShapeDtypeStruct((n, D_MODEL), F32),
        compiler_params=pltpu.CompilerParams(
            dimension_semantics=("arbitrary",), vmem_limit_bytes=VMEM_LIMIT),
        name="merge",
    )(o_gla, o_hgrn, p2, p2, p2, p2, x2d, w_gla, w_hgrn, w_out, gain_a, gain_h, ln_g, ln_b)


def _gate_params(up_f, up_b, bias_f, bias_b):
    rank = up_f.shape[0]
    up = jnp.zeros((LANES, 2 * GLA_QK), F32)
    up = up.at[:rank, :GLA_QK].set(up_f).at[rank:2 * rank, GLA_QK:].set(up_b)
    bias = jnp.concatenate([bias_f, bias_b])[None, :]
    return up.astype(BF16), bias


def kernel(x, w_in, gla_gk_up_f, gla_gk_bias_f, gla_gk_up_b, gla_gk_bias_b, gla_norm_g,
           hgrn_lb_logits_f, hgrn_lb_logits_b, hgrn_norm_g, w_branch_gla, w_branch_hgrn,
           w_out, ln_g, ln_b):
    assert w_in.shape[0] == DEPTH == 1
    bsz, seq, d = x.shape
    assert d == D_MODEL and seq % SCAN_TILE == 0 and SCAN_TILE % CUMSUM_ROWS == 0
    x2d = x.astype(F32).reshape(bsz * seq, d)

    w_t = jnp.swapaxes(w_in, 1, 2).reshape(w_in.shape[2], w_in.shape[1]).astype(F32)
    w_packed, w_lr = _pack_w(w_t)
    up, gbias = _gate_params(gla_gk_up_f[0].astype(F32), gla_gk_up_b[0].astype(F32),
                             gla_gk_bias_f[0].astype(F32), gla_gk_bias_b[0].astype(F32))
    p2 = _project(x2d, w_packed, w_lr, up, gbias,
                  hgrn_lb_logits_f.astype(F32), hgrn_lb_logits_b.astype(F32))
    p3 = p2.reshape(bsz, seq, PROJ_COLS)

    o_gla = _gla_scan(p3)
    o_hgrn = _hgrn_scan(p3)

    out = _merge(
        o_gla.reshape(bsz * seq, GLA_V), o_hgrn.reshape(bsz * seq, HG_V), p2, x2d,
        w_branch_gla[0].astype(BF16), w_branch_hgrn[0].astype(BF16), w_out[0].astype(BF16),
        gla_norm_g[0].astype(F32).reshape(1, GLA_DV), hgrn_norm_g[0].astype(F32).reshape(1, HGRN_DV),
        ln_g[0].astype(F32).reshape(1, D_MODEL), ln_b[0].astype(F32).reshape(1, D_MODEL))
    return out.reshape(bsz, seq, d).astype(x.dtype)
```

```python
import functools

import jax
import jax.numpy as jnp
import numpy as np
from jax import lax
from jax.experimental import pallas as pl
from jax.experimental.pallas import tpu as pltpu

F32 = jnp.float32
BF16 = jnp.bfloat16

D_MODEL = 1024
DEPTH = 1
GLA_HEADS = 4
GLA_DK = 128
GLA_DV = 256
GLA_GATE_RANK = 16
GLA_GATE_NORMALIZER = 16.0
HGRN_HEADS = 8
HGRN_DK = 128
HGRN_DV = 128
GLA_QK = GLA_HEADS * GLA_DK
GLA_V = GLA_HEADS * GLA_DV
HG_K = HGRN_HEADS * HGRN_DK
HG_V = HGRN_HEADS * HGRN_DV
RMS_EPS = 1e-6
LN_EPS = 1e-5
DEEPNORM_ALPHA = (2.0 * DEPTH) ** 0.25
LOG2E = 1.4426950408889634

LANES = 128

_ORIG_SIZES = (GLA_QK, GLA_QK, GLA_V, GLA_V, GLA_GATE_RANK, GLA_GATE_RANK,
               HG_K, HG_K, HG_K, HG_V, HG_V, D_MODEL, D_MODEL)
_ORIG_NAMES = ("a_q", "a_k", "a_v", "a_gate", "lr_f", "lr_b",
               "h_q", "h_ff", "h_fb", "h_i", "h_gate", "m_gla", "m_hgrn")
_ORIG_START = dict(zip(_ORIG_NAMES, np.cumsum((0,) + _ORIG_SIZES[:-1]).tolist()))
_SIZE = dict(zip(_ORIG_NAMES, _ORIG_SIZES))
_LR_LO = _ORIG_START["lr_f"]
_LR_HI = _ORIG_START["lr_b"] + _SIZE["lr_b"]

_OUT_ORDER = ("a_q", "a_k", "a_v", "a_gate", "g_f", "g_b",
              "h_q", "h_ff", "h_fb", "h_i", "h_gate", "m_gla", "m_hgrn")
_OUT_SIZE = dict(_SIZE, g_f=GLA_QK, g_b=GLA_QK)
COL = dict(zip(_OUT_ORDER, np.cumsum([0] + [_OUT_SIZE[nm] for nm in _OUT_ORDER[:-1]]).tolist()))
PROJ_COLS = sum(_OUT_SIZE[nm] for nm in _OUT_ORDER)

CHUNK = 64
SCAN_TILE = 512
CUMSUM_ROWS = 256
PHASE1_UNROLL = 4
PHASE2_UNROLL = 16
PROJ_TM = 512
PROJ_SUBN = 512
WPACK_TN = 1024
MERGE_TM = 1024
MERGE_SUB = 256
VMEM_LIMIT = 60 * 1024 * 1024


def _sigmoid(x):
    return 1.0 / (1.0 + jnp.exp2(x * (-LOG2E)))


def _lower_bound(logits, layer):
    m = jnp.max(logits, axis=0, keepdims=True)
    e = jnp.exp(logits - m)
    return jnp.sum(e[:layer + 1], axis=0, keepdims=True) / jnp.sum(e, axis=0, keepdims=True)


def _pack_w_kernel(a_ref, b_ref, o_ref, lr_ref):
    j = pl.program_id(0)
    gap = _LR_HI - _LR_LO
    first_shifted = _LR_LO // WPACK_TN

    @pl.when(j < first_shifted)
    def _():
        o_ref[...] = a_ref[...].T.astype(BF16)

    @pl.when(j == first_shifted - 1)
    def _():
        padded = jnp.concatenate([b_ref[...], jnp.zeros((LANES - gap, b_ref.shape[1]), F32)], axis=0)
        lr_ref[...] = padded.T.astype(BF16)

    @pl.when(j >= first_shifted)
    def _():
        rows = jnp.concatenate([a_ref[gap:, :], b_ref[...]], axis=0)
        o_ref[...] = rows.T.astype(BF16)


def _pack_w(w_t):
    d, gap = w_t.shape[1], _LR_HI - _LR_LO
    width = w_t.shape[0] - gap
    assert _LR_LO % WPACK_TN == 0 and _LR_LO > 0 and width % WPACK_TN == 0 and WPACK_TN % gap == 0
    return pl.pallas_call(
        _pack_w_kernel,
        grid=(width // WPACK_TN,),
        in_specs=[pl.BlockSpec((WPACK_TN, d), lambda j: (j, 0)),
                  pl.BlockSpec((gap, d), lambda j: ((j + 1) * (WPACK_TN // gap), 0))],
        out_specs=[pl.BlockSpec((d, WPACK_TN), lambda j: (0, j)),
                   pl.BlockSpec((d, LANES), lambda j: (0, 0))],
        out_shape=[jax.ShapeDtypeStruct((d, width), BF16), jax.ShapeDtypeStruct((d, LANES), BF16)],
        compiler_params=pltpu.CompilerParams(
            dimension_semantics=("arbitrary",), vmem_limit_bytes=VMEM_LIMIT),
        name="pack_w",
    )(w_t, w_t)


def _proj_kernel(x_ref, w_ref, wlr_ref, up_ref, gbias_ref, lf_ref, lb_ref, o_ref):
    xb = x_ref[...].astype(BF16)
    mm = lambda w: jnp.dot(xb, w, preferred_element_type=F32)
    silu = lambda a, cols: a * _sigmoid(a)
    ident = lambda a, cols: a
    squash = lambda a, cols: _sigmoid(a)
    scaled = lambda fn, c: (lambda a, cols: fn(a, cols) * c)

    def log2_forget(logits_ref):
        lb = _lower_bound(logits_ref[...], 0)
        return lambda h, cols: jnp.log2(lb[:, cols] + (1.0 - lb[:, cols]) * _sigmoid(h))

    def w_columns(name):
        lo = _ORIG_START[name] - (_LR_HI - _LR_LO if _ORIG_START[name] >= _LR_HI else 0)
        return lambda cols: mm(w_ref[:, lo + cols.start:lo + cols.stop])

    def emit(name, fn, product=None):
        product = product or w_columns(name)
        for off in range(0, _OUT_SIZE[name], PROJ_SUBN):
            cols = slice(off, off + PROJ_SUBN)
            dst = slice(COL[name] + off, COL[name] + off + PROJ_SUBN)
            o_ref[:, dst] = fn(product(cols), cols).astype(BF16)

    def log_sigmoid_gate(z, cols):
        log_sig = jnp.minimum(z, 0.0) - jnp.log2(1.0 + jnp.exp2(jnp.abs(z) * (-LOG2E))) * (1.0 / LOG2E)
        return log_sig * (LOG2E / GLA_GATE_NORMALIZER)

    emit("a_q", scaled(ident, GLA_DK ** -0.5))
    emit("a_k", ident)
    emit("a_v", ident)
    emit("a_gate", silu)
    lr = mm(wlr_ref[...]).astype(BF16)
    for d, name in enumerate(("g_f", "g_b")):
        base = d * GLA_QK
        emit(name, log_sigmoid_gate, product=lambda cols, base=base: (
            jnp.dot(lr, up_ref[:, base + cols.start:base + cols.stop], preferred_element_type=F32)
            + gbias_ref[:, base + cols.start:base + cols.stop]))
    emit("h_q", scaled(silu, HGRN_DK ** -0.5))
    emit("h_ff", log2_forget(lf_ref))
    emit("h_fb", log2_forget(lb_ref))
    emit("h_i", ident)
    emit("h_gate", silu)
    emit("m_gla", squash)
    emit("m_hgrn", squash)


def _project(x2d, w_packed, w_lr, up, gbias, logits_f, logits_b):
    n = x2d.shape[0]
    assert n % PROJ_TM == 0
    resident = lambda a: pl.BlockSpec(a.shape, lambda i: (0,) * a.ndim,
                                      pipeline_mode=pl.Buffered(1))
    consts = (w_packed, w_lr, up, gbias, logits_f, logits_b)
    return pl.pallas_call(
        _proj_kernel,
        grid=(n // PROJ_TM,),
        in_specs=[pl.BlockSpec((PROJ_TM, D_MODEL), lambda i: (i, 0))] + [resident(a) for a in consts],
        out_specs=pl.BlockSpec((PROJ_TM, PROJ_COLS), lambda i: (i, 0)),
        out_shape=jax.ShapeDtypeStruct((n, PROJ_COLS), BF16),
        compiler_params=pltpu.CompilerParams(
            dimension_semantics=("arbitrary",), vmem_limit_bytes=VMEM_LIMIT),
        name="proj",
    )(x2d, *consts)


def _dot_nt(a, b):
    return lax.dot_general(a, b, (((1,), (1,)), ((), ())), preferred_element_type=F32)


def _dot_tn(a, b):
    return lax.dot_general(a, b, (((0,), (0,)), ((), ())), preferred_element_type=F32)


def _block_diag_cumsum_mat(rows, c):
    row = lax.broadcasted_iota(jnp.int32, (rows, rows), 0)
    col = lax.broadcasted_iota(jnp.int32, (rows, rows), 1)
    return jnp.where(((row // c) == (col // c)) & (col <= row), 1.0, 0.0).astype(BF16)


def _scan_core(q_ref, k_of, g_refs, v_ref, o_ref, scratch, seq, dk, dv):
    qi_ref, upd_ref, dec_ref, oacc_ref, cum_ref, st_ref = scratch
    c = CHUNK
    pair = 2 * c
    tile = SCAN_TILE
    pairs_per_tile = tile // pair
    n_tiles = seq // tile
    n = seq // pair
    mid = c // 2
    row = lax.broadcasted_iota(jnp.int32, (pair, pair), 0)
    col = lax.broadcasted_iota(jnp.int32, (pair, pair), 1)
    masks = (col <= row, col >= row)
    cum_mat = _block_diag_cumsum_mat(CUMSUM_ROWS, c)

    def tile_rows(t):
        return pl.ds(pl.multiple_of(t * tile, tile), tile)

    def stage(t):
        g_pair = jnp.concatenate([g_refs[0][0, tile_rows(t), :], g_refs[1][0, tile_rows(t), :]], axis=1)
        for r in range(0, tile, CUMSUM_ROWS):
            cum_ref[r:r + CUMSUM_ROWS, :] = jnp.dot(cum_mat, g_pair[r:r + CUMSUM_ROWS],
                                                    preferred_element_type=F32)

    stage(0)

    def chunk_terms(d, q, k, g, cum):
        if d == 0:
            b_mid = cum[mid:mid + 1, :]
            total = cum[c - 1:c, :]
            rel, lead, trail = cum - b_mid, b_mid, total - b_mid
        else:
            e = cum - g
            e_mid = e[mid:mid + 1, :]
            total = cum[c - 1:c, :]
            rel, lead, trail = e_mid - e, total - e_mid, e_mid
        return q * jnp.exp2(rel), k * jnp.exp2(-rel), lead, trail, total

    def phase1(i, carry):
        rows = tile_rows(i)
        q = q_ref[0, rows, :].astype(F32)
        gs = (g_refs[0][0, rows, :].astype(F32), g_refs[1][0, rows, :].astype(F32))
        ks = (k_of(0, gs[0], rows), k_of(1, gs[1], rows))
        cum = cum_ref[...]
        v = v_ref[0, rows, :]
        bf = lambda a: a.astype(BF16)

        def scores_of(p):
            pair_rows = pl.ds(pl.multiple_of(i * tile + p * pair, pair), pair)
            sa, sb = slice(p * pair, p * pair + c), slice(p * pair + c, (p + 1) * pair)
            acc, kp_pair = None, []
            for d in range(2):
                lanes = slice(d * dk, (d + 1) * dk)
                qa, ka, lead_a, trail_a, tot_a = chunk_terms(d, q[sa], ks[d][sa], gs[d][sa], cum[sa, lanes])
                qb, kb, lead_b, trail_b, tot_b = chunk_terms(d, q[sb], ks[d][sb], gs[d][sb], cum[sb, lanes])
                if d == 0:
                    keys_a = jnp.concatenate([bf(ka), bf(kb)], axis=0)
                    keys_b = jnp.concatenate([bf(ka * jnp.exp2(trail_a + lead_b)), bf(kb)], axis=0)
                    qi = jnp.concatenate([qa * jnp.exp2(lead_a), qb * jnp.exp2(lead_b + tot_a)], axis=0)
                    kp = jnp.concatenate([ka * jnp.exp2(trail_a + tot_b), kb * jnp.exp2(trail_b)], axis=0)
                else:
                    keys_a = jnp.concatenate([bf(ka), bf(kb * jnp.exp2(trail_b + lead_a))], axis=0)
                    keys_b = jnp.concatenate([bf(ka), bf(kb)], axis=0)
                    qi = jnp.concatenate([qa * jnp.exp2(lead_a + tot_b), qb * jnp.exp2(lead_b)], axis=0)
                    kp = jnp.concatenate([ka * jnp.exp2(trail_a), kb * jnp.exp2(trail_b + tot_a)], axis=0)
                scores = jnp.concatenate([_dot_nt(bf(qa), keys_a), _dot_nt(bf(qb), keys_b)], axis=0)
                scores = jnp.where(masks[d], scores, 0.0)
                acc = scores if acc is None else acc + scores
                lane_decay = jnp.broadcast_to(jnp.exp2(tot_a + tot_b), (LANES, dk)).T
                if d == 0:
                    qi_f, decay_f = bf(qi), lane_decay
                else:
                    qi_ref[pair_rows, :] = bf(qi)
                    dec_ref[i * pairs_per_tile + p] = lane_decay
                kp_pair.append(bf(kp))
            return bf(acc), jnp.concatenate(kp_pair, axis=1), qi_f, decay_f

        def increment(p, kp):
            upd = _dot_tn(kp, v[p * pair:(p + 1) * pair])
            upd_ref[i * pairs_per_tile + p] = upd[dk:]
            return upd[:dk]

        def output(p, scores, qi_f, decay_f, upd_f):
            pair_rows = pl.ds(pl.multiple_of(i * tile + p * pair, pair), pair)
            st = st_ref[...]
            oacc_ref[pair_rows, :] = jnp.dot(
                jnp.concatenate([scores, qi_f], axis=1),
                jnp.concatenate([v[p * pair:(p + 1) * pair], bf(st)], axis=0),
                preferred_element_type=F32)
            st_ref[...] = st * jnp.concatenate([decay_f] * (dv // LANES), axis=1) + upd_f

        done, upd_f = {}, {}
        for s in range(pairs_per_tile + 2):
            if s < pairs_per_tile:
                done[s] = scores_of(s)
            if 1 <= s <= pairs_per_tile:
                upd_f[s - 1] = increment(s - 1, done[s - 1][1])
            if s == pairs_per_tile - 1:
                stage(jnp.minimum(i + 1, n_tiles - 1))
            if s >= 2:
                scores, _, qi_f, decay_f = done[s - 2]
                output(s - 2, scores, qi_f, decay_f, upd_f[s - 2])
        return carry

    st_ref[...] = jnp.zeros_like(st_ref)
    lax.fori_loop(0, n_tiles, phase1, 0, unroll=PHASE1_UNROLL)

    def backward(j, st):
        ci = n - 1 - j
        rows = pl.ds(pl.multiple_of(ci * pair, pair), pair)
        o = oacc_ref[rows, :] + jnp.dot(qi_ref[rows, :], st.astype(BF16), preferred_element_type=F32)
        o_ref[0, rows, :] = o.astype(o_ref.dtype)
        return st * jnp.concatenate([dec_ref[ci]] * (dv // LANES), axis=1) + upd_ref[ci]

    lax.fori_loop(0, n, backward, jnp.zeros((dk, dv), F32), unroll=PHASE2_UNROLL)


def _scan_scratch(seq, dk, dv):
    n = seq // (2 * CHUNK)
    return [pltpu.VMEM((seq, dk), BF16),
            pltpu.VMEM((n, dk, dv), F32),
            pltpu.VMEM((n, dk, LANES), F32),
            pltpu.VMEM((seq, dv), F32),
            pltpu.VMEM((SCAN_TILE, 2 * dk), F32),
            pltpu.VMEM((dk, dv), F32)]


def _scan_call(kernel_fn, name, p3, specs, heads, dk, dv):
    bsz, seq, _ = p3.shape
    return pl.pallas_call(
        functools.partial(kernel_fn, seq=seq),
        grid=(bsz, heads),
        in_specs=specs,
        out_specs=pl.BlockSpec((1, seq, dv), lambda b, h: (b, 0, h)),
        out_shape=jax.ShapeDtypeStruct((bsz, seq, heads * dv), BF16),
        scratch_shapes=_scan_scratch(seq, dk, dv),
        compiler_params=pltpu.CompilerParams(
            dimension_semantics=("arbitrary", "arbitrary"), vmem_limit_bytes=VMEM_LIMIT),
        name=name,
    )(*([p3] * len(specs)))


def _head_block(seq, name, width):
    first = COL[name] // width
    return pl.BlockSpec((1, seq, width), lambda b, h: (b, 0, first + h))


def _gla_kernel(q_ref, k_ref, gf_ref, gb_ref, v_ref, o_ref, *scratch, seq):
    k_of = lambda d, g, rows: k_ref[0, rows, :].astype(F32)
    _scan_core(q_ref, k_of, (gf_ref, gb_ref), v_ref, o_ref, scratch, seq, GLA_DK, GLA_DV)


def _gla_scan(p3):
    seq = p3.shape[1]
    specs = [_head_block(seq, "a_q", GLA_DK), _head_block(seq, "a_k", GLA_DK),
             _head_block(seq, "g_f", GLA_DK), _head_block(seq, "g_b", GLA_DK),
             _head_block(seq, "a_v", GLA_DV)]
    return _scan_call(_gla_kernel, "gla_scan", p3, specs, GLA_HEADS, GLA_DK, GLA_DV)


def _hgrn_kernel(q_ref, gf_ref, gb_ref, v_ref, o_ref, *scratch, seq):
    k_of = lambda d, g, rows: 1.0 - jnp.exp2(g)
    _scan_core(q_ref, k_of, (gf_ref, gb_ref), v_ref, o_ref, scratch, seq, HGRN_DK, HGRN_DV)


def _hgrn_scan(p3):
    seq = p3.shape[1]
    specs = [_head_block(seq, "h_q", HGRN_DK), _head_block(seq, "h_ff", HGRN_DK),
             _head_block(seq, "h_fb", HGRN_DK), _head_block(seq, "h_i", HGRN_DV)]
    return _scan_call(_hgrn_kernel, "hgrn_scan", p3, specs, HGRN_HEADS, HGRN_DK, HGRN_DV)


def _head_rmsnorm_gate(o, gain, gate, heads, dv):
    parts = []
    for h in range(heads):
        seg = o[:, h * dv:(h + 1) * dv]
        ms = jnp.mean(seg * seg, axis=-1, keepdims=True)
        parts.append(seg * lax.rsqrt(ms + RMS_EPS) * gain)
    return (jnp.concatenate(parts, axis=1) * gate).astype(BF16)


def _merge_kernel(oa_ref, oh_ref, ag_ref, hg_ref, mg_ref, mh_ref, x_ref, wa_ref, wh_ref, wo_ref,
                  ga_ref, gh_ref, lng_ref, lnb_ref, out_ref):
    subs = [slice(r, r + MERGE_SUB) for r in range(0, out_ref.shape[0], MERGE_SUB)]
    mixed = []
    for sl in subs:
        na = _head_rmsnorm_gate(oa_ref[sl, :].astype(F32), ga_ref[...], ag_ref[sl, :].astype(F32),
                                GLA_HEADS, GLA_DV)
        nh = _head_rmsnorm_gate(oh_ref[sl, :].astype(F32), gh_ref[...], hg_ref[sl, :].astype(F32),
                                HGRN_HEADS, HGRN_DV)
        y_gla = jnp.dot(na, wa_ref[...], preferred_element_type=F32)
        y_hgrn = jnp.dot(nh, wh_ref[...], preferred_element_type=F32)
        mixed.append((mg_ref[sl, :].astype(F32) * y_gla
                      + mh_ref[sl, :].astype(F32) * y_hgrn).astype(BF16))
    for sl, y in zip(subs, mixed):
        r = DEEPNORM_ALPHA * x_ref[sl, :] + jnp.dot(y, wo_ref[...], preferred_element_type=F32)
        mu = jnp.mean(r, axis=-1, keepdims=True)
        rc = r - mu
        var = jnp.mean(rc * rc, axis=-1, keepdims=True)
        out_ref[sl, :] = rc * lax.rsqrt(var + LN_EPS) * lng_ref[...] + lnb_ref[...]


def _merge(o_gla, o_hgrn, p2, x2d, w_gla, w_hgrn, w_out, gain_a, gain_h, ln_g, ln_b):
    n = x2d.shape[0]
    tm = MERGE_TM
    wide = lambda name: pl.BlockSpec((tm, D_MODEL), lambda i, name=name: (i, COL[name] // D_MODEL))
    row = lambda: pl.BlockSpec((tm, D_MODEL), lambda i: (i, 0))
    full = lambda shape: pl.BlockSpec(shape, lambda i: (0,) * len(shape),
                                      pipeline_mode=pl.Buffered(1))
    return pl.pallas_call(
        _merge_kernel,
        grid=(n // tm,),
        in_specs=[row(), row(), wide("a_gate"), wide("h_gate"), wide("m_gla"), wide("m_hgrn"), row(),
                  full((GLA_V, D_MODEL)), full((HG_V, D_MODEL)), full((D_MODEL, D_MODEL)),
                  full((1, GLA_DV)), full((1, HGRN_DV)), full((1, D_MODEL)), full((1, D_MODEL))],
        out_specs=row(),
        out_shape=jax.ShapeDtypeStruct((n, D_MODEL), F32),
        compiler_params=pltpu.CompilerParams(
            dimension_semantics=("arbitrary",), vmem_limit_bytes=VMEM_LIMIT),
        name="merge",
    )(o_gla, o_hgrn, p2, p2, p2, p2, x2d, w_gla, w_hgrn, w_out, gain_a, gain_h, ln_g, ln_b)


def _gate_params(up_f, up_b, bias_f, bias_b):
    rank = up_f.shape[0]
    up = jnp.zeros((LANES, 2 * GLA_QK), F32)
    up = up.at[:rank, :GLA_QK].set(up_f).at[rank:2 * rank, GLA_QK:].set(up_b)
    bias = jnp.concatenate([bias_f, bias_b])[None, :]
    return up.astype(BF16), bias


def kernel(x, w_in, gla_gk_up_f, gla_gk_bias_f, gla_gk_up_b, gla_gk_bias_b, gla_norm_g,
           hgrn_lb_logits_f, hgrn_lb_logits_b, hgrn_norm_g, w_branch_gla, w_branch_hgrn,
           w_out, ln_g, ln_b):
    assert w_in.shape[0] == DEPTH == 1
    bsz, seq, d = x.shape
    assert d == D_MODEL and seq % SCAN_TILE == 0 and SCAN_TILE % CUMSUM_ROWS == 0
    x2d = x.astype(F32).reshape(bsz * seq, d)

    w_t = jnp.swapaxes(w_in, 1, 2).reshape(w_in.shape[2], w_in.shape[1]).astype(F32)
    w_packed, w_lr = _pack_w(w_t)
    up, gbias = _gate_params(gla_gk_up_f[0].astype(F32), gla_gk_up_b[0].astype(F32),
                             gla_gk_bias_f[0].astype(F32), gla_gk_bias_b[0].astype(F32))
    p2 = _project(x2d, w_packed, w_lr, up, gbias,
                  hgrn_lb_logits_f.astype(F32), hgrn_lb_logits_b.astype(F32))
    p3 = p2.reshape(bsz, seq, PROJ_COLS)

    o_gla = _gla_scan(p3)
    o_hgrn = _hgrn_scan(p3)

    out = _merge(
        o_gla.reshape(bsz * seq, GLA_V), o_hgrn.reshape(bsz * seq, HG_V), p2, x2d,
        w_branch_gla[0].astype(BF16), w_branch_hgrn[0].astype(BF16), w_out[0].astype(BF16),
        gla_norm_g[0].astype(F32).reshape(1, GLA_DV), hgrn_norm_g[0].astype(F32).reshape(1, HGRN_DV),
        ln_g[0].astype(F32).reshape(1, D_MODEL), ln_b[0].astype(F32).reshape(1, D_MODEL))
    return out.reshape(bsz, seq, d).astype(x.dtype)
```

```python
import functools

import jax
import jax.numpy as jnp
import numpy as np
from jax import lax
from jax.experimental import pallas as pl
from jax.experimental.pallas import tpu as pltpu

F32 = jnp.float32
BF16 = jnp.bfloat16

D_MODEL = 1024
DEPTH = 1
GLA_HEADS = 4
GLA_DK = 128
GLA_DV = 256
GLA_GATE_RANK = 16
GLA_GATE_NORMALIZER = 16.0
HGRN_HEADS = 8
HGRN_DK = 128
HGRN_DV = 128
GLA_QK = GLA_HEADS * GLA_DK
GLA_V = GLA_HEADS * GLA_DV
HG_K = HGRN_HEADS * HGRN_DK
HG_V = HGRN_HEADS * HGRN_DV
RMS_EPS = 1e-6
LN_EPS = 1e-5
DEEPNORM_ALPHA = (2.0 * DEPTH) ** 0.25
LOG2E = 1.4426950408889634

LANES = 128

_ORIG_SIZES = (GLA_QK, GLA_QK, GLA_V, GLA_V, GLA_GATE_RANK, GLA_GATE_RANK,
               HG_K, HG_K, HG_K, HG_V, HG_V, D_MODEL, D_MODEL)
_ORIG_NAMES = ("a_q", "a_k", "a_v", "a_gate", "lr_f", "lr_b",
               "h_q", "h_ff", "h_fb", "h_i", "h_gate", "m_gla", "m_hgrn")
_ORIG_START = dict(zip(_ORIG_NAMES, np.cumsum((0,) + _ORIG_SIZES[:-1]).tolist()))
_SIZE = dict(zip(_ORIG_NAMES, _ORIG_SIZES))
_LR_LO = _ORIG_START["lr_f"]
_LR_HI = _ORIG_START["lr_b"] + _SIZE["lr_b"]

_OUT_ORDER = ("a_q", "a_k", "a_v", "a_gate", "g_f", "g_b",
              "h_q", "h_ff", "h_fb", "h_i", "h_gate", "m_gla", "m_hgrn")
_OUT_SIZE = dict(_SIZE, g_f=GLA_QK, g_b=GLA_QK)
COL = dict(zip(_OUT_ORDER, np.cumsum([0] + [_OUT_SIZE[nm] for nm in _OUT_ORDER[:-1]]).tolist()))
PROJ_COLS = sum(_OUT_SIZE[nm] for nm in _OUT_ORDER)

CHUNK = 64
SCAN_TILE = 512
CUMSUM_ROWS = 128
PHASE1_UNROLL = 4
PHASE2_UNROLL = 32
PROJ_TM = 512
PROJ_SUBN = 512
WPACK_TN = 1024
MERGE_TM = 1024
MERGE_SUB = 256
VMEM_LIMIT = 60 * 1024 * 1024


def _sigmoid(x):
    return 1.0 / (1.0 + jnp.exp2(x * (-LOG2E)))


def _lower_bound(logits, layer):
    m = jnp.max(logits, axis=0, keepdims=True)
    e = jnp.exp(logits - m)
    return jnp.sum(e[:layer + 1], axis=0, keepdims=True) / jnp.sum(e, axis=0, keepdims=True)


def _pack_w_kernel(a_ref, b_ref, o_ref, lr_ref):
    j = pl.program_id(0)
    gap = _LR_HI - _LR_LO
    first_shifted = _LR_LO // WPACK_TN

    @pl.when(j < first_shifted)
    def _():
        o_ref[...] = a_ref[...].T.astype(BF16)

    @pl.when(j == first_shifted - 1)
    def _():
        padded = jnp.concatenate([b_ref[...], jnp.zeros((LANES - gap, b_ref.shape[1]), F32)], axis=0)
        lr_ref[...] = padded.T.astype(BF16)

    @pl.when(j >= first_shifted)
    def _():
        rows = jnp.concatenate([a_ref[gap:, :], b_ref[...]], axis=0)
        o_ref[...] = rows.T.astype(BF16)


def _pack_w(w_t):
    d, gap = w_t.shape[1], _LR_HI - _LR_LO
    width = w_t.shape[0] - gap
    assert _LR_LO % WPACK_TN == 0 and _LR_LO > 0 and width % WPACK_TN == 0 and WPACK_TN % gap == 0
    return pl.pallas_call(
        _pack_w_kernel,
        grid=(width // WPACK_TN,),
        in_specs=[pl.BlockSpec((WPACK_TN, d), lambda j: (j, 0)),
                  pl.BlockSpec((gap, d), lambda j: ((j + 1) * (WPACK_TN // gap), 0))],
        out_specs=[pl.BlockSpec((d, WPACK_TN), lambda j: (0, j)),
                   pl.BlockSpec((d, LANES), lambda j: (0, 0))],
        out_shape=[jax.ShapeDtypeStruct((d, width), BF16), jax.ShapeDtypeStruct((d, LANES), BF16)],
        compiler_params=pltpu.CompilerParams(
            dimension_semantics=("arbitrary",), vmem_limit_bytes=VMEM_LIMIT),
        name="pack_w",
    )(w_t, w_t)


def _proj_kernel(x_ref, w_ref, wlr_ref, up_ref, gbias_ref, lf_ref, lb_ref, o_ref):
    xb = x_ref[...].astype(BF16)
    mm = lambda w: jnp.dot(xb, w, preferred_element_type=F32)
    silu = lambda a, cols: a * _sigmoid(a)
    ident = lambda a, cols: a
    squash = lambda a, cols: _sigmoid(a)
    scaled = lambda fn, c: (lambda a, cols: fn(a, cols) * c)

    def log2_forget(logits_ref):
        lb = _lower_bound(logits_ref[...], 0)
        return lambda h, cols: jnp.log2(lb[:, cols] + (1.0 - lb[:, cols]) * _sigmoid(h))

    def w_columns(name):
        lo = _ORIG_START[name] - (_LR_HI - _LR_LO if _ORIG_START[name] >= _LR_HI else 0)
        return lambda cols: mm(w_ref[:, lo + cols.start:lo + cols.stop])

    def emit(name, fn, product=None):
        product = product or w_columns(name)
        for off in range(0, _OUT_SIZE[name], PROJ_SUBN):
            cols = slice(off, off + PROJ_SUBN)
            dst = slice(COL[name] + off, COL[name] + off + PROJ_SUBN)
            o_ref[:, dst] = fn(product(cols), cols).astype(BF16)

    def log_sigmoid_gate(z, cols):
        log_sig = jnp.minimum(z, 0.0) - jnp.log2(1.0 + jnp.exp2(jnp.abs(z) * (-LOG2E))) * (1.0 / LOG2E)
        return log_sig * (LOG2E / GLA_GATE_NORMALIZER)

    emit("a_q", scaled(ident, GLA_DK ** -0.5))
    emit("a_k", ident)
    emit("a_v", ident)
    emit("a_gate", silu)
    lr = mm(wlr_ref[...]).astype(BF16)
    for d, name in enumerate(("g_f", "g_b")):
        base = d * GLA_QK
        emit(name, log_sigmoid_gate, product=lambda cols, base=base: (
            jnp.dot(lr, up_ref[:, base + cols.start:base + cols.stop], preferred_element_type=F32)
            + gbias_ref[:, base + cols.start:base + cols.stop]))
    emit("h_q", scaled(silu, HGRN_DK ** -0.5))
    emit("h_ff", log2_forget(lf_ref))
    emit("h_fb", log2_forget(lb_ref))
    emit("h_i", ident)
    emit("h_gate", silu)
    emit("m_gla", squash)
    emit("m_hgrn", squash)


def _project(x2d, w_packed, w_lr, up, gbias, logits_f, logits_b):
    n = x2d.shape[0]
    assert n % PROJ_TM == 0
    resident = lambda a: pl.BlockSpec(a.shape, lambda i: (0,) * a.ndim,
                                      pipeline_mode=pl.Buffered(1))
    consts = (w_packed, w_lr, up, gbias, logits_f, logits_b)
    return pl.pallas_call(
        _proj_kernel,
        grid=(n // PROJ_TM,),
        in_specs=[pl.BlockSpec((PROJ_TM, D_MODEL), lambda i: (i, 0))] + [resident(a) for a in consts],
        out_specs=pl.BlockSpec((PROJ_TM, PROJ_COLS), lambda i: (i, 0)),
        out_shape=jax.ShapeDtypeStruct((n, PROJ_COLS), BF16),
        compiler_params=pltpu.CompilerParams(
            dimension_semantics=("arbitrary",), vmem_limit_bytes=VMEM_LIMIT),
        name="proj",
    )(x2d, *consts)


def _dot_nt(a, b):
    return lax.dot_general(a, b, (((1,), (1,)), ((), ())), preferred_element_type=F32)


def _dot_tn(a, b):
    return lax.dot_general(a, b, (((0,), (0,)), ((), ())), preferred_element_type=F32)


def _block_diag_cumsum_mat(rows, c):
    row = lax.broadcasted_iota(jnp.int32, (rows, rows), 0)
    col = lax.broadcasted_iota(jnp.int32, (rows, rows), 1)
    return jnp.where(((row // c) == (col // c)) & (col <= row), 1.0, 0.0).astype(BF16)


def _scan_core(q_ref, k_of, g_refs, v_ref, o_ref, scratch, seq, dk, dv):
    qi_ref, upd_ref, dec_ref, oacc_ref, cum_ref, st_ref = scratch
    c = CHUNK
    pair = 2 * c
    tile = SCAN_TILE
    pairs_per_tile = tile // pair
    n_tiles = seq // tile
    n = seq // pair
    mid = c // 2
    row = lax.broadcasted_iota(jnp.int32, (pair, pair), 0)
    col = lax.broadcasted_iota(jnp.int32, (pair, pair), 1)
    masks = (col <= row, col >= row)
    cum_mat = _block_diag_cumsum_mat(CUMSUM_ROWS, c)

    def tile_rows(t):
        return pl.ds(pl.multiple_of(t * tile, tile), tile)

    def stage(t):
        g_pair = jnp.concatenate([g_refs[0][0, tile_rows(t), :], g_refs[1][0, tile_rows(t), :]], axis=1)
        for r in range(0, tile, CUMSUM_ROWS):
            cum_ref[r:r + CUMSUM_ROWS, :] = jnp.dot(cum_mat, g_pair[r:r + CUMSUM_ROWS],
                                                    preferred_element_type=F32)

    stage(0)

    def chunk_terms(d, q, k, g, cum):
        if d == 0:
            b_mid = cum[mid:mid + 1, :]
            total = cum[c - 1:c, :]
            rel, lead, trail = cum - b_mid, b_mid, total - b_mid
        else:
            e = cum - g
            e_mid = e[mid:mid + 1, :]
            total = cum[c - 1:c, :]
            rel, lead, trail = e_mid - e, total - e_mid, e_mid
        return q * jnp.exp2(rel), k * jnp.exp2(-rel), lead, trail, total

    def phase1(i, carry):
        rows = tile_rows(i)
        q = q_ref[0, rows, :].astype(F32)
        gs = (g_refs[0][0, rows, :].astype(F32), g_refs[1][0, rows, :].astype(F32))
        ks = (k_of(0, gs[0], rows), k_of(1, gs[1], rows))
        cum = cum_ref[...]
        v = v_ref[0, rows, :]
        bf = lambda a: a.astype(BF16)

        def scores_of(p):
            pair_rows = pl.ds(pl.multiple_of(i * tile + p * pair, pair), pair)
            sa, sb = slice(p * pair, p * pair + c), slice(p * pair + c, (p + 1) * pair)
            acc, kp_pair = None, []
            for d in range(2):
                lanes = slice(d * dk, (d + 1) * dk)
                qa, ka, lead_a, trail_a, tot_a = chunk_terms(d, q[sa], ks[d][sa], gs[d][sa], cum[sa, lanes])
                qb, kb, lead_b, trail_b, tot_b = chunk_terms(d, q[sb], ks[d][sb], gs[d][sb], cum[sb, lanes])
                if d == 0:
                    keys_a = jnp.concatenate([bf(ka), bf(kb)], axis=0)
                    keys_b = jnp.concatenate([bf(ka * jnp.exp2(trail_a + lead_b)), bf(kb)], axis=0)
                    qi = jnp.concatenate([qa * jnp.exp2(lead_a), qb * jnp.exp2(lead_b + tot_a)], axis=0)
                    kp = jnp.concatenate([ka * jnp.exp2(trail_a + tot_b), kb * jnp.exp2(trail_b)], axis=0)
                else:
                    keys_a = jnp.concatenate([bf(ka), bf(kb * jnp.exp2(trail_b + lead_a))], axis=0)
                    keys_b = jnp.concatenate([bf(ka), bf(kb)], axis=0)
                    qi = jnp.concatenate([qa * jnp.exp2(lead_a + tot_b), qb * jnp.exp2(lead_b)], axis=0)
                    kp = jnp.concatenate([ka * jnp.exp2(trail_a), kb * jnp.exp2(trail_b + tot_a)], axis=0)
                scores = jnp.concatenate([_dot_nt(bf(qa), keys_a), _dot_nt(bf(qb), keys_b)], axis=0)
                scores = jnp.where(masks[d], scores, 0.0)
                acc = scores if acc is None else acc + scores
                lane_decay = jnp.broadcast_to(jnp.exp2(tot_a + tot_b), (LANES, dk)).T
                if d == 0:
                    qi_f, decay_f = bf(qi), lane_decay
                else:
                    qi_ref[pair_rows, :] = bf(qi)
                    dec_ref[i * pairs_per_tile + p] = lane_decay
                kp_pair.append(bf(kp))
            return bf(acc), jnp.concatenate(kp_pair, axis=1), qi_f, decay_f

        def increment(p, kp):
            upd = _dot_tn(kp, v[p * pair:(p + 1) * pair])
            upd_ref[i * pairs_per_tile + p] = upd[dk:]
            return upd[:dk]

        def output(p, scores, qi_f, decay_f, upd_f):
            pair_rows = pl.ds(pl.multiple_of(i * tile + p * pair, pair), pair)
            st = st_ref[...]
            oacc_ref[pair_rows, :] = jnp.dot(
                jnp.concatenate([scores, qi_f], axis=1),
                jnp.concatenate([v[p * pair:(p + 1) * pair], bf(st)], axis=0),
                preferred_element_type=F32)
            st_ref[...] = st * jnp.concatenate([decay_f] * (dv // LANES), axis=1) + upd_f

        done, upd_f = {}, {}
        for s in range(pairs_per_tile + 2):
            if s < pairs_per_tile:
                done[s] = scores_of(s)
            if 1 <= s <= pairs_per_tile:
                upd_f[s - 1] = increment(s - 1, done[s - 1][1])
            if s == pairs_per_tile - 1:
                stage(jnp.minimum(i + 1, n_tiles - 1))
            if s >= 2:
                scores, _, qi_f, decay_f = done[s - 2]
                output(s - 2, scores, qi_f, decay_f, upd_f[s - 2])
        return carry

    st_ref[...] = jnp.zeros_like(st_ref)
    lax.fori_loop(0, n_tiles, phase1, 0, unroll=PHASE1_UNROLL)

    def backward(j, st):
        ci = n - 1 - j
        rows = pl.ds(pl.multiple_of(ci * pair, pair), pair)
        o = oacc_ref[rows, :] + jnp.dot(qi_ref[rows, :], st.astype(BF16), preferred_element_type=F32)
        o_ref[0, rows, :] = o.astype(o_ref.dtype)
        return st * jnp.concatenate([dec_ref[ci]] * (dv // LANES), axis=1) + upd_ref[ci]

    lax.fori_loop(0, n, backward, jnp.zeros((dk, dv), F32), unroll=PHASE2_UNROLL)


def _scan_scratch(seq, dk, dv):
    n = seq // (2 * CHUNK)
    return [pltpu.VMEM((seq, dk), BF16),
            pltpu.VMEM((n, dk, dv), F32),
            pltpu.VMEM((n, dk, LANES), F32),
            pltpu.VMEM((seq, dv), F32),
            pltpu.VMEM((SCAN_TILE, 2 * dk), F32),
            pltpu.VMEM((dk, dv), F32)]


def _scan_call(kernel_fn, name, p3, specs, heads, dk, dv):
    bsz, seq, _ = p3.shape
    return pl.pallas_call(
        functools.partial(kernel_fn, seq=seq),
        grid=(bsz, heads),
        in_specs=specs,
        out_specs=pl.BlockSpec((1, seq, dv), lambda b, h: (b, 0, h)),
        out_shape=jax.ShapeDtypeStruct((bsz, seq, heads * dv), BF16),
        scratch_shapes=_scan_scratch(seq, dk, dv),
        compiler_params=pltpu.CompilerParams(
            dimension_semantics=("arbitrary", "arbitrary"), vmem_limit_bytes=VMEM_LIMIT),
        name=name,
    )(*([p3] * len(specs)))


def _head_block(seq, name, width):
    first = COL[name] // width
    return pl.BlockSpec((1, seq, width), lambda b, h: (b, 0, first + h))


def _gla_kernel(q_ref, k_ref, gf_ref, gb_ref, v_ref, o_ref, *scratch, seq):
    k_of = lambda d, g, rows: k_ref[0, rows, :].astype(F32)
    _scan_core(q_ref, k_of, (gf_ref, gb_ref), v_ref, o_ref, scratch, seq, GLA_DK, GLA_DV)


def _gla_scan(p3):
    seq = p3.shape[1]
    specs = [_head_block(seq, "a_q", GLA_DK), _head_block(seq, "a_k", GLA_DK),
             _head_block(seq, "g_f", GLA_DK), _head_block(seq, "g_b", GLA_DK),
             _head_block(seq, "a_v", GLA_DV)]
    return _scan_call(_gla_kernel, "gla_scan", p3, specs, GLA_HEADS, GLA_DK, GLA_DV)


def _hgrn_kernel(q_ref, gf_ref, gb_ref, v_ref, o_ref, *scratch, seq):
    k_of = lambda d, g, rows: 1.0 - jnp.exp2(g)
    _scan_core(q_ref, k_of, (gf_ref, gb_ref), v_ref, o_ref, scratch, seq, HGRN_DK, HGRN_DV)


def _hgrn_scan(p3):
    seq = p3.shape[1]
    specs = [_head_block(seq, "h_q", HGRN_DK), _head_block(seq, "h_ff", HGRN_DK),
             _head_block(seq, "h_fb", HGRN_DK), _head_block(seq, "h_i", HGRN_DV)]
    return _scan_call(_hgrn_kernel, "hgrn_scan", p3, specs, HGRN_HEADS, HGRN_DK, HGRN_DV)


def _head_rmsnorm_gate(o, gain, gate, heads, dv):
    parts = []
    for h in range(heads):
        seg = o[:, h * dv:(h + 1) * dv]
        ms = jnp.mean(seg * seg, axis=-1, keepdims=True)
        parts.append(seg * lax.rsqrt(ms + RMS_EPS) * gain)
    return (jnp.concatenate(parts, axis=1) * gate).astype(BF16)


def _merge_kernel(oa_ref, oh_ref, ag_ref, hg_ref, mg_ref, mh_ref, x_ref, wa_ref, wh_ref, wo_ref,
                  ga_ref, gh_ref, lng_ref, lnb_ref, out_ref):
    subs = [slice(r, r + MERGE_SUB) for r in range(0, out_ref.shape[0], MERGE_SUB)]
    mixed = []
    for sl in subs:
        na = _head_rmsnorm_gate(oa_ref[sl, :].astype(F32), ga_ref[...], ag_ref[sl, :].astype(F32),
                                GLA_HEADS, GLA_DV)
        nh = _head_rmsnorm_gate(oh_ref[sl, :].astype(F32), gh_ref[...], hg_ref[sl, :].astype(F32),
                                HGRN_HEADS, HGRN_DV)
        y_gla = jnp.dot(na, wa_ref[...], preferred_element_type=F32)
        y_hgrn = jnp.dot(nh, wh_ref[...], preferred_element_type=F32)
        mixed.append((mg_ref[sl, :].astype(F32) * y_gla
                      + mh_ref[sl, :].astype(F32) * y_hgrn).astype(BF16))
    for sl, y in zip(subs, mixed):
        r = DEEPNORM_ALPHA * x_ref[sl, :] + jnp.dot(y, wo_ref[...], preferred_element_type=F32)
        mu = jnp.mean(r, axis=-1, keepdims=True)
        rc = r - mu
        var = jnp.mean(rc * rc, axis=-1, keepdims=True)
        out_ref[sl, :] = rc * lax.rsqrt(var + LN_EPS) * lng_ref[...] + lnb_ref[...]


def _merge(o_gla, o_hgrn, p2, x2d, w_gla, w_hgrn, w_out, gain_a, gain_h, ln_g, ln_b):
    n = x2d.shape[0]
    tm = MERGE_TM
    wide = lambda name: pl.BlockSpec((tm, D_MODEL), lambda i, name=name: (i, COL[name] // D_MODEL))
    row = lambda: pl.BlockSpec((tm, D_MODEL), lambda i: (i, 0))
    full = lambda shape: pl.BlockSpec(shape, lambda i: (0,) * len(shape),
                                      pipeline_mode=pl.Buffered(1))
    return pl.pallas_call(
        _merge_kernel,
        grid=(n // tm,),
        in_specs=[row(), row(), wide("a_gate"), wide("h_gate"), wide("m_gla"), wide("m_hgrn"), row(),
                  full((GLA_V, D_MODEL)), full((HG_V, D_MODEL)), full((D_MODEL, D_MODEL)),
                  full((1, GLA_DV)), full((1, HGRN_DV)), full((1, D_MODEL)), full((1, D_MODEL))],
        out_specs=row(),
        out_shape=jax.ShapeDtypeStruct((n, D_MODEL), F32),
        compiler_params=pltpu.CompilerParams(
            dimension_semantics=("arbitrary",), vmem_limit_bytes=VMEM_LIMIT),
        name="merge",
    )(o_gla, o_hgrn, p2, p2, p2, p2, x2d, w_gla, w_hgrn, w_out, gain_a, gain_h, ln_g, ln_b)


def _gate_params(up_f, up_b, bias_f, bias_b):
    rank = up_f.shape[0]
    up = jnp.zeros((LANES, 2 * GLA_QK), F32)
    up = up.at[:rank, :GLA_QK].set(up_f).at[rank:2 * rank, GLA_QK:].set(up_b)
    bias = jnp.concatenate([bias_f, bias_b])[None, :]
    return up.astype(BF16), bias


def kernel(x, w_in, gla_gk_up_f, gla_gk_bias_f, gla_gk_up_b, gla_gk_bias_b, gla_norm_g,
           hgrn_lb_logits_f, hgrn_lb_logits_b, hgrn_norm_g, w_branch_gla, w_branch_hgrn,
           w_out, ln_g, ln_b):
    assert w_in.shape[0] == DEPTH == 1
    bsz, seq, d = x.shape
    assert d == D_MODEL and seq % SCAN_TILE == 0 and SCAN_TILE % CUMSUM_ROWS == 0
    x2d = x.astype(F32).reshape(bsz * seq, d)

    w_t = jnp.swapaxes(w_in, 1, 2).reshape(w_in.shape[2], w_in.shape[1]).astype(F32)
    w_packed, w_lr = _pack_w(w_t)
    up, gbias = _gate_params(gla_gk_up_f[0].astype(F32), gla_gk_up_b[0].astype(F32),
                             gla_gk_bias_f[0].astype(F32), gla_gk_bias_b[0].astype(F32))
    p2 = _project(x2d, w_packed, w_lr, up, gbias,
                  hgrn_lb_logits_f.astype(F32), hgrn_lb_logits_b.astype(F32))
    p3 = p2.reshape(bsz, seq, PROJ_COLS)

    o_gla = _gla_scan(p3)
    o_hgrn = _hgrn_scan(p3)

    out = _merge(
        o_gla.reshape(bsz * seq, GLA_V), o_hgrn.reshape(bsz * seq, HG_V), p2, x2d,
        w_branch_gla[0].astype(BF16), w_branch_hgrn[0].astype(BF16), w_out[0].astype(BF16),
        gla_norm_g[0].astype(F32).reshape(1, GLA_DV), hgrn_norm_g[0].astype(F32).reshape(1, HGRN_DV),
        ln_g[0].astype(F32).reshape(1, D_MODEL), ln_b[0].astype(F32).reshape(1, D_MODEL))
    return out.reshape(bsz, seq, d).astype(x.dtype)
```

```python
import functools

import jax
import jax.numpy as jnp
import numpy as np
from jax import lax
from jax.experimental import pallas as pl
from jax.experimental.pallas import tpu as pltpu

F32 = jnp.float32
BF16 = jnp.bfloat16

D_MODEL = 1024
DEPTH = 1
GLA_HEADS = 4
GLA_DK = 128
GLA_DV = 256
GLA_GATE_RANK = 16
GLA_GATE_NORMALIZER = 16.0
HGRN_HEADS = 8
HGRN_DK = 128
HGRN_DV = 128
GLA_QK = GLA_HEADS * GLA_DK
GLA_V = GLA_HEADS * GLA_DV
HG_K = HGRN_HEADS * HGRN_DK
HG_V = HGRN_HEADS * HGRN_DV
RMS_EPS = 1e-6
LN_EPS = 1e-5
DEEPNORM_ALPHA = (2.0 * DEPTH) ** 0.25
LOG2E = 1.4426950408889634

LANES = 128

_ORIG_SIZES = (GLA_QK, GLA_QK, GLA_V, GLA_V, GLA_GATE_RANK, GLA_GATE_RANK,
               HG_K, HG_K, HG_K, HG_V, HG_V, D_MODEL, D_MODEL)
_ORIG_NAMES = ("a_q", "a_k", "a_v", "a_gate", "lr_f", "lr_b",
               "h_q", "h_ff", "h_fb", "h_i", "h_gate", "m_gla", "m_hgrn")
_ORIG_START = dict(zip(_ORIG_NAMES, np.cumsum((0,) + _ORIG_SIZES[:-1]).tolist()))
_SIZE = dict(zip(_ORIG_NAMES, _ORIG_SIZES))
_LR_LO = _ORIG_START["lr_f"]
_LR_HI = _ORIG_START["lr_b"] + _SIZE["lr_b"]

_OUT_ORDER = ("a_q", "a_k", "a_v", "a_gate", "g_f", "g_b",
              "h_q", "h_ff", "h_fb", "h_i", "h_gate", "m_gla", "m_hgrn")
_OUT_SIZE = dict(_SIZE, g_f=GLA_QK, g_b=GLA_QK)
COL = dict(zip(_OUT_ORDER, np.cumsum([0] + [_OUT_SIZE[nm] for nm in _OUT_ORDER[:-1]]).tolist()))
PROJ_COLS = sum(_OUT_SIZE[nm] for nm in _OUT_ORDER)

CHUNK = 64
SCAN_TILE = 512
CUMSUM_ROWS = 128
PHASE1_UNROLL = 4
PHASE2_UNROLL = 32
GLA_HEADS_PER_STEP = 2
HGRN_HEADS_PER_STEP = 4
PROJ_TM = 512
PROJ_SUBN = 512
WPACK_TN = 1024
MERGE_TM = 1024
MERGE_SUB = 256
VMEM_LIMIT = 60 * 1024 * 1024


def _sigmoid(x):
    return 1.0 / (1.0 + jnp.exp2(x * (-LOG2E)))


def _lower_bound(logits, layer):
    m = jnp.max(logits, axis=0, keepdims=True)
    e = jnp.exp(logits - m)
    return jnp.sum(e[:layer + 1], axis=0, keepdims=True) / jnp.sum(e, axis=0, keepdims=True)


def _pack_w_kernel(a_ref, b_ref, o_ref, lr_ref):
    j = pl.program_id(0)
    gap = _LR_HI - _LR_LO
    first_shifted = _LR_LO // WPACK_TN

    @pl.when(j < first_shifted)
    def _():
        o_ref[...] = a_ref[...].T.astype(BF16)

    @pl.when(j == first_shifted - 1)
    def _():
        padded = jnp.concatenate([b_ref[...], jnp.zeros((LANES - gap, b_ref.shape[1]), F32)], axis=0)
        lr_ref[...] = padded.T.astype(BF16)

    @pl.when(j >= first_shifted)
    def _():
        rows = jnp.concatenate([a_ref[gap:, :], b_ref[...]], axis=0)
        o_ref[...] = rows.T.astype(BF16)


def _pack_w(w_t):
    d, gap = w_t.shape[1], _LR_HI - _LR_LO
    width = w_t.shape[0] - gap
    assert _LR_LO % WPACK_TN == 0 and _LR_LO > 0 and width % WPACK_TN == 0 and WPACK_TN % gap == 0
    return pl.pallas_call(
        _pack_w_kernel,
        grid=(width // WPACK_TN,),
        in_specs=[pl.BlockSpec((WPACK_TN, d), lambda j: (j, 0)),
                  pl.BlockSpec((gap, d), lambda j: ((j + 1) * (WPACK_TN // gap), 0))],
        out_specs=[pl.BlockSpec((d, WPACK_TN), lambda j: (0, j)),
                   pl.BlockSpec((d, LANES), lambda j: (0, 0))],
        out_shape=[jax.ShapeDtypeStruct((d, width), BF16), jax.ShapeDtypeStruct((d, LANES), BF16)],
        compiler_params=pltpu.CompilerParams(
            dimension_semantics=("arbitrary",), vmem_limit_bytes=VMEM_LIMIT),
        name="pack_w",
    )(w_t, w_t)


def _proj_kernel(x_ref, w_ref, wlr_ref, up_ref, gbias_ref, lf_ref, lb_ref, o_ref):
    xb = x_ref[...].astype(BF16)
    mm = lambda w: jnp.dot(xb, w, preferred_element_type=F32)
    silu = lambda a, cols: a * _sigmoid(a)
    ident = lambda a, cols: a
    squash = lambda a, cols: _sigmoid(a)
    scaled = lambda fn, c: (lambda a, cols: fn(a, cols) * c)

    def log2_forget(logits_ref):
        lb = _lower_bound(logits_ref[...], 0)
        return lambda h, cols: jnp.log2(lb[:, cols] + (1.0 - lb[:, cols]) * _sigmoid(h))

    def w_columns(name):
        lo = _ORIG_START[name] - (_LR_HI - _LR_LO if _ORIG_START[name] >= _LR_HI else 0)
        return lambda cols: mm(w_ref[:, lo + cols.start:lo + cols.stop])

    def emit(name, fn, product=None):
        product = product or w_columns(name)
        for off in range(0, _OUT_SIZE[name], PROJ_SUBN):
            cols = slice(off, off + PROJ_SUBN)
            dst = slice(COL[name] + off, COL[name] + off + PROJ_SUBN)
            o_ref[:, dst] = fn(product(cols), cols).astype(BF16)

    def log_sigmoid_gate(z, cols):
        log_sig = jnp.minimum(z, 0.0) - jnp.log2(1.0 + jnp.exp2(jnp.abs(z) * (-LOG2E))) * (1.0 / LOG2E)
        return log_sig * (LOG2E / GLA_GATE_NORMALIZER)

    emit("a_q", scaled(ident, GLA_DK ** -0.5))
    emit("a_k", ident)
    emit("a_v", ident)
    emit("a_gate", silu)
    lr = mm(wlr_ref[...]).astype(BF16)
    for d, name in enumerate(("g_f", "g_b")):
        base = d * GLA_QK
        emit(name, log_sigmoid_gate, product=lambda cols, base=base: (
            jnp.dot(lr, up_ref[:, base + cols.start:base + cols.stop], preferred_element_type=F32)
            + gbias_ref[:, base + cols.start:base + cols.stop]))
    emit("h_q", scaled(silu, HGRN_DK ** -0.5))
    emit("h_ff", log2_forget(lf_ref))
    emit("h_fb", log2_forget(lb_ref))
    emit("h_i", ident)
    emit("h_gate", silu)
    emit("m_gla", squash)
    emit("m_hgrn", squash)


def _project(x2d, w_packed, w_lr, up, gbias, logits_f, logits_b):
    n = x2d.shape[0]
    assert n % PROJ_TM == 0
    resident = lambda a: pl.BlockSpec(a.shape, lambda i: (0,) * a.ndim,
                                      pipeline_mode=pl.Buffered(1))
    consts = (w_packed, w_lr, up, gbias, logits_f, logits_b)
    return pl.pallas_call(
        _proj_kernel,
        grid=(n // PROJ_TM,),
        in_specs=[pl.BlockSpec((PROJ_TM, D_MODEL), lambda i: (i, 0))] + [resident(a) for a in consts],
        out_specs=pl.BlockSpec((PROJ_TM, PROJ_COLS), lambda i: (i, 0)),
        out_shape=jax.ShapeDtypeStruct((n, PROJ_COLS), BF16),
        compiler_params=pltpu.CompilerParams(
            dimension_semantics=("arbitrary",), vmem_limit_bytes=VMEM_LIMIT),
        name="proj",
    )(x2d, *consts)


def _dot_nt(a, b):
    return lax.dot_general(a, b, (((1,), (1,)), ((), ())), preferred_element_type=F32)


def _dot_tn(a, b):
    return lax.dot_general(a, b, (((0,), (0,)), ((), ())), preferred_element_type=F32)


def _block_diag_cumsum_mat(rows, c):
    row = lax.broadcasted_iota(jnp.int32, (rows, rows), 0)
    col = lax.broadcasted_iota(jnp.int32, (rows, rows), 1)
    return jnp.where(((row // c) == (col // c)) & (col <= row), 1.0, 0.0).astype(BF16)


def _scan_core(q_ref, k_of, g_refs, v_ref, o_ref, scratch, seq, dk, dv, head):
    qi_ref, upd_ref, dec_ref, oacc_ref, cum_ref, st_ref = scratch
    kcols, vcols = slice(head * dk, (head + 1) * dk), slice(head * dv, (head + 1) * dv)
    c = CHUNK
    pair = 2 * c
    tile = SCAN_TILE
    pairs_per_tile = tile // pair
    n_tiles = seq // tile
    n = seq // pair
    mid = c // 2
    row = lax.broadcasted_iota(jnp.int32, (pair, pair), 0)
    col = lax.broadcasted_iota(jnp.int32, (pair, pair), 1)
    masks = (col <= row, col >= row)
    cum_mat = _block_diag_cumsum_mat(CUMSUM_ROWS, c)

    def tile_rows(t):
        return pl.ds(pl.multiple_of(t * tile, tile), tile)

    def stage(t):
        g_pair = jnp.concatenate([g_refs[0][0, tile_rows(t), kcols], g_refs[1][0, tile_rows(t), kcols]], axis=1)
        for r in range(0, tile, CUMSUM_ROWS):
            cum_ref[r:r + CUMSUM_ROWS, :] = jnp.dot(cum_mat, g_pair[r:r + CUMSUM_ROWS],
                                                    preferred_element_type=F32)

    stage(0)

    def chunk_terms(d, q, k, g, cum):
        if d == 0:
            b_mid = cum[mid:mid + 1, :]
            total = cum[c - 1:c, :]
            rel, lead, trail = cum - b_mid, b_mid, total - b_mid
        else:
            e = cum - g
            e_mid = e[mid:mid + 1, :]
            total = cum[c - 1:c, :]
            rel, lead, trail = e_mid - e, total - e_mid, e_mid
        return q * jnp.exp2(rel), k * jnp.exp2(-rel), lead, trail, total

    def phase1(i, carry):
        rows = tile_rows(i)
        q = q_ref[0, rows, kcols].astype(F32)
        gs = (g_refs[0][0, rows, kcols].astype(F32), g_refs[1][0, rows, kcols].astype(F32))
        ks = (k_of(0, gs[0], rows), k_of(1, gs[1], rows))
        cum = cum_ref[...]
        v = v_ref[0, rows, vcols]
        bf = lambda a: a.astype(BF16)

        def scores_of(p):
            pair_rows = pl.ds(pl.multiple_of(i * tile + p * pair, pair), pair)
            sa, sb = slice(p * pair, p * pair + c), slice(p * pair + c, (p + 1) * pair)
            acc, kp_pair = None, []
            for d in range(2):
                lanes = slice(d * dk, (d + 1) * dk)
                qa, ka, lead_a, trail_a, tot_a = chunk_terms(d, q[sa], ks[d][sa], gs[d][sa], cum[sa, lanes])
                qb, kb, lead_b, trail_b, tot_b = chunk_terms(d, q[sb], ks[d][sb], gs[d][sb], cum[sb, lanes])
                if d == 0:
                    keys_a = jnp.concatenate([bf(ka), bf(kb)], axis=0)
                    keys_b = jnp.concatenate([bf(ka * jnp.exp2(trail_a + lead_b)), bf(kb)], axis=0)
                    qi = jnp.concatenate([qa * jnp.exp2(lead_a), qb * jnp.exp2(lead_b + tot_a)], axis=0)
                    kp = jnp.concatenate([ka * jnp.exp2(trail_a + tot_b), kb * jnp.exp2(trail_b)], axis=0)
                else:
                    keys_a = jnp.concatenate([bf(ka), bf(kb * jnp.exp2(trail_b + lead_a))], axis=0)
                    keys_b = jnp.concatenate([bf(ka), bf(kb)], axis=0)
                    qi = jnp.concatenate([qa * jnp.exp2(lead_a + tot_b), qb * jnp.exp2(lead_b)], axis=0)
                    kp = jnp.concatenate([ka * jnp.exp2(trail_a), kb * jnp.exp2(trail_b + tot_a)], axis=0)
                scores = jnp.concatenate([_dot_nt(bf(qa), keys_a), _dot_nt(bf(qb), keys_b)], axis=0)
                scores = jnp.where(masks[d], scores, 0.0)
                acc = scores if acc is None else acc + scores
                lane_decay = jnp.broadcast_to(jnp.exp2(tot_a + tot_b), (LANES, dk)).T
                if d == 0:
                    qi_f, decay_f = bf(qi), lane_decay
                else:
                    qi_ref[pair_rows, :] = bf(qi)
                    dec_ref[i * pairs_per_tile + p] = lane_decay
                kp_pair.append(bf(kp))
            return bf(acc), jnp.concatenate(kp_pair, axis=1), qi_f, decay_f

        def increment(p, kp):
            upd = _dot_tn(kp, v[p * pair:(p + 1) * pair])
            upd_ref[i * pairs_per_tile + p] = upd[dk:]
            return upd[:dk]

        def output(p, scores, qi_f, decay_f, upd_f):
            pair_rows = pl.ds(pl.multiple_of(i * tile + p * pair, pair), pair)
            st = st_ref[...]
            oacc_ref[pair_rows, :] = jnp.dot(
                jnp.concatenate([scores, qi_f], axis=1),
                jnp.concatenate([v[p * pair:(p + 1) * pair], bf(st)], axis=0),
                preferred_element_type=F32)
            st_ref[...] = st * jnp.concatenate([decay_f] * (dv // LANES), axis=1) + upd_f

        done, upd_f = {}, {}
        for s in range(pairs_per_tile + 2):
            if s < pairs_per_tile:
                done[s] = scores_of(s)
            if 1 <= s <= pairs_per_tile:
                upd_f[s - 1] = increment(s - 1, done[s - 1][1])
            if s == pairs_per_tile - 1:
                stage(jnp.minimum(i + 1, n_tiles - 1))
            if s >= 2:
                scores, _, qi_f, decay_f = done[s - 2]
                output(s - 2, scores, qi_f, decay_f, upd_f[s - 2])
        return carry

    st_ref[...] = jnp.zeros_like(st_ref)
    lax.fori_loop(0, n_tiles, phase1, 0, unroll=PHASE1_UNROLL)

    def backward(j, st):
        ci = n - 1 - j
        rows = pl.ds(pl.multiple_of(ci * pair, pair), pair)
        o = oacc_ref[rows, :] + jnp.dot(qi_ref[rows, :], st.astype(BF16), preferred_element_type=F32)
        o_ref[0, rows, vcols] = o.astype(o_ref.dtype)
        return st * jnp.concatenate([dec_ref[ci]] * (dv // LANES), axis=1) + upd_ref[ci]

    lax.fori_loop(0, n, backward, jnp.zeros((dk, dv), F32), unroll=PHASE2_UNROLL)


def _scan_scratch(seq, dk, dv):
    n = seq // (2 * CHUNK)
    return [pltpu.VMEM((seq, dk), BF16),
            pltpu.VMEM((n, dk, dv), F32),
            pltpu.VMEM((n, dk, LANES), F32),
            pltpu.VMEM((seq, dv), F32),
            pltpu.VMEM((SCAN_TILE, 2 * dk), F32),
            pltpu.VMEM((dk, dv), F32)]


def _scan_call(kernel_fn, name, p3, specs, heads, per_step, dk, dv):
    bsz, seq, _ = p3.shape
    assert heads % per_step == 0
    return pl.pallas_call(
        functools.partial(kernel_fn, seq=seq),
        grid=(bsz, heads // per_step),
        in_specs=specs,
        out_specs=pl.BlockSpec((1, seq, per_step * dv), lambda b, h: (b, 0, h)),
        out_shape=jax.ShapeDtypeStruct((bsz, seq, heads * dv), BF16),
        scratch_shapes=_scan_scratch(seq, dk, dv),
        compiler_params=pltpu.CompilerParams(
            dimension_semantics=("arbitrary", "arbitrary"), vmem_limit_bytes=VMEM_LIMIT),
        name=name,
    )(*([p3] * len(specs)))


def _head_block(seq, name, width):
    assert COL[name] % width == 0
    first = COL[name] // width
    return pl.BlockSpec((1, seq, width), lambda b, h: (b, 0, first + h))


def _gla_kernel(q_ref, k_ref, gf_ref, gb_ref, v_ref, o_ref, *scratch, seq):
    for head in range(GLA_HEADS_PER_STEP):
        k_of = lambda d, g, rows, head=head: (
            k_ref[0, rows, head * GLA_DK:(head + 1) * GLA_DK].astype(F32))
        _scan_core(q_ref, k_of, (gf_ref, gb_ref), v_ref, o_ref, scratch, seq, GLA_DK, GLA_DV, head)


def _gla_scan(p3):
    seq = p3.shape[1]
    kw, vw = GLA_HEADS_PER_STEP * GLA_DK, GLA_HEADS_PER_STEP * GLA_DV
    specs = [_head_block(seq, "a_q", kw), _head_block(seq, "a_k", kw),
             _head_block(seq, "g_f", kw), _head_block(seq, "g_b", kw), _head_block(seq, "a_v", vw)]
    return _scan_call(_gla_kernel, "gla_scan", p3, specs, GLA_HEADS, GLA_HEADS_PER_STEP, GLA_DK, GLA_DV)


def _hgrn_kernel(q_ref, gf_ref, gb_ref, v_ref, o_ref, *scratch, seq):
    k_of = lambda d, g, rows: 1.0 - jnp.exp2(g)
    for head in range(HGRN_HEADS_PER_STEP):
        _scan_core(q_ref, k_of, (gf_ref, gb_ref), v_ref, o_ref, scratch, seq, HGRN_DK, HGRN_DV, head)


def _hgrn_scan(p3):
    seq = p3.shape[1]
    kw, vw = HGRN_HEADS_PER_STEP * HGRN_DK, HGRN_HEADS_PER_STEP * HGRN_DV
    specs = [_head_block(seq, "h_q", kw), _head_block(seq, "h_ff", kw),
             _head_block(seq, "h_fb", kw), _head_block(seq, "h_i", vw)]
    return _scan_call(_hgrn_kernel, "hgrn_scan", p3, specs, HGRN_HEADS, HGRN_HEADS_PER_STEP,
                      HGRN_DK, HGRN_DV)


def _head_rmsnorm_gate(o, gain, gate, heads, dv):
    parts = []
    for h in range(heads):
        seg = o[:, h * dv:(h + 1) * dv]
        ms = jnp.mean(seg * seg, axis=-1, keepdims=True)
        parts.append(seg * lax.rsqrt(ms + RMS_EPS) * gain)
    return (jnp.concatenate(parts, axis=1) * gate).astype(BF16)


def _merge_kernel(oa_ref, oh_ref, ag_ref, hg_ref, mg_ref, mh_ref, x_ref, wa_ref, wh_ref, wo_ref,
                  ga_ref, gh_ref, lng_ref, lnb_ref, out_ref):
    subs = [slice(r, r + MERGE_SUB) for r in range(0, out_ref.shape[0], MERGE_SUB)]
    mixed = []
    for sl in subs:
        na = _head_rmsnorm_gate(oa_ref[sl, :].astype(F32), ga_ref[...], ag_ref[sl, :].astype(F32),
                                GLA_HEADS, GLA_DV)
        nh = _head_rmsnorm_gate(oh_ref[sl, :].astype(F32), gh_ref[...], hg_ref[sl, :].astype(F32),
                                HGRN_HEADS, HGRN_DV)
        y_gla = jnp.dot(na, wa_ref[...], preferred_element_type=F32)
        y_hgrn = jnp.dot(nh, wh_ref[...], preferred_element_type=F32)
        mixed.append((mg_ref[sl, :].astype(F32) * y_gla
                      + mh_ref[sl, :].astype(F32) * y_hgrn).astype(BF16))
    for sl, y in zip(subs, mixed):
        r = DEEPNORM_ALPHA * x_ref[sl, :] + jnp.dot(y, wo_ref[...], preferred_element_type=F32)
        mu = jnp.mean(r, axis=-1, keepdims=True)
        rc = r - mu
        var = jnp.mean(rc * rc, axis=-1, keepdims=True)
        out_ref[sl, :] = rc * lax.rsqrt(var + LN_EPS) * lng_ref[...] + lnb_ref[...]


def _merge(o_gla, o_hgrn, p2, x2d, w_gla, w_hgrn, w_out, gain_a, gain_h, ln_g, ln_b):
    n = x2d.shape[0]
    tm = MERGE_TM
    wide = lambda name: pl.BlockSpec((tm, D_MODEL), lambda i, name=name: (i, COL[name] // D_MODEL))
    row = lambda: pl.BlockSpec((tm, D_MODEL), lambda i: (i, 0))
    full = lambda shape: pl.BlockSpec(shape, lambda i: (0,) * len(shape),
                                      pipeline_mode=pl.Buffered(1))
    return pl.pallas_call(
        _merge_kernel,
        grid=(n // tm,),
        in_specs=[row(), row(), wide("a_gate"), wide("h_gate"), wide("m_gla"), wide("m_hgrn"), row(),
                  full((GLA_V, D_MODEL)), full((HG_V, D_MODEL)), full((D_MODEL, D_MODEL)),
                  full((1, GLA_DV)), full((1, HGRN_DV)), full((1, D_MODEL)), full((1, D_MODEL))],
        out_specs=row(),
        out_shape=jax.ShapeDtypeStruct((n, D_MODEL), F32),
        compiler_params=pltpu.CompilerParams(
            dimension_semantics=("arbitrary",), vmem_limit_bytes=VMEM_LIMIT),
        name="merge",
    )(o_gla, o_hgrn, p2, p2, p2, p2, x2d, w_gla, w_hgrn, w_out, gain_a, gain_h, ln_g, ln_b)


def _gate_params(up_f, up_b, bias_f, bias_b):
    rank = up_f.shape[0]
    up = jnp.zeros((LANES, 2 * GLA_QK), F32)
    up = up.at[:rank, :GLA_QK].set(up_f).at[rank:2 * rank, GLA_QK:].set(up_b)
    bias = jnp.concatenate([bias_f, bias_b])[None, :]
    return up.astype(BF16), bias


def kernel(x, w_in, gla_gk_up_f, gla_gk_bias_f, gla_gk_up_b, gla_gk_bias_b, gla_norm_g,
           hgrn_lb_logits_f, hgrn_lb_logits_b, hgrn_norm_g, w_branch_gla, w_branch_hgrn,
           w_out, ln_g, ln_b):
    assert w_in.shape[0] == DEPTH == 1
    bsz, seq, d = x.shape
    assert d == D_MODEL and seq % SCAN_TILE == 0
    assert SCAN_TILE % CUMSUM_ROWS == 0 and CUMSUM_ROWS % CHUNK == 0 and SCAN_TILE % (2 * CHUNK) == 0
    x2d = x.astype(F32).reshape(bsz * seq, d)

    w_t = jnp.swapaxes(w_in, 1, 2).reshape(w_in.shape[2], w_in.shape[1]).astype(F32)
    w_packed, w_lr = _pack_w(w_t)
    up, gbias = _gate_params(gla_gk_up_f[0].astype(F32), gla_gk_up_b[0].astype(F32),
                             gla_gk_bias_f[0].astype(F32), gla_gk_bias_b[0].astype(F32))
    p2 = _project(x2d, w_packed, w_lr, up, gbias,
                  hgrn_lb_logits_f.astype(F32), hgrn_lb_logits_b.astype(F32))
    p3 = p2.reshape(bsz, seq, PROJ_COLS)

    o_gla = _gla_scan(p3)
    o_hgrn = _hgrn_scan(p3)

    out = _merge(
        o_gla.reshape(bsz * seq, GLA_V), o_hgrn.reshape(bsz * seq, HG_V), p2, x2d,
        w_branch_gla[0].astype(BF16), w_branch_hgrn[0].astype(BF16), w_out[0].astype(BF16),
        gla_norm_g[0].astype(F32).reshape(1, GLA_DV), hgrn_norm_g[0].astype(F32).reshape(1, HGRN_DV),
        ln_g[0].astype(F32).reshape(1, D_MODEL), ln_b[0].astype(F32).reshape(1, D_MODEL))
    return out.reshape(bsz, seq, d).astype(x.dtype)
```

```python
import functools

import jax
import jax.numpy as jnp
import numpy as np
from jax import lax
from jax.experimental import pallas as pl
from jax.experimental.pallas import tpu as pltpu

F32 = jnp.float32
BF16 = jnp.bfloat16

D_MODEL = 1024
DEPTH = 1
GLA_HEADS = 4
GLA_DK = 128
GLA_DV = 256
GLA_GATE_RANK = 16
GLA_GATE_NORMALIZER = 16.0
HGRN_HEADS = 8
HGRN_DK = 128
HGRN_DV = 128
GLA_QK = GLA_HEADS * GLA_DK
GLA_V = GLA_HEADS * GLA_DV
HG_K = HGRN_HEADS * HGRN_DK
HG_V = HGRN_HEADS * HGRN_DV
RMS_EPS = 1e-6
LN_EPS = 1e-5
DEEPNORM_ALPHA = (2.0 * DEPTH) ** 0.25
LOG2E = 1.4426950408889634

LANES = 128

_ORIG_SIZES = (GLA_QK, GLA_QK, GLA_V, GLA_V, GLA_GATE_RANK, GLA_GATE_RANK,
               HG_K, HG_K, HG_K, HG_V, HG_V, D_MODEL, D_MODEL)
_ORIG_NAMES = ("a_q", "a_k", "a_v", "a_gate", "lr_f", "lr_b",
               "h_q", "h_ff", "h_fb", "h_i", "h_gate", "m_gla", "m_hgrn")
_ORIG_START = dict(zip(_ORIG_NAMES, np.cumsum((0,) + _ORIG_SIZES[:-1]).tolist()))
_SIZE = dict(zip(_ORIG_NAMES, _ORIG_SIZES))
_LR_LO = _ORIG_START["lr_f"]
_LR_HI = _ORIG_START["lr_b"] + _SIZE["lr_b"]

_OUT_ORDER = ("a_q", "a_k", "a_v", "a_gate", "g_f", "g_b",
              "h_q", "h_ff", "h_fb", "h_i", "h_gate", "m_gla", "m_hgrn")
_OUT_SIZE = dict(_SIZE, g_f=GLA_QK, g_b=GLA_QK)
COL = dict(zip(_OUT_ORDER, np.cumsum([0] + [_OUT_SIZE[nm] for nm in _OUT_ORDER[:-1]]).tolist()))
PROJ_COLS = sum(_OUT_SIZE[nm] for nm in _OUT_ORDER)

CHUNK = 64
SCAN_TILE = 512
CUMSUM_ROWS = 128
PHASE1_UNROLL = 4
PHASE2_UNROLL = 32
PROJ_TM = 512
PROJ_SUBN = 512
WPACK_TN = 1024
MERGE_TM = 1024
MERGE_SUB = 256
VMEM_LIMIT = 60 * 1024 * 1024


def _sigmoid(x):
    return 1.0 / (1.0 + jnp.exp2(x * (-LOG2E)))


def _lower_bound(logits, layer):
    m = jnp.max(logits, axis=0, keepdims=True)
    e = jnp.exp(logits - m)
    return jnp.sum(e[:layer + 1], axis=0, keepdims=True) / jnp.sum(e, axis=0, keepdims=True)


def _pack_w_kernel(a_ref, b_ref, o_ref, lr_ref):
    j = pl.program_id(0)
    gap = _LR_HI - _LR_LO
    first_shifted = _LR_LO // WPACK_TN

    @pl.when(j < first_shifted)
    def _():
        o_ref[...] = a_ref[...].T.astype(BF16)

    @pl.when(j == first_shifted - 1)
    def _():
        padded = jnp.concatenate([b_ref[...], jnp.zeros((LANES - gap, b_ref.shape[1]), F32)], axis=0)
        lr_ref[...] = padded.T.astype(BF16)

    @pl.when(j >= first_shifted)
    def _():
        rows = jnp.concatenate([a_ref[gap:, :], b_ref[...]], axis=0)
        o_ref[...] = rows.T.astype(BF16)


def _pack_w(w_t):
    d, gap = w_t.shape[1], _LR_HI - _LR_LO
    width = w_t.shape[0] - gap
    assert _LR_LO % WPACK_TN == 0 and _LR_LO > 0 and width % WPACK_TN == 0 and WPACK_TN % gap == 0
    return pl.pallas_call(
        _pack_w_kernel,
        grid=(width // WPACK_TN,),
        in_specs=[pl.BlockSpec((WPACK_TN, d), lambda j: (j, 0)),
                  pl.BlockSpec((gap, d), lambda j: ((j + 1) * (WPACK_TN // gap), 0))],
        out_specs=[pl.BlockSpec((d, WPACK_TN), lambda j: (0, j)),
                   pl.BlockSpec((d, LANES), lambda j: (0, 0))],
        out_shape=[jax.ShapeDtypeStruct((d, width), BF16), jax.ShapeDtypeStruct((d, LANES), BF16)],
        compiler_params=pltpu.CompilerParams(
            dimension_semantics=("arbitrary",), vmem_limit_bytes=VMEM_LIMIT),
        name="pack_w",
    )(w_t, w_t)


def _proj_kernel(x_ref, w_ref, wlr_ref, up_ref, gbias_ref, lf_ref, lb_ref, o_ref):
    xb = x_ref[...].astype(BF16)
    mm = lambda w: jnp.dot(xb, w, preferred_element_type=F32)
    silu = lambda a, cols: a * _sigmoid(a)
    ident = lambda a, cols: a
    squash = lambda a, cols: _sigmoid(a)
    scaled = lambda fn, c: (lambda a, cols: fn(a, cols) * c)

    def log2_forget(logits_ref):
        lb = _lower_bound(logits_ref[...], 0)
        return lambda h, cols: jnp.log2(lb[:, cols] + (1.0 - lb[:, cols]) * _sigmoid(h))

    def w_columns(name):
        lo = _ORIG_START[name] - (_LR_HI - _LR_LO if _ORIG_START[name] >= _LR_HI else 0)
        return lambda cols: mm(w_ref[:, lo + cols.start:lo + cols.stop])

    def emit(name, fn, product=None):
        product = product or w_columns(name)
        for off in range(0, _OUT_SIZE[name], PROJ_SUBN):
            cols = slice(off, off + PROJ_SUBN)
            dst = slice(COL[name] + off, COL[name] + off + PROJ_SUBN)
            o_ref[:, dst] = fn(product(cols), cols).astype(BF16)

    def log_sigmoid_gate(z, cols):
        log_sig = jnp.minimum(z, 0.0) - jnp.log2(1.0 + jnp.exp2(jnp.abs(z) * (-LOG2E))) * (1.0 / LOG2E)
        return log_sig * (LOG2E / GLA_GATE_NORMALIZER)

    lr = mm(wlr_ref[...]).astype(BF16)
    for d, name in enumerate(("g_f", "g_b")):
        base = d * GLA_QK
        emit(name, log_sigmoid_gate, product=lambda cols, base=base: (
            jnp.dot(lr, up_ref[:, base + cols.start:base + cols.stop], preferred_element_type=F32)
            + gbias_ref[:, base + cols.start:base + cols.stop]))
    emit("a_gate", silu)
    emit("h_q", scaled(silu, HGRN_DK ** -0.5))
    emit("a_q", scaled(ident, GLA_DK ** -0.5))
    emit("h_ff", log2_forget(lf_ref))
    emit("a_k", ident)
    emit("h_fb", log2_forget(lb_ref))
    emit("h_gate", silu)
    emit("a_v", ident)
    emit("m_gla", squash)
    emit("m_hgrn", squash)
    emit("h_i", ident)


def _project(x2d, w_packed, w_lr, up, gbias, logits_f, logits_b):
    n = x2d.shape[0]
    assert n % PROJ_TM == 0
    resident = lambda a: pl.BlockSpec(a.shape, lambda i: (0,) * a.ndim,
                                      pipeline_mode=pl.Buffered(1))
    consts = (w_packed, w_lr, up, gbias, logits_f, logits_b)
    return pl.pallas_call(
        _proj_kernel,
        grid=(n // PROJ_TM,),
        in_specs=[pl.BlockSpec((PROJ_TM, D_MODEL), lambda i: (i, 0))] + [resident(a) for a in consts],
        out_specs=pl.BlockSpec((PROJ_TM, PROJ_COLS), lambda i: (i, 0)),
        out_shape=jax.ShapeDtypeStruct((n, PROJ_COLS), BF16),
        compiler_params=pltpu.CompilerParams(
            dimension_semantics=("arbitrary",), vmem_limit_bytes=VMEM_LIMIT),
        name="proj",
    )(x2d, *consts)


def _dot_nt(a, b):
    return lax.dot_general(a, b, (((1,), (1,)), ((), ())), preferred_element_type=F32)


def _dot_tn(a, b):
    return lax.dot_general(a, b, (((0,), (0,)), ((), ())), preferred_element_type=F32)


def _block_diag_cumsum_mat(rows, c):
    row = lax.broadcasted_iota(jnp.int32, (rows, rows), 0)
    col = lax.broadcasted_iota(jnp.int32, (rows, rows), 1)
    return jnp.where(((row // c) == (col // c)) & (col <= row), 1.0, 0.0).astype(BF16)


def _scan_core(q_ref, k_of, g_refs, v_ref, o_ref, scratch, seq, dk, dv):
    qi_ref, upd_ref, dec_ref, oacc_ref, cum_ref, st_ref = scratch
    c = CHUNK
    pair = 2 * c
    tile = SCAN_TILE
    pairs_per_tile = tile // pair
    n_tiles = seq // tile
    n = seq // pair
    mid = c // 2
    row = lax.broadcasted_iota(jnp.int32, (pair, pair), 0)
    col = lax.broadcasted_iota(jnp.int32, (pair, pair), 1)
    masks = (col <= row, col >= row)
    cum_mat = _block_diag_cumsum_mat(CUMSUM_ROWS, c)

    def tile_rows(t):
        return pl.ds(pl.multiple_of(t * tile, tile), tile)

    def stage(t):
        g_pair = jnp.concatenate([g_refs[0][0, tile_rows(t), :], g_refs[1][0, tile_rows(t), :]], axis=1)
        for r in range(0, tile, CUMSUM_ROWS):
            cum_ref[r:r + CUMSUM_ROWS, :] = jnp.dot(cum_mat, g_pair[r:r + CUMSUM_ROWS],
                                                    preferred_element_type=F32)

    stage(0)

    def chunk_terms(d, q, k, g, cum):
        if d == 0:
            b_mid = cum[mid:mid + 1, :]
            total = cum[c - 1:c, :]
            rel, lead, trail = cum - b_mid, b_mid, total - b_mid
        else:
            e = cum - g
            e_mid = e[mid:mid + 1, :]
            total = cum[c - 1:c, :]
            rel, lead, trail = e_mid - e, total - e_mid, e_mid
        return q * jnp.exp2(rel), k * jnp.exp2(-rel), lead, trail, total

    def phase1(i, carry):
        rows = tile_rows(i)
        q = q_ref[0, rows, :].astype(F32)
        gs = (g_refs[0][0, rows, :].astype(F32), g_refs[1][0, rows, :].astype(F32))
        ks = (k_of(0, gs[0], rows), k_of(1, gs[1], rows))
        cum = cum_ref[...]
        v = v_ref[0, rows, :]
        bf = lambda a: a.astype(BF16)

        def scores_of(p):
            pair_rows = pl.ds(pl.multiple_of(i * tile + p * pair, pair), pair)
            sa, sb = slice(p * pair, p * pair + c), slice(p * pair + c, (p + 1) * pair)
            acc, kp_pair = None, []
            for d in range(2):
                lanes = slice(d * dk, (d + 1) * dk)
                qa, ka, lead_a, trail_a, tot_a = chunk_terms(d, q[sa], ks[d][sa], gs[d][sa], cum[sa, lanes])
                qb, kb, lead_b, trail_b, tot_b = chunk_terms(d, q[sb], ks[d][sb], gs[d][sb], cum[sb, lanes])
                if d == 0:
                    keys_a = jnp.concatenate([bf(ka), bf(kb)], axis=0)
                    keys_b = jnp.concatenate([bf(ka * jnp.exp2(trail_a + lead_b)), bf(kb)], axis=0)
                    qi = jnp.concatenate([qa * jnp.exp2(lead_a), qb * jnp.exp2(lead_b + tot_a)], axis=0)
                    kp = jnp.concatenate([ka * jnp.exp2(trail_a + tot_b), kb * jnp.exp2(trail_b)], axis=0)
                else:
                    keys_a = jnp.concatenate([bf(ka), bf(kb * jnp.exp2(trail_b + lead_a))], axis=0)
                    keys_b = jnp.concatenate([bf(ka), bf(kb)], axis=0)
                    qi = jnp.concatenate([qa * jnp.exp2(lead_a + tot_b), qb * jnp.exp2(lead_b)], axis=0)
                    kp = jnp.concatenate([ka * jnp.exp2(trail_a), kb * jnp.exp2(trail_b + tot_a)], axis=0)
                scores = jnp.concatenate([_dot_nt(bf(qa), keys_a), _dot_nt(bf(qb), keys_b)], axis=0)
                scores = jnp.where(masks[d], scores, 0.0)
                acc = scores if acc is None else acc + scores
                lane_decay = jnp.broadcast_to(jnp.exp2(tot_a + tot_b), (LANES, dk)).T
                if d == 0:
                    qi_f, decay_f = bf(qi), lane_decay
                else:
                    qi_ref[pair_rows, :] = bf(qi)
                    dec_ref[i * pairs_per_tile + p] = lane_decay
                kp_pair.append(bf(kp))
            return bf(acc), jnp.concatenate(kp_pair, axis=1), qi_f, decay_f

        def increment(p, kp):
            upd = _dot_tn(kp, v[p * pair:(p + 1) * pair])
            upd_ref[i * pairs_per_tile + p] = upd[dk:]
            return upd[:dk]

        def output(p, scores, qi_f, decay_f, upd_f):
            pair_rows = pl.ds(pl.multiple_of(i * tile + p * pair, pair), pair)
            st = st_ref[...]
            oacc_ref[pair_rows, :] = jnp.dot(
                jnp.concatenate([scores, qi_f], axis=1),
                jnp.concatenate([v[p * pair:(p + 1) * pair], bf(st)], axis=0),
                preferred_element_type=F32)
            st_ref[...] = st * jnp.concatenate([decay_f] * (dv // LANES), axis=1) + upd_f

        done, upd_f = {}, {}
        for s in range(pairs_per_tile + 2):
            if s < pairs_per_tile:
                done[s] = scores_of(s)
            if 1 <= s <= pairs_per_tile:
                upd_f[s - 1] = increment(s - 1, done[s - 1][1])
            if s == pairs_per_tile - 1:
                stage(jnp.minimum(i + 1, n_tiles - 1))
            if s >= 2:
                scores, _, qi_f, decay_f = done[s - 2]
                output(s - 2, scores, qi_f, decay_f, upd_f[s - 2])
        return carry

    st_ref[...] = jnp.zeros_like(st_ref)
    lax.fori_loop(0, n_tiles, phase1, 0, unroll=PHASE1_UNROLL)

    def backward(j, st):
        ci = n - 1 - j
        rows = pl.ds(pl.multiple_of(ci * pair, pair), pair)
        o = oacc_ref[rows, :] + jnp.dot(qi_ref[rows, :], st.astype(BF16), preferred_element_type=F32)
        o_ref[0, rows, :] = o.astype(o_ref.dtype)
        return st * jnp.concatenate([dec_ref[ci]] * (dv // LANES), axis=1) + upd_ref[ci]

    lax.fori_loop(0, n, backward, jnp.zeros((dk, dv), F32), unroll=PHASE2_UNROLL)


def _scan_scratch(seq, dk, dv):
    n = seq // (2 * CHUNK)
    return [pltpu.VMEM((seq, dk), BF16),
            pltpu.VMEM((n, dk, dv), F32),
            pltpu.VMEM((n, dk, LANES), F32),
            pltpu.VMEM((seq, dv), F32),
            pltpu.VMEM((SCAN_TILE, 2 * dk), F32),
            pltpu.VMEM((dk, dv), F32)]


def _scan_call(kernel_fn, name, p3, specs, heads, dk, dv):
    bsz, seq, _ = p3.shape
    return pl.pallas_call(
        functools.partial(kernel_fn, seq=seq),
        grid=(bsz, heads),
        in_specs=specs,
        out_specs=pl.BlockSpec((1, seq, dv), lambda b, h: (b, 0, h)),
        out_shape=jax.ShapeDtypeStruct((bsz, seq, heads * dv), BF16),
        scratch_shapes=_scan_scratch(seq, dk, dv),
        compiler_params=pltpu.CompilerParams(
            dimension_semantics=("arbitrary", "arbitrary"), vmem_limit_bytes=VMEM_LIMIT),
        name=name,
    )(*([p3] * len(specs)))


def _head_block(seq, name, width):
    first = COL[name] // width
    return pl.BlockSpec((1, seq, width), lambda b, h: (b, 0, first + h))


def _gla_kernel(q_ref, k_ref, gf_ref, gb_ref, v_ref, o_ref, *scratch, seq):
    k_of = lambda d, g, rows: k_ref[0, rows, :].astype(F32)
    _scan_core(q_ref, k_of, (gf_ref, gb_ref), v_ref, o_ref, scratch, seq, GLA_DK, GLA_DV)


def _gla_scan(p3):
    seq = p3.shape[1]
    specs = [_head_block(seq, "a_q", GLA_DK), _head_block(seq, "a_k", GLA_DK),
             _head_block(seq, "g_f", GLA_DK), _head_block(seq, "g_b", GLA_DK),
             _head_block(seq, "a_v", GLA_DV)]
    return _scan_call(_gla_kernel, "gla_scan", p3, specs, GLA_HEADS, GLA_DK, GLA_DV)


def _hgrn_kernel(q_ref, gf_ref, gb_ref, v_ref, o_ref, *scratch, seq):
    k_of = lambda d, g, rows: 1.0 - jnp.exp2(g)
    _scan_core(q_ref, k_of, (gf_ref, gb_ref), v_ref, o_ref, scratch, seq, HGRN_DK, HGRN_DV)


def _hgrn_scan(p3):
    seq = p3.shape[1]
    specs = [_head_block(seq, "h_q", HGRN_DK), _head_block(seq, "h_ff", HGRN_DK),
             _head_block(seq, "h_fb", HGRN_DK), _head_block(seq, "h_i", HGRN_DV)]
    return _scan_call(_hgrn_kernel, "hgrn_scan", p3, specs, HGRN_HEADS, HGRN_DK, HGRN_DV)


def _head_rmsnorm_gate(o, gain, gate, heads, dv):
    parts = []
    for h in range(heads):
        seg = o[:, h * dv:(h + 1) * dv]
        ms = jnp.mean(seg * seg, axis=-1, keepdims=True)
        parts.append(seg * lax.rsqrt(ms + RMS_EPS) * gain)
    return (jnp.concatenate(parts, axis=1) * gate).astype(BF16)


def _merge_kernel(oa_ref, oh_ref, ag_ref, hg_ref, mg_ref, mh_ref, x_ref, wa_ref, wh_ref, wo_ref,
                  ga_ref, gh_ref, lng_ref, lnb_ref, out_ref):
    subs = [slice(r, r + MERGE_SUB) for r in range(0, out_ref.shape[0], MERGE_SUB)]
    mixed = []
    for sl in subs:
        na = _head_rmsnorm_gate(oa_ref[sl, :].astype(F32), ga_ref[...], ag_ref[sl, :].astype(F32),
                                GLA_HEADS, GLA_DV)
        nh = _head_rmsnorm_gate(oh_ref[sl, :].astype(F32), gh_ref[...], hg_ref[sl, :].astype(F32),
                                HGRN_HEADS, HGRN_DV)
        y_gla = jnp.dot(na, wa_ref[...], preferred_element_type=F32)
        y_hgrn = jnp.dot(nh, wh_ref[...], preferred_element_type=F32)
        mixed.append((mg_ref[sl, :].astype(F32) * y_gla
                      + mh_ref[sl, :].astype(F32) * y_hgrn).astype(BF16))
    for sl, y in zip(subs, mixed):
        r = DEEPNORM_ALPHA * x_ref[sl, :] + jnp.dot(y, wo_ref[...], preferred_element_type=F32)
        mu = jnp.mean(r, axis=-1, keepdims=True)
        rc = r - mu
        var = jnp.mean(rc * rc, axis=-1, keepdims=True)
        out_ref[sl, :] = rc * lax.rsqrt(var + LN_EPS) * lng_ref[...] + lnb_ref[...]


def _merge(o_gla, o_hgrn, p2, x2d, w_gla, w_hgrn, w_out, gain_a, gain_h, ln_g, ln_b):
    n = x2d.shape[0]
    tm = MERGE_TM
    wide = lambda name: pl.BlockSpec((tm, D_MODEL), lambda i, name=name: (i, COL[name] // D_MODEL))
    row = lambda: pl.BlockSpec((tm, D_MODEL), lambda i: (i, 0))
    full = lambda shape: pl.BlockSpec(shape, lambda i: (0,) * len(shape),
                                      pipeline_mode=pl.Buffered(1))
    return pl.pallas_call(
        _merge_kernel,
        grid=(n // tm,),
        in_specs=[row(), row(), wide("a_gate"), wide("h_gate"), wide("m_gla"), wide("m_hgrn"), row(),
                  full((GLA_V, D_MODEL)), full((HG_V, D_MODEL)), full((D_MODEL, D_MODEL)),
                  full((1, GLA_DV)), full((1, HGRN_DV)), full((1, D_MODEL)), full((1, D_MODEL))],
        out_specs=row(),
        out_shape=jax.ShapeDtypeStruct((n, D_MODEL), F32),
        compiler_params=pltpu.CompilerParams(
            dimension_semantics=("arbitrary",), vmem_limit_bytes=VMEM_LIMIT),
        name="merge",
    )(o_gla, o_hgrn, p2, p2, p2, p2, x2d, w_gla, w_hgrn, w_out, gain_a, gain_h, ln_g, ln_b)


def _gate_params(up_f, up_b, bias_f, bias_b):
    rank = up_f.shape[0]
    up = jnp.zeros((LANES, 2 * GLA_QK), F32)
    up = up.at[:rank, :GLA_QK].set(up_f).at[rank:2 * rank, GLA_QK:].set(up_b)
    bias = jnp.concatenate([bias_f, bias_b])[None, :]
    return up.astype(BF16), bias


def kernel(x, w_in, gla_gk_up_f, gla_gk_bias_f, gla_gk_up_b, gla_gk_bias_b, gla_norm_g,
           hgrn_lb_logits_f, hgrn_lb_logits_b, hgrn_norm_g, w_branch_gla, w_branch_hgrn,
           w_out, ln_g, ln_b):
    assert w_in.shape[0] == DEPTH == 1
    bsz, seq, d = x.shape
    assert d == D_MODEL and seq % SCAN_TILE == 0 and SCAN_TILE % CUMSUM_ROWS == 0
    x2d = x.astype(F32).reshape(bsz * seq, d)

    w_t = jnp.swapaxes(w_in, 1, 2).reshape(w_in.shape[2], w_in.shape[1]).astype(F32)
    w_packed, w_lr = _pack_w(w_t)
    up, gbias = _gate_params(gla_gk_up_f[0].astype(F32), gla_gk_up_b[0].astype(F32),
                             gla_gk_bias_f[0].astype(F32), gla_gk_bias_b[0].astype(F32))
    p2 = _project(x2d, w_packed, w_lr, up, gbias,
                  hgrn_lb_logits_f.astype(F32), hgrn_lb_logits_b.astype(F32))
    p3 = p2.reshape(bsz, seq, PROJ_COLS)

    o_gla = _gla_scan(p3)
    o_hgrn = _hgrn_scan(p3)

    out = _merge(
        o_gla.reshape(bsz * seq, GLA_V), o_hgrn.reshape(bsz * seq, HG_V), p2, x2d,
        w_branch_gla[0].astype(BF16), w_branch_hgrn[0].astype(BF16), w_out[0].astype(BF16),
        gla_norm_g[0].astype(F32).reshape(1, GLA_DV), hgrn_norm_g[0].astype(F32).reshape(1, HGRN_DV),
        ln_g[0].astype(F32).reshape(1, D_MODEL), ln_b[0].astype(F32).reshape(1, D_MODEL))
    return out.reshape(bsz, seq, d).astype(x.dtype)
```

```python
import functools

import jax
import jax.numpy as jnp
import numpy as np
from jax import lax
from jax.experimental import pallas as pl
from jax.experimental.pallas import tpu as pltpu

F32 = jnp.float32
BF16 = jnp.bfloat16

D_MODEL = 1024
DEPTH = 1
GLA_HEADS = 4
GLA_DK = 128
GLA_DV = 256
GLA_GATE_RANK = 16
GLA_GATE_NORMALIZER = 16.0
HGRN_HEADS = 8
HGRN_DK = 128
HGRN_DV = 128
GLA_QK = GLA_HEADS * GLA_DK
GLA_V = GLA_HEADS * GLA_DV
HG_K = HGRN_HEADS * HGRN_DK
HG_V = HGRN_HEADS * HGRN_DV
RMS_EPS = 1e-6
LN_EPS = 1e-5
DEEPNORM_ALPHA = (2.0 * DEPTH) ** 0.25
LOG2E = 1.4426950408889634

LANES = 128

_ORIG_SIZES = (GLA_QK, GLA_QK, GLA_V, GLA_V, GLA_GATE_RANK, GLA_GATE_RANK,
               HG_K, HG_K, HG_K, HG_V, HG_V, D_MODEL, D_MODEL)
_ORIG_NAMES = ("a_q", "a_k", "a_v", "a_gate", "lr_f", "lr_b",
               "h_q", "h_ff", "h_fb", "h_i", "h_gate", "m_gla", "m_hgrn")
_ORIG_START = dict(zip(_ORIG_NAMES, np.cumsum((0,) + _ORIG_SIZES[:-1]).tolist()))
_SIZE = dict(zip(_ORIG_NAMES, _ORIG_SIZES))
_LR_LO = _ORIG_START["lr_f"]
_LR_HI = _ORIG_START["lr_b"] + _SIZE["lr_b"]

_OUT_ORDER = ("a_q", "a_k", "a_v", "a_gate", "g_f", "g_b",
              "h_q", "h_ff", "h_fb", "h_i", "h_gate", "m_gla", "m_hgrn")
_OUT_SIZE = dict(_SIZE, g_f=GLA_QK, g_b=GLA_QK)
COL = dict(zip(_OUT_ORDER, np.cumsum([0] + [_OUT_SIZE[nm] for nm in _OUT_ORDER[:-1]]).tolist()))
PROJ_COLS = sum(_OUT_SIZE[nm] for nm in _OUT_ORDER)

CHUNK = 64
SCAN_TILE = 512
CUMSUM_ROWS = 128
PHASE1_UNROLL = 4
PHASE2_UNROLL = 32
PROJ_TM = 512
PROJ_SUBN = 512
WPACK_TN = 1024
MERGE_TM = 1024
MERGE_SUB = 256
VMEM_LIMIT = 60 * 1024 * 1024


def _sigmoid(x):
    return 1.0 / (1.0 + jnp.exp2(x * (-LOG2E)))


def _lower_bound(logits, layer):
    m = jnp.max(logits, axis=0, keepdims=True)
    e = jnp.exp(logits - m)
    return jnp.sum(e[:layer + 1], axis=0, keepdims=True) / jnp.sum(e, axis=0, keepdims=True)


def _pack_w_kernel(a_ref, b_ref, o_ref, lr_ref):
    j = pl.program_id(0)
    gap = _LR_HI - _LR_LO
    first_shifted = _LR_LO // WPACK_TN

    @pl.when(j < first_shifted)
    def _():
        o_ref[...] = a_ref[...].T.astype(BF16)

    @pl.when(j == first_shifted - 1)
    def _():
        padded = jnp.concatenate([b_ref[...], jnp.zeros((LANES - gap, b_ref.shape[1]), F32)], axis=0)
        lr_ref[...] = padded.T.astype(BF16)

    @pl.when(j >= first_shifted)
    def _():
        rows = jnp.concatenate([a_ref[gap:, :], b_ref[...]], axis=0)
        o_ref[...] = rows.T.astype(BF16)


def _pack_w(w_t):
    d, gap = w_t.shape[1], _LR_HI - _LR_LO
    width = w_t.shape[0] - gap
    assert _LR_LO % WPACK_TN == 0 and _LR_LO > 0 and width % WPACK_TN == 0 and WPACK_TN % gap == 0
    return pl.pallas_call(
        _pack_w_kernel,
        grid=(width // WPACK_TN,),
        in_specs=[pl.BlockSpec((WPACK_TN, d), lambda j: (j, 0)),
                  pl.BlockSpec((gap, d), lambda j: ((j + 1) * (WPACK_TN // gap), 0))],
        out_specs=[pl.BlockSpec((d, WPACK_TN), lambda j: (0, j)),
                   pl.BlockSpec((d, LANES), lambda j: (0, 0))],
        out_shape=[jax.ShapeDtypeStruct((d, width), BF16), jax.ShapeDtypeStruct((d, LANES), BF16)],
        compiler_params=pltpu.CompilerParams(
            dimension_semantics=("arbitrary",), vmem_limit_bytes=VMEM_LIMIT),
        name="pack_w",
    )(w_t, w_t)


def _proj_kernel(x_ref, w_ref, wlr_ref, up_ref, gbias_ref, lf_ref, lb_ref, o_ref):
    xb = x_ref[...].astype(BF16)
    mm = lambda w: jnp.dot(xb, w, preferred_element_type=F32)
    silu = lambda a, cols: a * _sigmoid(a)
    ident = lambda a, cols: a
    squash = lambda a, cols: _sigmoid(a)
    scaled = lambda fn, c: (lambda a, cols: fn(a, cols) * c)

    def log2_forget(logits_ref):
        lb = _lower_bound(logits_ref[...], 0)
        return lambda h, cols: jnp.log2(lb[:, cols] + (1.0 - lb[:, cols]) * _sigmoid(h))

    def w_columns(name):
        lo = _ORIG_START[name] - (_LR_HI - _LR_LO if _ORIG_START[name] >= _LR_HI else 0)
        return lambda cols: mm(w_ref[:, lo + cols.start:lo + cols.stop])

    def emit(name, fn, product=None):
        product = product or w_columns(name)
        for off in range(0, _OUT_SIZE[name], PROJ_SUBN):
            cols = slice(off, off + PROJ_SUBN)
            dst = slice(COL[name] + off, COL[name] + off + PROJ_SUBN)
            o_ref[:, dst] = fn(product(cols), cols).astype(BF16)

    def log_sigmoid_gate(z, cols):
        log_sig = jnp.minimum(z, 0.0) - jnp.log2(1.0 + jnp.exp2(jnp.abs(z) * (-LOG2E))) * (1.0 / LOG2E)
        return log_sig * (LOG2E / GLA_GATE_NORMALIZER)

    lr = mm(wlr_ref[...]).astype(BF16)
    for d, name in enumerate(("g_f", "g_b")):
        base = d * GLA_QK
        emit(name, log_sigmoid_gate, product=lambda cols, base=base: (
            jnp.dot(lr, up_ref[:, base + cols.start:base + cols.stop], preferred_element_type=F32)
            + gbias_ref[:, base + cols.start:base + cols.stop]))
    emit("a_q", scaled(ident, GLA_DK ** -0.5))
    emit("a_gate", silu)
    emit("a_k", ident)
    emit("h_q", scaled(silu, HGRN_DK ** -0.5))
    emit("a_v", ident)
    emit("h_ff", log2_forget(lf_ref))
    emit("h_fb", log2_forget(lb_ref))
    emit("h_i", ident)
    emit("h_gate", silu)
    emit("m_gla", squash)
    emit("m_hgrn", squash)


def _project(x2d, w_packed, w_lr, up, gbias, logits_f, logits_b):
    n = x2d.shape[0]
    assert n % PROJ_TM == 0
    resident = lambda a: pl.BlockSpec(a.shape, lambda i: (0,) * a.ndim,
                                      pipeline_mode=pl.Buffered(1))
    consts = (w_packed, w_lr, up, gbias, logits_f, logits_b)
    return pl.pallas_call(
        _proj_kernel,
        grid=(n // PROJ_TM,),
        in_specs=[pl.BlockSpec((PROJ_TM, D_MODEL), lambda i: (i, 0))] + [resident(a) for a in consts],
        out_specs=pl.BlockSpec((PROJ_TM, PROJ_COLS), lambda i: (i, 0)),
        out_shape=jax.ShapeDtypeStruct((n, PROJ_COLS), BF16),
        compiler_params=pltpu.CompilerParams(
            dimension_semantics=("arbitrary",), vmem_limit_bytes=VMEM_LIMIT),
        name="proj",
    )(x2d, *consts)


def _dot_nt(a, b):
    return lax.dot_general(a, b, (((1,), (1,)), ((), ())), preferred_element_type=F32)


def _dot_tn(a, b):
    return lax.dot_general(a, b, (((0,), (0,)), ((), ())), preferred_element_type=F32)


def _block_diag_cumsum_mat(rows, c):
    row = lax.broadcasted_iota(jnp.int32, (rows, rows), 0)
    col = lax.broadcasted_iota(jnp.int32, (rows, rows), 1)
    return jnp.where(((row // c) == (col // c)) & (col <= row), 1.0, 0.0).astype(BF16)


def _scan_core(q_ref, k_of, g_refs, v_ref, o_ref, scratch, seq, dk, dv):
    qi_ref, upd_ref, dec_ref, oacc_ref, cum_ref, st_ref = scratch
    c = CHUNK
    pair = 2 * c
    tile = SCAN_TILE
    pairs_per_tile = tile // pair
    n_tiles = seq // tile
    n = seq // pair
    mid = c // 2
    row = lax.broadcasted_iota(jnp.int32, (pair, pair), 0)
    col = lax.broadcasted_iota(jnp.int32, (pair, pair), 1)
    masks = (col <= row, col >= row)
    cum_mat = _block_diag_cumsum_mat(CUMSUM_ROWS, c)

    def tile_rows(t):
        return pl.ds(pl.multiple_of(t * tile, tile), tile)

    def stage(t):
        g_pair = jnp.concatenate([g_refs[0][0, tile_rows(t), :], g_refs[1][0, tile_rows(t), :]], axis=1)
        for r in range(0, tile, CUMSUM_ROWS):
            cum_ref[r:r + CUMSUM_ROWS, :] = jnp.dot(cum_mat, g_pair[r:r + CUMSUM_ROWS],
                                                    preferred_element_type=F32)

    stage(0)

    def chunk_terms(d, q, k, g, cum):
        if d == 0:
            b_mid = cum[mid:mid + 1, :]
            total = cum[c - 1:c, :]
            rel, lead, trail = cum - b_mid, b_mid, total - b_mid
        else:
            e = cum - g
            e_mid = e[mid:mid + 1, :]
            total = cum[c - 1:c, :]
            rel, lead, trail = e_mid - e, total - e_mid, e_mid
        return q * jnp.exp2(rel), k * jnp.exp2(-rel), lead, trail, total

    def phase1(i, carry):
        rows = tile_rows(i)
        q = q_ref[0, rows, :].astype(F32)
        gs = (g_refs[0][0, rows, :].astype(F32), g_refs[1][0, rows, :].astype(F32))
        ks = (k_of(0, gs[0], rows), k_of(1, gs[1], rows))
        cum = cum_ref[...]
        v = v_ref[0, rows, :]
        bf = lambda a: a.astype(BF16)

        def scores_of(p):
            pair_rows = pl.ds(pl.multiple_of(i * tile + p * pair, pair), pair)
            sa, sb = slice(p * pair, p * pair + c), slice(p * pair + c, (p + 1) * pair)
            acc, kp_pair = None, []
            for d in range(2):
                lanes = slice(d * dk, (d + 1) * dk)
                qa, ka, lead_a, trail_a, tot_a = chunk_terms(d, q[sa], ks[d][sa], gs[d][sa], cum[sa, lanes])
                qb, kb, lead_b, trail_b, tot_b = chunk_terms(d, q[sb], ks[d][sb], gs[d][sb], cum[sb, lanes])
                if d == 0:
                    keys_a = jnp.concatenate([bf(ka), bf(kb)], axis=0)
                    keys_b = jnp.concatenate([bf(ka * jnp.exp2(trail_a + lead_b)), bf(kb)], axis=0)
                    qi = jnp.concatenate([qa * jnp.exp2(lead_a), qb * jnp.exp2(lead_b + tot_a)], axis=0)
                    kp = jnp.concatenate([ka * jnp.exp2(trail_a + tot_b), kb * jnp.exp2(trail_b)], axis=0)
                else:
                    keys_a = jnp.concatenate([bf(ka), bf(kb * jnp.exp2(trail_b + lead_a))], axis=0)
                    keys_b = jnp.concatenate([bf(ka), bf(kb)], axis=0)
                    qi = jnp.concatenate([qa * jnp.exp2(lead_a + tot_b), qb * jnp.exp2(lead_b)], axis=0)
                    kp = jnp.concatenate([ka * jnp.exp2(trail_a), kb * jnp.exp2(trail_b + tot_a)], axis=0)
                scores = jnp.concatenate([_dot_nt(bf(qa), keys_a), _dot_nt(bf(qb), keys_b)], axis=0)
                scores = jnp.where(masks[d], scores, 0.0)
                acc = scores if acc is None else acc + scores
                lane_decay = jnp.broadcast_to(jnp.exp2(tot_a + tot_b), (LANES, dk)).T
                if d == 0:
                    qi_f, decay_f = bf(qi), lane_decay
                else:
                    qi_ref[pair_rows, :] = bf(qi)
                    dec_ref[i * pairs_per_tile + p] = lane_decay
                kp_pair.append(bf(kp))
            return bf(acc), jnp.concatenate(kp_pair, axis=1), qi_f, decay_f

        def increment(p, kp):
            upd = _dot_tn(kp, v[p * pair:(p + 1) * pair])
            upd_ref[i * pairs_per_tile + p] = upd[dk:]
            return upd[:dk]

        def output(p, scores, qi_f, decay_f, upd_f):
            pair_rows = pl.ds(pl.multiple_of(i * tile + p * pair, pair), pair)
            st = st_ref[...]
            oacc_ref[pair_rows, :] = jnp.dot(
                jnp.concatenate([scores, qi_f], axis=1),
                jnp.concatenate([v[p * pair:(p + 1) * pair], bf(st)], axis=0),
                preferred_element_type=F32)
            st_ref[...] = st * jnp.concatenate([decay_f] * (dv // LANES), axis=1) + upd_f

        done, upd_f = {}, {}
        for s in range(pairs_per_tile + 2):
            if s < pairs_per_tile:
                done[s] = scores_of(s)
            if 1 <= s <= pairs_per_tile:
                upd_f[s - 1] = increment(s - 1, done[s - 1][1])
            if s == pairs_per_tile - 1:
                stage(jnp.minimum(i + 1, n_tiles - 1))
            if s >= 2:
                scores, _, qi_f, decay_f = done[s - 2]
                output(s - 2, scores, qi_f, decay_f, upd_f[s - 2])
        return carry

    st_ref[...] = jnp.zeros_like(st_ref)
    lax.fori_loop(0, n_tiles, phase1, 0, unroll=PHASE1_UNROLL)

    def backward(j, st):
        ci = n - 1 - j
        rows = pl.ds(pl.multiple_of(ci * pair, pair), pair)
        o = oacc_ref[rows, :] + jnp.dot(qi_ref[rows, :], st.astype(BF16), preferred_element_type=F32)
        o_ref[0, rows, :] = o.astype(o_ref.dtype)
        return st * jnp.concatenate([dec_ref[ci]] * (dv // LANES), axis=1) + upd_ref[ci]

    lax.fori_loop(0, n, backward, jnp.zeros((dk, dv), F32), unroll=PHASE2_UNROLL)


def _scan_scratch(seq, dk, dv):
    n = seq // (2 * CHUNK)
    return [pltpu.VMEM((seq, dk), BF16),
            pltpu.VMEM((n, dk, dv), F32),
            pltpu.VMEM((n, dk, LANES), F32),
            pltpu.VMEM((seq, dv), F32),
            pltpu.VMEM((SCAN_TILE, 2 * dk), F32),
            pltpu.VMEM((dk, dv), F32)]


def _scan_call(kernel_fn, name, p3, specs, heads, dk, dv):
    bsz, seq, _ = p3.shape
    return pl.pallas_call(
        functools.partial(kernel_fn, seq=seq),
        grid=(bsz, heads),
        in_specs=specs,
        out_specs=pl.BlockSpec((1, seq, dv), lambda b, h: (b, 0, h)),
        out_shape=jax.ShapeDtypeStruct((bsz, seq, heads * dv), BF16),
        scratch_shapes=_scan_scratch(seq, dk, dv),
        compiler_params=pltpu.CompilerParams(
            dimension_semantics=("arbitrary", "arbitrary"), vmem_limit_bytes=VMEM_LIMIT),
        name=name,
    )(*([p3] * len(specs)))


def _head_block(seq, name, width):
    first = COL[name] // width
    return pl.BlockSpec((1, seq, width), lambda b, h: (b, 0, first + h))


def _gla_kernel(q_ref, k_ref, gf_ref, gb_ref, v_ref, o_ref, *scratch, seq):
    k_of = lambda d, g, rows: k_ref[0, rows, :].astype(F32)
    _scan_core(q_ref, k_of, (gf_ref, gb_ref), v_ref, o_ref, scratch, seq, GLA_DK, GLA_DV)


def _gla_scan(p3):
    seq = p3.shape[1]
    specs = [_head_block(seq, "a_q", GLA_DK), _head_block(seq, "a_k", GLA_DK),
             _head_block(seq, "g_f", GLA_DK), _head_block(seq, "g_b", GLA_DK),
             _head_block(seq, "a_v", GLA_DV)]
    return _scan_call(_gla_kernel, "gla_scan", p3, specs, GLA_HEADS, GLA_DK, GLA_DV)


def _hgrn_kernel(q_ref, gf_ref, gb_ref, v_ref, o_ref, *scratch, seq):
    k_of = lambda d, g, rows: 1.0 - jnp.exp2(g)
    _scan_core(q_ref, k_of, (gf_ref, gb_ref), v_ref, o_ref, scratch, seq, HGRN_DK, HGRN_DV)


def _hgrn_scan(p3):
    seq = p3.shape[1]
    specs = [_head_block(seq, "h_q", HGRN_DK), _head_block(seq, "h_ff", HGRN_DK),
             _head_block(seq, "h_fb", HGRN_DK), _head_block(seq, "h_i", HGRN_DV)]
    return _scan_call(_hgrn_kernel, "hgrn_scan", p3, specs, HGRN_HEADS, HGRN_DK, HGRN_DV)


def _head_rmsnorm_gate(o, gain, gate, heads, dv):
    parts = []
    for h in range(heads):
        seg = o[:, h * dv:(h + 1) * dv]
        ms = jnp.mean(seg * seg, axis=-1, keepdims=True)
        parts.append(seg * lax.rsqrt(ms + RMS_EPS) * gain)
    return (jnp.concatenate(parts, axis=1) * gate).astype(BF16)


def _merge_kernel(oa_ref, oh_ref, ag_ref, hg_ref, mg_ref, mh_ref, x_ref, wa_ref, wh_ref, wo_ref,
                  ga_ref, gh_ref, lng_ref, lnb_ref, out_ref):
    subs = [slice(r, r + MERGE_SUB) for r in range(0, out_ref.shape[0], MERGE_SUB)]
    mixed = []
    for sl in subs:
        na = _head_rmsnorm_gate(oa_ref[sl, :].astype(F32), ga_ref[...], ag_ref[sl, :].astype(F32),
                                GLA_HEADS, GLA_DV)
        nh = _head_rmsnorm_gate(oh_ref[sl, :].astype(F32), gh_ref[...], hg_ref[sl, :].astype(F32),
                                HGRN_HEADS, HGRN_DV)
        y_gla = jnp.dot(na, wa_ref[...], preferred_element_type=F32)
        y_hgrn = jnp.dot(nh, wh_ref[...], preferred_element_type=F32)
        mixed.append((mg_ref[sl, :].astype(F32) * y_gla
                      + mh_ref[sl, :].astype(F32) * y_hgrn).astype(BF16))
    for sl, y in zip(subs, mixed):
        r = DEEPNORM_ALPHA * x_ref[sl, :] + jnp.dot(y, wo_ref[...], preferred_element_type=F32)
        mu = jnp.mean(r, axis=-1, keepdims=True)
        rc = r - mu
        var = jnp.mean(rc * rc, axis=-1, keepdims=True)
        out_ref[sl, :] = rc * lax.rsqrt(var + LN_EPS) * lng_ref[...] + lnb_ref[...]


def _merge(o_gla, o_hgrn, p2, x2d, w_gla, w_hgrn, w_out, gain_a, gain_h, ln_g, ln_b):
    n = x2d.shape[0]
    tm = MERGE_TM
    wide = lambda name: pl.BlockSpec((tm, D_MODEL), lambda i, name=name: (i, COL[name] // D_MODEL))
    row = lambda: pl.BlockSpec((tm, D_MODEL), lambda i: (i, 0))
    full = lambda shape: pl.BlockSpec(shape, lambda i: (0,) * len(shape),
                                      pipeline_mode=pl.Buffered(1))
    return pl.pallas_call(
        _merge_kernel,
        grid=(n // tm,),
        in_specs=[row(), row(), wide("a_gate"), wide("h_gate"), wide("m_gla"), wide("m_hgrn"), row(),
                  full((GLA_V, D_MODEL)), full((HG_V, D_MODEL)), full((D_MODEL, D_MODEL)),
                  full((1, GLA_DV)), full((1, HGRN_DV)), full((1, D_MODEL)), full((1, D_MODEL))],
        out_specs=row(),
        out_shape=jax.ShapeDtypeStruct((n, D_MODEL), F32),
        compiler_params=pltpu.CompilerParams(
            dimension_semantics=("arbitrary",), vmem_limit_bytes=VMEM_LIMIT),
        name="merge",
    )(o_gla, o_hgrn, p2, p2, p2, p2, x2d, w_gla, w_hgrn, w_out, gain_a, gain_h, ln_g, ln_b)


def _gate_params(up_f, up_b, bias_f, bias_b):
    rank = up_f.shape[0]
    up = jnp.zeros((LANES, 2 * GLA_QK), F32)
    up = up.at[:rank, :GLA_QK].set(up_f).at[rank:2 * rank, GLA_QK:].set(up_b)
    bias = jnp.concatenate([bias_f, bias_b])[None, :]
    return up.astype(BF16), bias


def kernel(x, w_in, gla_gk_up_f, gla_gk_bias_f, gla_gk_up_b, gla_gk_bias_b, gla_norm_g,
           hgrn_lb_logits_f, hgrn_lb_logits_b, hgrn_norm_g, w_branch_gla, w_branch_hgrn,
           w_out, ln_g, ln_b):
    assert w_in.shape[0] == DEPTH == 1
    bsz, seq, d = x.shape
    assert d == D_MODEL and seq % SCAN_TILE == 0 and SCAN_TILE % CUMSUM_ROWS == 0
    x2d = x.astype(F32).reshape(bsz * seq, d)

    w_t = jnp.swapaxes(w_in, 1, 2).reshape(w_in.shape[2], w_in.shape[1]).astype(F32)
    w_packed, w_lr = _pack_w(w_t)
    up, gbias = _gate_params(gla_gk_up_f[0].astype(F32), gla_gk_up_b[0].astype(F32),
                             gla_gk_bias_f[0].astype(F32), gla_gk_bias_b[0].astype(F32))
    p2 = _project(x2d, w_packed, w_lr, up, gbias,
                  hgrn_lb_logits_f.astype(F32), hgrn_lb_logits_b.astype(F32))
    p3 = p2.reshape(bsz, seq, PROJ_COLS)

    o_gla = _gla_scan(p3)
    o_hgrn = _hgrn_scan(p3)

    out = _merge(
        o_gla.reshape(bsz * seq, GLA_V), o_hgrn.reshape(bsz * seq, HG_V), p2, x2d,
        w_branch_gla[0].astype(BF16), w_branch_hgrn[0].astype(BF16), w_out[0].astype(BF16),
        gla_norm_g[0].astype(F32).reshape(1, GLA_DV), hgrn_norm_g[0].astype(F32).reshape(1, HGRN_DV),
        ln_g[0].astype(F32).reshape(1, D_MODEL), ln_b[0].astype(F32).reshape(1, D_MODEL))
    return out.reshape(bsz, seq, d).astype(x.dtype)
```

```python
import functools

import jax
import jax.numpy as jnp
import numpy as np
from jax import lax
from jax.experimental import pallas as pl
from jax.experimental.pallas import tpu as pltpu

F32 = jnp.float32
BF16 = jnp.bfloat16

D_MODEL = 1024
DEPTH = 1
GLA_HEADS = 4
GLA_DK = 128
GLA_DV = 256
GLA_GATE_RANK = 16
GLA_GATE_NORMALIZER = 16.0
HGRN_HEADS = 8
HGRN_DK = 128
HGRN_DV = 128
GLA_QK = GLA_HEADS * GLA_DK
GLA_V = GLA_HEADS * GLA_DV
HG_K = HGRN_HEADS * HGRN_DK
HG_V = HGRN_HEADS * HGRN_DV
RMS_EPS = 1e-6
LN_EPS = 1e-5
DEEPNORM_ALPHA = (2.0 * DEPTH) ** 0.25
LOG2E = 1.4426950408889634

LANES = 128

_ORIG_SIZES = (GLA_QK, GLA_QK, GLA_V, GLA_V, GLA_GATE_RANK, GLA_GATE_RANK,
               HG_K, HG_K, HG_K, HG_V, HG_V, D_MODEL, D_MODEL)
_ORIG_NAMES = ("a_q", "a_k", "a_v", "a_gate", "lr_f", "lr_b",
               "h_q", "h_ff", "h_fb", "h_i", "h_gate", "m_gla", "m_hgrn")
_ORIG_START = dict(zip(_ORIG_NAMES, np.cumsum((0,) + _ORIG_SIZES[:-1]).tolist()))
_SIZE = dict(zip(_ORIG_NAMES, _ORIG_SIZES))
_LR_LO = _ORIG_START["lr_f"]
_LR_HI = _ORIG_START["lr_b"] + _SIZE["lr_b"]

_OUT_ORDER = ("a_q", "a_k", "a_v", "a_gate", "g_f", "g_b",
              "h_q", "h_ff", "h_fb", "h_i", "h_gate", "m_gla", "m_hgrn")
_OUT_SIZE = dict(_SIZE, g_f=GLA_QK, g_b=GLA_QK)
COL = dict(zip(_OUT_ORDER, np.cumsum([0] + [_OUT_SIZE[nm] for nm in _OUT_ORDER[:-1]]).tolist()))
PROJ_COLS = sum(_OUT_SIZE[nm] for nm in _OUT_ORDER)

CHUNK = 64
GLA_TILING = (512, 4)
HGRN_TILING = (2048, 1)
CUMSUM_ROWS = 128
PHASE2_UNROLL = 32
PROJ_TM = 512
PROJ_SUBN = 512
WPACK_TN = 1024
MERGE_TM = 1024
MERGE_SUB = 256
VMEM_LIMIT = 60 * 1024 * 1024


def _sigmoid(x):
    return 1.0 / (1.0 + jnp.exp2(x * (-LOG2E)))


def _lower_bound(logits, layer):
    m = jnp.max(logits, axis=0, keepdims=True)
    e = jnp.exp(logits - m)
    return jnp.sum(e[:layer + 1], axis=0, keepdims=True) / jnp.sum(e, axis=0, keepdims=True)


def _pack_w_kernel(a_ref, b_ref, o_ref, lr_ref):
    j = pl.program_id(0)
    gap = _LR_HI - _LR_LO
    first_shifted = _LR_LO // WPACK_TN

    @pl.when(j < first_shifted)
    def _():
        o_ref[...] = a_ref[...].T.astype(BF16)

    @pl.when(j == first_shifted - 1)
    def _():
        padded = jnp.concatenate([b_ref[...], jnp.zeros((LANES - gap, b_ref.shape[1]), F32)], axis=0)
        lr_ref[...] = padded.T.astype(BF16)

    @pl.when(j >= first_shifted)
    def _():
        rows = jnp.concatenate([a_ref[gap:, :], b_ref[...]], axis=0)
        o_ref[...] = rows.T.astype(BF16)


def _pack_w(w_t):
    d, gap = w_t.shape[1], _LR_HI - _LR_LO
    width = w_t.shape[0] - gap
    assert _LR_LO % WPACK_TN == 0 and _LR_LO > 0 and width % WPACK_TN == 0 and WPACK_TN % gap == 0
    return pl.pallas_call(
        _pack_w_kernel,
        grid=(width // WPACK_TN,),
        in_specs=[pl.BlockSpec((WPACK_TN, d), lambda j: (j, 0)),
                  pl.BlockSpec((gap, d), lambda j: ((j + 1) * (WPACK_TN // gap), 0))],
        out_specs=[pl.BlockSpec((d, WPACK_TN), lambda j: (0, j)),
                   pl.BlockSpec((d, LANES), lambda j: (0, 0))],
        out_shape=[jax.ShapeDtypeStruct((d, width), BF16), jax.ShapeDtypeStruct((d, LANES), BF16)],
        compiler_params=pltpu.CompilerParams(
            dimension_semantics=("arbitrary",), vmem_limit_bytes=VMEM_LIMIT),
        name="pack_w",
    )(w_t, w_t)


def _proj_kernel(x_ref, w_ref, wlr_ref, up_ref, gbias_ref, lf_ref, lb_ref, o_ref):
    xb = x_ref[...].astype(BF16)
    mm = lambda w: jnp.dot(xb, w, preferred_element_type=F32)
    silu = lambda a, cols: a * _sigmoid(a)
    ident = lambda a, cols: a
    squash = lambda a, cols: _sigmoid(a)
    scaled = lambda fn, c: (lambda a, cols: fn(a, cols) * c)

    def log2_forget(logits_ref):
        lb = _lower_bound(logits_ref[...], 0)
        return lambda h, cols: jnp.log2(lb[:, cols] + (1.0 - lb[:, cols]) * _sigmoid(h))

    def w_columns(name):
        lo = _ORIG_START[name] - (_LR_HI - _LR_LO if _ORIG_START[name] >= _LR_HI else 0)
        return lambda cols: mm(w_ref[:, lo + cols.start:lo + cols.stop])

    def emit(name, fn, product=None):
        product = product or w_columns(name)
        for off in range(0, _OUT_SIZE[name], PROJ_SUBN):
            cols = slice(off, off + PROJ_SUBN)
            dst = slice(COL[name] + off, COL[name] + off + PROJ_SUBN)
            o_ref[:, dst] = fn(product(cols), cols).astype(BF16)

    def log_sigmoid_gate(z, cols):
        log_sig = jnp.minimum(z, 0.0) - jnp.log2(1.0 + jnp.exp2(jnp.abs(z) * (-LOG2E))) * (1.0 / LOG2E)
        return log_sig * (LOG2E / GLA_GATE_NORMALIZER)

    lr = mm(wlr_ref[...]).astype(BF16)
    for d, name in enumerate(("g_f", "g_b")):
        base = d * GLA_QK
        emit(name, log_sigmoid_gate, product=lambda cols, base=base: (
            jnp.dot(lr, up_ref[:, base + cols.start:base + cols.stop], preferred_element_type=F32)
            + gbias_ref[:, base + cols.start:base + cols.stop]))
    emit("a_q", scaled(ident, GLA_DK ** -0.5))
    emit("a_gate", silu)
    emit("a_k", ident)
    emit("h_q", scaled(silu, HGRN_DK ** -0.5))
    emit("a_v", ident)
    emit("h_ff", log2_forget(lf_ref))
    emit("h_fb", log2_forget(lb_ref))
    emit("h_i", ident)
    emit("h_gate", silu)
    emit("m_gla", squash)
    emit("m_hgrn", squash)


def _project(x2d, w_packed, w_lr, up, gbias, logits_f, logits_b):
    n = x2d.shape[0]
    assert n % PROJ_TM == 0
    resident = lambda a: pl.BlockSpec(a.shape, lambda i: (0,) * a.ndim,
                                      pipeline_mode=pl.Buffered(1))
    consts = (w_packed, w_lr, up, gbias, logits_f, logits_b)
    return pl.pallas_call(
        _proj_kernel,
        grid=(n // PROJ_TM,),
        in_specs=[pl.BlockSpec((PROJ_TM, D_MODEL), lambda i: (i, 0))] + [resident(a) for a in consts],
        out_specs=pl.BlockSpec((PROJ_TM, PROJ_COLS), lambda i: (i, 0)),
        out_shape=jax.ShapeDtypeStruct((n, PROJ_COLS), BF16),
        compiler_params=pltpu.CompilerParams(
            dimension_semantics=("arbitrary",), vmem_limit_bytes=VMEM_LIMIT),
        name="proj",
    )(x2d, *consts)


def _dot_nt(a, b):
    return lax.dot_general(a, b, (((1,), (1,)), ((), ())), preferred_element_type=F32)


def _dot_tn(a, b):
    return lax.dot_general(a, b, (((0,), (0,)), ((), ())), preferred_element_type=F32)


def _block_diag_cumsum_mat(rows, c):
    row = lax.broadcasted_iota(jnp.int32, (rows, rows), 0)
    col = lax.broadcasted_iota(jnp.int32, (rows, rows), 1)
    return jnp.where(((row // c) == (col // c)) & (col <= row), 1.0, 0.0).astype(BF16)


def _scan_core(q_ref, k_of, g_refs, v_ref, o_ref, scratch, seq, dk, dv, tiling):
    qi_ref, upd_ref, dec_ref, oacc_ref, cum_ref, st_ref = scratch
    c = CHUNK
    pair = 2 * c
    tile, tiles_per_trip = tiling
    pairs_per_tile = tile // pair
    n_tiles = seq // tile
    n = seq // pair
    mid = c // 2
    row = lax.broadcasted_iota(jnp.int32, (pair, pair), 0)
    col = lax.broadcasted_iota(jnp.int32, (pair, pair), 1)
    masks = (col <= row, col >= row)
    cum_mat = _block_diag_cumsum_mat(CUMSUM_ROWS, c)

    def tile_rows(t):
        return pl.ds(pl.multiple_of(t * tile, tile), tile)

    def stage(t):
        g_pair = jnp.concatenate([g_refs[0][0, tile_rows(t), :], g_refs[1][0, tile_rows(t), :]], axis=1)
        for r in range(0, tile, CUMSUM_ROWS):
            cum_ref[r:r + CUMSUM_ROWS, :] = jnp.dot(cum_mat, g_pair[r:r + CUMSUM_ROWS],
                                                    preferred_element_type=F32)

    stage(0)

    def chunk_terms(d, q, k, g, cum):
        if d == 0:
            b_mid = cum[mid:mid + 1, :]
            total = cum[c - 1:c, :]
            rel, lead, trail = cum - b_mid, b_mid, total - b_mid
        else:
            e = cum - g
            e_mid = e[mid:mid + 1, :]
            total = cum[c - 1:c, :]
            rel, lead, trail = e_mid - e, total - e_mid, e_mid
        return q * jnp.exp2(rel), k * jnp.exp2(-rel), lead, trail, total

    def phase1(i, carry):
        rows = tile_rows(i)
        q = q_ref[0, rows, :].astype(F32)
        gs = (g_refs[0][0, rows, :].astype(F32), g_refs[1][0, rows, :].astype(F32))
        ks = (k_of(0, gs[0], rows), k_of(1, gs[1], rows))
        cum = cum_ref[...]
        v = v_ref[0, rows, :]
        bf = lambda a: a.astype(BF16)

        def scores_of(p):
            pair_rows = pl.ds(pl.multiple_of(i * tile + p * pair, pair), pair)
            sa, sb = slice(p * pair, p * pair + c), slice(p * pair + c, (p + 1) * pair)
            acc, kp_pair = None, []
            for d in range(2):
                lanes = slice(d * dk, (d + 1) * dk)
                qa, ka, lead_a, trail_a, tot_a = chunk_terms(d, q[sa], ks[d][sa], gs[d][sa], cum[sa, lanes])
                qb, kb, lead_b, trail_b, tot_b = chunk_terms(d, q[sb], ks[d][sb], gs[d][sb], cum[sb, lanes])
                if d == 0:
                    keys_a = jnp.concatenate([bf(ka), bf(kb)], axis=0)
                    keys_b = jnp.concatenate([bf(ka * jnp.exp2(trail_a + lead_b)), bf(kb)], axis=0)
                    qi = jnp.concatenate([qa * jnp.exp2(lead_a), qb * jnp.exp2(lead_b + tot_a)], axis=0)
                    kp = jnp.concatenate([ka * jnp.exp2(trail_a + tot_b), kb * jnp.exp2(trail_b)], axis=0)
                else:
                    keys_a = jnp.concatenate([bf(ka), bf(kb * jnp.exp2(trail_b + lead_a))], axis=0)
                    keys_b = jnp.concatenate([bf(ka), bf(kb)], axis=0)
                    qi = jnp.concatenate([qa * jnp.exp2(lead_a + tot_b), qb * jnp.exp2(lead_b)], axis=0)
                    kp = jnp.concatenate([ka * jnp.exp2(trail_a), kb * jnp.exp2(trail_b + tot_a)], axis=0)
                scores = jnp.concatenate([_dot_nt(bf(qa), keys_a), _dot_nt(bf(qb), keys_b)], axis=0)
                scores = jnp.where(masks[d], scores, 0.0)
                acc = scores if acc is None else acc + scores
                lane_decay = jnp.broadcast_to(jnp.exp2(tot_a + tot_b), (LANES, dk)).T
                if d == 0:
                    qi_f, decay_f = bf(qi), lane_decay
                else:
                    qi_ref[pair_rows, :] = bf(qi)
                    dec_ref[i * pairs_per_tile + p] = lane_decay
                kp_pair.append(bf(kp))
            return bf(acc), jnp.concatenate(kp_pair, axis=1), qi_f, decay_f

        def increment(p, kp):
            upd = _dot_tn(kp, v[p * pair:(p + 1) * pair])
            upd_ref[i * pairs_per_tile + p] = upd[dk:]
            return upd[:dk]

        def output(p, scores, qi_f, decay_f, upd_f):
            pair_rows = pl.ds(pl.multiple_of(i * tile + p * pair, pair), pair)
            st = st_ref[...]
            oacc_ref[pair_rows, :] = jnp.dot(
                jnp.concatenate([scores, qi_f], axis=1),
                jnp.concatenate([v[p * pair:(p + 1) * pair], bf(st)], axis=0),
                preferred_element_type=F32)
            st_ref[...] = st * jnp.concatenate([decay_f] * (dv // LANES), axis=1) + upd_f

        done, upd_f = {}, {}
        for s in range(pairs_per_tile + 2):
            if s < pairs_per_tile:
                done[s] = scores_of(s)
            if 1 <= s <= pairs_per_tile:
                upd_f[s - 1] = increment(s - 1, done[s - 1][1])
            if s == pairs_per_tile - 1:
                stage(jnp.minimum(i + 1, n_tiles - 1))
            if s >= 2:
                scores, _, qi_f, decay_f = done[s - 2]
                output(s - 2, scores, qi_f, decay_f, upd_f[s - 2])
        return carry

    st_ref[...] = jnp.zeros_like(st_ref)
    lax.fori_loop(0, n_tiles, phase1, 0, unroll=tiles_per_trip)

    def backward(j, st):
        ci = n - 1 - j
        rows = pl.ds(pl.multiple_of(ci * pair, pair), pair)
        o = oacc_ref[rows, :] + jnp.dot(qi_ref[rows, :], st.astype(BF16), preferred_element_type=F32)
        o_ref[0, rows, :] = o.astype(o_ref.dtype)
        return st * jnp.concatenate([dec_ref[ci]] * (dv // LANES), axis=1) + upd_ref[ci]

    lax.fori_loop(0, n, backward, jnp.zeros((dk, dv), F32), unroll=PHASE2_UNROLL)


def _scan_scratch(seq, dk, dv, tile):
    n = seq // (2 * CHUNK)
    return [pltpu.VMEM((seq, dk), BF16),
            pltpu.VMEM((n, dk, dv), F32),
            pltpu.VMEM((n, dk, LANES), F32),
            pltpu.VMEM((seq, dv), F32),
            pltpu.VMEM((tile, 2 * dk), F32),
            pltpu.VMEM((dk, dv), F32)]


def _scan_call(kernel_fn, name, p3, specs, heads, dk, dv, tiling):
    bsz, seq, _ = p3.shape
    tile = tiling[0]
    assert seq % tile == 0 and tile % CUMSUM_ROWS == 0 and tile % (2 * CHUNK) == 0
    assert CUMSUM_ROWS % CHUNK == 0
    return pl.pallas_call(
        functools.partial(kernel_fn, seq=seq),
        grid=(bsz, heads),
        in_specs=specs,
        out_specs=pl.BlockSpec((1, seq, dv), lambda b, h: (b, 0, h)),
        out_shape=jax.ShapeDtypeStruct((bsz, seq, heads * dv), BF16),
        scratch_shapes=_scan_scratch(seq, dk, dv, tile),
        compiler_params=pltpu.CompilerParams(
            dimension_semantics=("arbitrary", "arbitrary"), vmem_limit_bytes=VMEM_LIMIT),
        name=name,
    )(*([p3] * len(specs)))


def _head_block(seq, name, width):
    first = COL[name] // width
    return pl.BlockSpec((1, seq, width), lambda b, h: (b, 0, first + h))


def _gla_kernel(q_ref, k_ref, gf_ref, gb_ref, v_ref, o_ref, *scratch, seq):
    k_of = lambda d, g, rows: k_ref[0, rows, :].astype(F32)
    _scan_core(q_ref, k_of, (gf_ref, gb_ref), v_ref, o_ref, scratch, seq, GLA_DK, GLA_DV, GLA_TILING)


def _gla_scan(p3):
    seq = p3.shape[1]
    specs = [_head_block(seq, "a_q", GLA_DK), _head_block(seq, "a_k", GLA_DK),
             _head_block(seq, "g_f", GLA_DK), _head_block(seq, "g_b", GLA_DK),
             _head_block(seq, "a_v", GLA_DV)]
    return _scan_call(_gla_kernel, "gla_scan", p3, specs, GLA_HEADS, GLA_DK, GLA_DV, GLA_TILING)


def _hgrn_kernel(q_ref, gf_ref, gb_ref, v_ref, o_ref, *scratch, seq):
    k_of = lambda d, g, rows: 1.0 - jnp.exp2(g)
    _scan_core(q_ref, k_of, (gf_ref, gb_ref), v_ref, o_ref, scratch, seq, HGRN_DK, HGRN_DV,
               HGRN_TILING)


def _hgrn_scan(p3):
    seq = p3.shape[1]
    specs = [_head_block(seq, "h_q", HGRN_DK), _head_block(seq, "h_ff", HGRN_DK),
             _head_block(seq, "h_fb", HGRN_DK), _head_block(seq, "h_i", HGRN_DV)]
    return _scan_call(_hgrn_kernel, "hgrn_scan", p3, specs, HGRN_HEADS, HGRN_DK, HGRN_DV, HGRN_TILING)


def _head_rmsnorm_gate(o, gain, gate, heads, dv):
    parts = []
    for h in range(heads):
        seg = o[:, h * dv:(h + 1) * dv]
        ms = jnp.mean(seg * seg, axis=-1, keepdims=True)
        parts.append(seg * lax.rsqrt(ms + RMS_EPS) * gain)
    return (jnp.concatenate(parts, axis=1) * gate).astype(BF16)


def _merge_kernel(oa_ref, oh_ref, ag_ref, hg_ref, mg_ref, mh_ref, x_ref, wa_ref, wh_ref, wo_ref,
                  ga_ref, gh_ref, lng_ref, lnb_ref, out_ref):
    subs = [slice(r, r + MERGE_SUB) for r in range(0, out_ref.shape[0], MERGE_SUB)]
    mixed = []
    for sl in subs:
        na = _head_rmsnorm_gate(oa_ref[sl, :].astype(F32), ga_ref[...], ag_ref[sl, :].astype(F32),
                                GLA_HEADS, GLA_DV)
        nh = _head_rmsnorm_gate(oh_ref[sl, :].astype(F32), gh_ref[...], hg_ref[sl, :].astype(F32),
                                HGRN_HEADS, HGRN_DV)
        y_gla = jnp.dot(na, wa_ref[...], preferred_element_type=F32)
        y_hgrn = jnp.dot(nh, wh_ref[...], preferred_element_type=F32)
        mixed.append((mg_ref[sl, :].astype(F32) * y_gla
                      + mh_ref[sl, :].astype(F32) * y_hgrn).astype(BF16))
    for sl, y in zip(subs, mixed):
        r = DEEPNORM_ALPHA * x_ref[sl, :] + jnp.dot(y, wo_ref[...], preferred_element_type=F32)
        mu = jnp.mean(r, axis=-1, keepdims=True)
        rc = r - mu
        var = jnp.mean(rc * rc, axis=-1, keepdims=True)
        out_ref[sl, :] = rc * lax.rsqrt(var + LN_EPS) * lng_ref[...] + lnb_ref[...]


def _merge(o_gla, o_hgrn, p2, x2d, w_gla, w_hgrn, w_out, gain_a, gain_h, ln_g, ln_b):
    n = x2d.shape[0]
    tm = MERGE_TM
    wide = lambda name: pl.BlockSpec((tm, D_MODEL), lambda i, name=name: (i, COL[name] // D_MODEL))
    row = lambda: pl.BlockSpec((tm, D_MODEL), lambda i: (i, 0))
    full = lambda shape: pl.BlockSpec(shape, lambda i: (0,) * len(shape),
                                      pipeline_mode=pl.Buffered(1))
    return pl.pallas_call(
        _merge_kernel,
        grid=(n // tm,),
        in_specs=[row(), row(), wide("a_gate"), wide("h_gate"), wide("m_gla"), wide("m_hgrn"), row(),
                  full((GLA_V, D_MODEL)), full((HG_V, D_MODEL)), full((D_MODEL, D_MODEL)),
                  full((1, GLA_DV)), full((1, HGRN_DV)), full((1, D_MODEL)), full((1, D_MODEL))],
        out_specs=row(),
        out_shape=jax.ShapeDtypeStruct((n, D_MODEL), F32),
        compiler_params=pltpu.CompilerParams(
            dimension_semantics=("arbitrary",), vmem_limit_bytes=VMEM_LIMIT),
        name="merge",
    )(o_gla, o_hgrn, p2, p2, p2, p2, x2d, w_gla, w_hgrn, w_out, gain_a, gain_h, ln_g, ln_b)


def _gate_params(up_f, up_b, bias_f, bias_b):
    rank = up_f.shape[0]
    up = jnp.zeros((LANES, 2 * GLA_QK), F32)
    up = up.at[:rank, :GLA_QK].set(up_f).at[rank:2 * rank, GLA_QK:].set(up_b)
    bias = jnp.concatenate([bias_f, bias_b])[None, :]
    return up.astype(BF16), bias


def kernel(x, w_in, gla_gk_up_f, gla_gk_bias_f, gla_gk_up_b, gla_gk_bias_b, gla_norm_g,
           hgrn_lb_logits_f, hgrn_lb_logits_b, hgrn_norm_g, w_branch_gla, w_branch_hgrn,
           w_out, ln_g, ln_b):
    assert w_in.shape[0] == DEPTH == 1
    bsz, seq, d = x.shape
    assert d == D_MODEL
    x2d = x.astype(F32).reshape(bsz * seq, d)

    w_t = jnp.swapaxes(w_in, 1, 2).reshape(w_in.shape[2], w_in.shape[1]).astype(F32)
    w_packed, w_lr = _pack_w(w_t)
    up, gbias = _gate_params(gla_gk_up_f[0].astype(F32), gla_gk_up_b[0].astype(F32),
                             gla_gk_bias_f[0].astype(F32), gla_gk_bias_b[0].astype(F32))
    p2 = _project(x2d, w_packed, w_lr, up, gbias,
                  hgrn_lb_logits_f.astype(F32), hgrn_lb_logits_b.astype(F32))
    p3 = p2.reshape(bsz, seq, PROJ_COLS)

    o_gla = _gla_scan(p3)
    o_hgrn = _hgrn_scan(p3)

    out = _merge(
        o_gla.reshape(bsz * seq, GLA_V), o_hgrn.reshape(bsz * seq, HG_V), p2, x2d,
        w_branch_gla[0].astype(BF16), w_branch_hgrn[0].astype(BF16), w_out[0].astype(BF16),
        gla_norm_g[0].astype(F32).reshape(1, GLA_DV), hgrn_norm_g[0].astype(F32).reshape(1, HGRN_DV),
        ln_g[0].astype(F32).reshape(1, D_MODEL), ln_b[0].astype(F32).reshape(1, D_MODEL))
    return out.reshape(bsz, seq, d).astype(x.dtype)
```

```python
import functools

import jax
import jax.numpy as jnp
import numpy as np
from jax import lax
from jax.experimental import pallas as pl
from jax.experimental.pallas import tpu as pltpu

F32 = jnp.float32
BF16 = jnp.bfloat16

D_MODEL = 1024
DEPTH = 1
GLA_HEADS = 4
GLA_DK = 128
GLA_DV = 256
GLA_GATE_RANK = 16
GLA_GATE_NORMALIZER = 16.0
HGRN_HEADS = 8
HGRN_DK = 128
HGRN_DV = 128
GLA_QK = GLA_HEADS * GLA_DK
GLA_V = GLA_HEADS * GLA_DV
HG_K = HGRN_HEADS * HGRN_DK
HG_V = HGRN_HEADS * HGRN_DV
RMS_EPS = 1e-6
LN_EPS = 1e-5
DEEPNORM_ALPHA = (2.0 * DEPTH) ** 0.25
LOG2E = 1.4426950408889634

LANES = 128

_ORIG_SIZES = (GLA_QK, GLA_QK, GLA_V, GLA_V, GLA_GATE_RANK, GLA_GATE_RANK,
               HG_K, HG_K, HG_K, HG_V, HG_V, D_MODEL, D_MODEL)
_ORIG_NAMES = ("a_q", "a_k", "a_v", "a_gate", "lr_f", "lr_b",
               "h_q", "h_ff", "h_fb", "h_i", "h_gate", "m_gla", "m_hgrn")
_ORIG_START = dict(zip(_ORIG_NAMES, np.cumsum((0,) + _ORIG_SIZES[:-1]).tolist()))
_SIZE = dict(zip(_ORIG_NAMES, _ORIG_SIZES))
_LR_LO = _ORIG_START["lr_f"]
_LR_HI = _ORIG_START["lr_b"] + _SIZE["lr_b"]

_OUT_ORDER = ("a_q", "a_k", "a_v", "a_gate", "g_f", "g_b",
              "h_q", "h_ff", "h_fb", "h_i", "h_gate", "m_gla", "m_hgrn")
_OUT_SIZE = dict(_SIZE, g_f=GLA_QK, g_b=GLA_QK)
COL = dict(zip(_OUT_ORDER, np.cumsum([0] + [_OUT_SIZE[nm] for nm in _OUT_ORDER[:-1]]).tolist()))
PROJ_COLS = sum(_OUT_SIZE[nm] for nm in _OUT_ORDER)

CHUNK = 64
GLA_TILING = (512, 4)
HGRN_TILING = (2048, 1)
CUMSUM_ROWS = 128
PHASE2_UNROLL = 32
PROJ_TM = 512
PROJ_SUBN = 512
WPACK_TN = 1024
MERGE_TM = 1024
MERGE_SUB = 256
VMEM_LIMIT = 60 * 1024 * 1024


def _sigmoid(x):
    return 1.0 / (1.0 + jnp.exp2(x * (-LOG2E)))


def _lower_bound(logits, layer):
    m = jnp.max(logits, axis=0, keepdims=True)
    e = jnp.exp(logits - m)
    return jnp.sum(e[:layer + 1], axis=0, keepdims=True) / jnp.sum(e, axis=0, keepdims=True)


_LR_GAP = _LR_HI - _LR_LO
PACK_STEPS = (sum(_ORIG_SIZES) - _LR_GAP) // WPACK_TN
assert _LR_LO % WPACK_TN == 0 and _LR_LO > 0 and WPACK_TN % _LR_GAP == 0 and WPACK_TN % PROJ_SUBN == 0
assert (sum(_ORIG_SIZES) - _LR_GAP) % WPACK_TN == 0


def _pack_w_step(j, a_ref, b_ref, w_ref, wlr_ref):
    first_shifted = _LR_LO // WPACK_TN

    @pl.when(j < first_shifted)
    def _():
        w_ref[j] = a_ref[...].T.astype(BF16)

    @pl.when(j == first_shifted - 1)
    def _():
        padded = jnp.concatenate([b_ref[...], jnp.zeros((LANES - _LR_GAP, b_ref.shape[1]), F32)], axis=0)
        wlr_ref[...] = padded.T.astype(BF16)

    @pl.when(j >= first_shifted)
    def _():
        rows = jnp.concatenate([a_ref[_LR_GAP:, :], b_ref[...]], axis=0)
        w_ref[j] = rows.T.astype(BF16)


def _proj_kernel(a_ref, b_ref, x_ref, up_ref, gbias_ref, lf_ref, lb_ref, o_ref, w_ref, wlr_ref):
    step = pl.program_id(0)

    @pl.when(step < PACK_STEPS)
    def _():
        _pack_w_step(step, a_ref, b_ref, w_ref, wlr_ref)

    @pl.when(step >= PACK_STEPS)
    def _():
        _proj_tile(x_ref, w_ref, wlr_ref, up_ref, gbias_ref, lf_ref, lb_ref, o_ref)


def _proj_tile(x_ref, w_ref, wlr_ref, up_ref, gbias_ref, lf_ref, lb_ref, o_ref):
    xb = x_ref[...].astype(BF16)
    mm = lambda w: jnp.dot(xb, w, preferred_element_type=F32)
    silu = lambda a, cols: a * _sigmoid(a)
    ident = lambda a, cols: a
    squash = lambda a, cols: _sigmoid(a)
    scaled = lambda fn, c: (lambda a, cols: fn(a, cols) * c)

    def log2_forget(logits_ref):
        lb = _lower_bound(logits_ref[...], 0)
        return lambda h, cols: jnp.log2(lb[:, cols] + (1.0 - lb[:, cols]) * _sigmoid(h))

    def w_columns(name):
        lo = _ORIG_START[name] - (_LR_GAP if _ORIG_START[name] >= _LR_HI else 0)

        def product(cols):
            blk, start = divmod(lo + cols.start, WPACK_TN)
            return mm(w_ref[blk, :, start:start + (cols.stop - cols.start)])
        return product

    def emit(name, fn, product=None):
        product = product or w_columns(name)
        for off in range(0, _OUT_SIZE[name], PROJ_SUBN):
            cols = slice(off, off + PROJ_SUBN)
            dst = slice(COL[name] + off, COL[name] + off + PROJ_SUBN)
            o_ref[:, dst] = fn(product(cols), cols).astype(BF16)

    def log_sigmoid_gate(z, cols):
        log_sig = jnp.minimum(z, 0.0) - jnp.log2(1.0 + jnp.exp2(jnp.abs(z) * (-LOG2E))) * (1.0 / LOG2E)
        return log_sig * (LOG2E / GLA_GATE_NORMALIZER)

    lr = mm(wlr_ref[...]).astype(BF16)
    for d, name in enumerate(("g_f", "g_b")):
        base = d * GLA_QK
        emit(name, log_sigmoid_gate, product=lambda cols, base=base: (
            jnp.dot(lr, up_ref[:, base + cols.start:base + cols.stop], preferred_element_type=F32)
            + gbias_ref[:, base + cols.start:base + cols.stop]))
    emit("a_q", scaled(ident, GLA_DK ** -0.5))
    emit("a_gate", silu)
    emit("a_k", ident)
    emit("h_q", scaled(silu, HGRN_DK ** -0.5))
    emit("a_v", ident)
    emit("h_ff", log2_forget(lf_ref))
    emit("h_fb", log2_forget(lb_ref))
    emit("h_i", ident)
    emit("h_gate", silu)
    emit("m_gla", squash)
    emit("m_hgrn", squash)


def _project(x2d, w_t, up, gbias, logits_f, logits_b):
    n, d = x2d.shape
    assert n % PROJ_TM == 0 and w_t.shape == (sum(_ORIG_SIZES), d)
    resident = lambda a: pl.BlockSpec(a.shape, lambda i: (0,) * a.ndim,
                                      pipeline_mode=pl.Buffered(1))
    consts = (up, gbias, logits_f, logits_b)
    w_tile = lambda i: jnp.minimum(i, PACK_STEPS - 1)
    row_tile = lambda i: jnp.maximum(i - PACK_STEPS, 0)
    return pl.pallas_call(
        _proj_kernel,
        grid=(PACK_STEPS + n // PROJ_TM,),
        in_specs=[pl.BlockSpec((WPACK_TN, d), lambda i: (w_tile(i), 0)),
                  pl.BlockSpec((_LR_GAP, d), lambda i: ((w_tile(i) + 1) * (WPACK_TN // _LR_GAP), 0)),
                  pl.BlockSpec((PROJ_TM, d), lambda i: (row_tile(i), 0))]
                 + [resident(a) for a in consts],
        out_specs=pl.BlockSpec((PROJ_TM, PROJ_COLS), lambda i: (row_tile(i), 0)),
        out_shape=jax.ShapeDtypeStruct((n, PROJ_COLS), BF16),
        scratch_shapes=[pltpu.VMEM((PACK_STEPS, d, WPACK_TN), BF16), pltpu.VMEM((d, LANES), BF16)],
        compiler_params=pltpu.CompilerParams(
            dimension_semantics=("arbitrary",), vmem_limit_bytes=VMEM_LIMIT),
        name="proj",
    )(w_t, w_t, x2d, *consts)


def _dot_nt(a, b):
    return lax.dot_general(a, b, (((1,), (1,)), ((), ())), preferred_element_type=F32)


def _dot_tn(a, b):
    return lax.dot_general(a, b, (((0,), (0,)), ((), ())), preferred_element_type=F32)


def _block_diag_cumsum_mat(rows, c):
    row = lax.broadcasted_iota(jnp.int32, (rows, rows), 0)
    col = lax.broadcasted_iota(jnp.int32, (rows, rows), 1)
    return jnp.where(((row // c) == (col // c)) & (col <= row), 1.0, 0.0).astype(BF16)


def _scan_core(q_ref, k_of, g_refs, v_ref, o_ref, scratch, seq, dk, dv, tiling):
    qi_ref, upd_ref, dec_ref, oacc_ref, cum_ref, st_ref = scratch
    c = CHUNK
    pair = 2 * c
    tile, tiles_per_trip = tiling
    pairs_per_tile = tile // pair
    n_tiles = seq // tile
    n = seq // pair
    mid = c // 2
    row = lax.broadcasted_iota(jnp.int32, (pair, pair), 0)
    col = lax.broadcasted_iota(jnp.int32, (pair, pair), 1)
    masks = (col <= row, col >= row)
    cum_mat = _block_diag_cumsum_mat(CUMSUM_ROWS, c)

    def tile_rows(t):
        return pl.ds(pl.multiple_of(t * tile, tile), tile)

    def stage(t):
        g_pair = jnp.concatenate([g_refs[0][0, tile_rows(t), :], g_refs[1][0, tile_rows(t), :]], axis=1)
        for r in range(0, tile, CUMSUM_ROWS):
            cum_ref[r:r + CUMSUM_ROWS, :] = jnp.dot(cum_mat, g_pair[r:r + CUMSUM_ROWS],
                                                    preferred_element_type=F32)

    stage(0)

    def chunk_terms(d, q, k, g, cum):
        if d == 0:
            b_mid = cum[mid:mid + 1, :]
            total = cum[c - 1:c, :]
            rel, lead, trail = cum - b_mid, b_mid, total - b_mid
        else:
            e = cum - g
            e_mid = e[mid:mid + 1, :]
            total = cum[c - 1:c, :]
            rel, lead, trail = e_mid - e, total - e_mid, e_mid
        return q * jnp.exp2(rel), k * jnp.exp2(-rel), lead, trail, total

    def phase1(i, carry):
        rows = tile_rows(i)
        q = q_ref[0, rows, :].astype(F32)
        gs = (g_refs[0][0, rows, :].astype(F32), g_refs[1][0, rows, :].astype(F32))
        ks = (k_of(0, gs[0], rows), k_of(1, gs[1], rows))
        cum = cum_ref[...]
        v = v_ref[0, rows, :]
        bf = lambda a: a.astype(BF16)

        def scores_of(p):
            pair_rows = pl.ds(pl.multiple_of(i * tile + p * pair, pair), pair)
            sa, sb = slice(p * pair, p * pair + c), slice(p * pair + c, (p + 1) * pair)
            acc, kp_pair = None, []
            for d in range(2):
                lanes = slice(d * dk, (d + 1) * dk)
                qa, ka, lead_a, trail_a, tot_a = chunk_terms(d, q[sa], ks[d][sa], gs[d][sa], cum[sa, lanes])
                qb, kb, lead_b, trail_b, tot_b = chunk_terms(d, q[sb], ks[d][sb], gs[d][sb], cum[sb, lanes])
                if d == 0:
                    keys_a = jnp.concatenate([bf(ka), bf(kb)], axis=0)
                    keys_b = jnp.concatenate([bf(ka * jnp.exp2(trail_a + lead_b)), bf(kb)], axis=0)
                    qi = jnp.concatenate([qa * jnp.exp2(lead_a), qb * jnp.exp2(lead_b + tot_a)], axis=0)
                    kp = jnp.concatenate([ka * jnp.exp2(trail_a + tot_b), kb * jnp.exp2(trail_b)], axis=0)
                else:
                    keys_a = jnp.concatenate([bf(ka), bf(kb * jnp.exp2(trail_b + lead_a))], axis=0)
                    keys_b = jnp.concatenate([bf(ka), bf(kb)], axis=0)
                    qi = jnp.concatenate([qa * jnp.exp2(lead_a + tot_b), qb * jnp.exp2(lead_b)], axis=0)
                    kp = jnp.concatenate([ka * jnp.exp2(trail_a), kb * jnp.exp2(trail_b + tot_a)], axis=0)
                scores = jnp.concatenate([_dot_nt(bf(qa), keys_a), _dot_nt(bf(qb), keys_b)], axis=0)
                scores = jnp.where(masks[d], scores, 0.0)
                acc = scores if acc is None else acc + scores
                lane_decay = jnp.broadcast_to(jnp.exp2(tot_a + tot_b), (LANES, dk)).T
                if d == 0:
                    qi_f, decay_f = bf(qi), lane_decay
                else:
                    qi_ref[pair_rows, :] = bf(qi)
                    dec_ref[i * pairs_per_tile + p] = lane_decay
                kp_pair.append(bf(kp))
            return bf(acc), jnp.concatenate(kp_pair, axis=1), qi_f, decay_f

        def increment(p, kp):
            upd = _dot_tn(kp, v[p * pair:(p + 1) * pair])
            upd_ref[i * pairs_per_tile + p] = upd[dk:]
            return upd[:dk]

        def output(p, scores, qi_f, decay_f, upd_f):
            pair_rows = pl.ds(pl.multiple_of(i * tile + p * pair, pair), pair)
            st = st_ref[...]
            oacc_ref[pair_rows, :] = jnp.dot(
                jnp.concatenate([scores, qi_f], axis=1),
                jnp.concatenate([v[p * pair:(p + 1) * pair], bf(st)], axis=0),
                preferred_element_type=F32)
            st_ref[...] = st * jnp.concatenate([decay_f] * (dv // LANES), axis=1) + upd_f

        done, upd_f = {}, {}
        for s in range(pairs_per_tile + 2):
            if s < pairs_per_tile:
                done[s] = scores_of(s)
            if 1 <= s <= pairs_per_tile:
                upd_f[s - 1] = increment(s - 1, done[s - 1][1])
            if s == pairs_per_tile - 1:
                stage(jnp.minimum(i + 1, n_tiles - 1))
            if s >= 2:
                scores, _, qi_f, decay_f = done[s - 2]
                output(s - 2, scores, qi_f, decay_f, upd_f[s - 2])
        return carry

    st_ref[...] = jnp.zeros_like(st_ref)
    lax.fori_loop(0, n_tiles, phase1, 0, unroll=tiles_per_trip)

    def backward(j, st):
        ci = n - 1 - j
        rows = pl.ds(pl.multiple_of(ci * pair, pair), pair)
        o = oacc_ref[rows, :] + jnp.dot(qi_ref[rows, :], st.astype(BF16), preferred_element_type=F32)
        o_ref[0, rows, :] = o.astype(o_ref.dtype)
        return st * jnp.concatenate([dec_ref[ci]] * (dv // LANES), axis=1) + upd_ref[ci]

    lax.fori_loop(0, n, backward, jnp.zeros((dk, dv), F32), unroll=PHASE2_UNROLL)


def _scan_scratch(seq, dk, dv, tile):
    n = seq // (2 * CHUNK)
    return [pltpu.VMEM((seq, dk), BF16),
            pltpu.VMEM((n, dk, dv), F32),
            pltpu.VMEM((n, dk, LANES), F32),
            pltpu.VMEM((seq, dv), F32),
            pltpu.VMEM((tile, 2 * dk), F32),
            pltpu.VMEM((dk, dv), F32)]


def _scan_call(kernel_fn, name, p3, specs, heads, dk, dv, tiling):
    bsz, seq, _ = p3.shape
    tile = tiling[0]
    assert seq % tile == 0 and tile % CUMSUM_ROWS == 0 and tile % (2 * CHUNK) == 0
    assert CUMSUM_ROWS % CHUNK == 0
    return pl.pallas_call(
        functools.partial(kernel_fn, seq=seq),
        grid=(bsz, heads),
        in_specs=specs,
        out_specs=pl.BlockSpec((1, seq, dv), lambda b, h: (b, 0, h)),
        out_shape=jax.ShapeDtypeStruct((bsz, seq, heads * dv), BF16),
        scratch_shapes=_scan_scratch(seq, dk, dv, tile),
        compiler_params=pltpu.CompilerParams(
            dimension_semantics=("arbitrary", "arbitrary"), vmem_limit_bytes=VMEM_LIMIT),
        name=name,
    )(*([p3] * len(specs)))


def _head_block(seq, name, width):
    first = COL[name] // width
    return pl.BlockSpec((1, seq, width), lambda b, h: (b, 0, first + h))


def _gla_kernel(q_ref, k_ref, gf_ref, gb_ref, v_ref, o_ref, *scratch, seq):
    k_of = lambda d, g, rows: k_ref[0, rows, :].astype(F32)
    _scan_core(q_ref, k_of, (gf_ref, gb_ref), v_ref, o_ref, scratch, seq, GLA_DK, GLA_DV, GLA_TILING)


def _gla_scan(p3):
    seq = p3.shape[1]
    specs = [_head_block(seq, "a_q", GLA_DK), _head_block(seq, "a_k", GLA_DK),
             _head_block(seq, "g_f", GLA_DK), _head_block(seq, "g_b", GLA_DK),
             _head_block(seq, "a_v", GLA_DV)]
    return _scan_call(_gla_kernel, "gla_scan", p3, specs, GLA_HEADS, GLA_DK, GLA_DV, GLA_TILING)


def _hgrn_kernel(q_ref, gf_ref, gb_ref, v_ref, o_ref, *scratch, seq):
    k_of = lambda d, g, rows: 1.0 - jnp.exp2(g)
    _scan_core(q_ref, k_of, (gf_ref, gb_ref), v_ref, o_ref, scratch, seq, HGRN_DK, HGRN_DV,
               HGRN_TILING)


def _hgrn_scan(p3):
    seq = p3.shape[1]
    specs = [_head_block(seq, "h_q", HGRN_DK), _head_block(seq, "h_ff", HGRN_DK),
             _head_block(seq, "h_fb", HGRN_DK), _head_block(seq, "h_i", HGRN_DV)]
    return _scan_call(_hgrn_kernel, "hgrn_scan", p3, specs, HGRN_HEADS, HGRN_DK, HGRN_DV, HGRN_TILING)


def _head_rmsnorm_gate(o, gain, gate, heads, dv):
    parts = []
    for h in range(heads):
        seg = o[:, h * dv:(h + 1) * dv]
        ms = jnp.mean(seg * seg, axis=-1, keepdims=True)
        parts.append(seg * lax.rsqrt(ms + RMS_EPS) * gain)
    return (jnp.concatenate(parts, axis=1) * gate).astype(BF16)


def _merge_kernel(oa_ref, oh_ref, ag_ref, hg_ref, mg_ref, mh_ref, x_ref, wa_ref, wh_ref, wo_ref,
                  ga_ref, gh_ref, lng_ref, lnb_ref, out_ref):
    subs = [slice(r, r + MERGE_SUB) for r in range(0, out_ref.shape[0], MERGE_SUB)]
    mixed = []
    for sl in subs:
        na = _head_rmsnorm_gate(oa_ref[sl, :].astype(F32), ga_ref[...], ag_ref[sl, :].astype(F32),
                                GLA_HEADS, GLA_DV)
        nh = _head_rmsnorm_gate(oh_ref[sl, :].astype(F32), gh_ref[...], hg_ref[sl, :].astype(F32),
                                HGRN_HEADS, HGRN_DV)
        y_gla = jnp.dot(na, wa_ref[...], preferred_element_type=F32)
        y_hgrn = jnp.dot(nh, wh_ref[...], preferred_element_type=F32)
        mixed.append((mg_ref[sl, :].astype(F32) * y_gla
                      + mh_ref[sl, :].astype(F32) * y_hgrn).astype(BF16))
    for sl, y in zip(subs, mixed):
        r = DEEPNORM_ALPHA * x_ref[sl, :] + jnp.dot(y, wo_ref[...], preferred_element_type=F32)
        mu = jnp.mean(r, axis=-1, keepdims=True)
        rc = r - mu
        var = jnp.mean(rc * rc, axis=-1, keepdims=True)
        out_ref[sl, :] = rc * lax.rsqrt(var + LN_EPS) * lng_ref[...] + lnb_ref[...]


def _merge(o_gla, o_hgrn, p2, x2d, w_gla, w_hgrn, w_out, gain_a, gain_h, ln_g, ln_b):
    n = x2d.shape[0]
    tm = MERGE_TM
    wide = lambda name: pl.BlockSpec((tm, D_MODEL), lambda i, name=name: (i, COL[name] // D_MODEL))
    row = lambda: pl.BlockSpec((tm, D_MODEL), lambda i: (i, 0))
    full = lambda shape: pl.BlockSpec(shape, lambda i: (0,) * len(shape),
                                      pipeline_mode=pl.Buffered(1))
    return pl.pallas_call(
        _merge_kernel,
        grid=(n // tm,),
        in_specs=[row(), row(), wide("a_gate"), wide("h_gate"), wide("m_gla"), wide("m_hgrn"), row(),
                  full((GLA_V, D_MODEL)), full((HG_V, D_MODEL)), full((D_MODEL, D_MODEL)),
                  full((1, GLA_DV)), full((1, HGRN_DV)), full((1, D_MODEL)), full((1, D_MODEL))],
        out_specs=row(),
        out_shape=jax.ShapeDtypeStruct((n, D_MODEL), F32),
        compiler_params=pltpu.CompilerParams(
            dimension_semantics=("arbitrary",), vmem_limit_bytes=VMEM_LIMIT),
        name="merge",
    )(o_gla, o_hgrn, p2, p2, p2, p2, x2d, w_gla, w_hgrn, w_out, gain_a, gain_h, ln_g, ln_b)


def _gate_params(up_f, up_b, bias_f, bias_b):
    rank = up_f.shape[0]
    up = jnp.zeros((LANES, 2 * GLA_QK), F32)
    up = up.at[:rank, :GLA_QK].set(up_f).at[rank:2 * rank, GLA_QK:].set(up_b)
    bias = jnp.concatenate([bias_f, bias_b])[None, :]
    return up.astype(BF16), bias


def kernel(x, w_in, gla_gk_up_f, gla_gk_bias_f, gla_gk_up_b, gla_gk_bias_b, gla_norm_g,
           hgrn_lb_logits_f, hgrn_lb_logits_b, hgrn_norm_g, w_branch_gla, w_branch_hgrn,
           w_out, ln_g, ln_b):
    assert w_in.shape[0] == DEPTH == 1
    bsz, seq, d = x.shape
    assert d == D_MODEL
    x2d = x.astype(F32).reshape(bsz * seq, d)

    w_t = jnp.swapaxes(w_in, 1, 2).reshape(w_in.shape[2], w_in.shape[1]).astype(F32)
    up, gbias = _gate_params(gla_gk_up_f[0].astype(F32), gla_gk_up_b[0].astype(F32),
                             gla_gk_bias_f[0].astype(F32), gla_gk_bias_b[0].astype(F32))
    p2 = _project(x2d, w_t, up, gbias,
                  hgrn_lb_logits_f.astype(F32), hgrn_lb_logits_b.astype(F32))
    p3 = p2.reshape(bsz, seq, PROJ_COLS)

    o_gla = _gla_scan(p3)
    o_hgrn = _hgrn_scan(p3)

    out = _merge(
        o_gla.reshape(bsz * seq, GLA_V), o_hgrn.reshape(bsz * seq, HG_V), p2, x2d,
        w_branch_gla[0].astype(BF16), w_branch_hgrn[0].astype(BF16), w_out[0].astype(BF16),
        gla_norm_g[0].astype(F32).reshape(1, GLA_DV), hgrn_norm_g[0].astype(F32).reshape(1, HGRN_DV),
        ln_g[0].astype(F32).reshape(1, D_MODEL), ln_b[0].astype(F32).reshape(1, D_MODEL))
    return out.reshape(bsz, seq, d).astype(x.dtype)
```

```python
import functools

import jax
import jax.numpy as jnp
import numpy as np
from jax import lax
from jax.experimental import pallas as pl
from jax.experimental.pallas import tpu as pltpu

F32 = jnp.float32
BF16 = jnp.bfloat16

D_MODEL = 1024
DEPTH = 1
GLA_HEADS = 4
GLA_DK = 128
GLA_DV = 256
GLA_GATE_RANK = 16
GLA_GATE_NORMALIZER = 16.0
HGRN_HEADS = 8
HGRN_DK = 128
HGRN_DV = 128
GLA_QK = GLA_HEADS * GLA_DK
GLA_V = GLA_HEADS * GLA_DV
HG_K = HGRN_HEADS * HGRN_DK
HG_V = HGRN_HEADS * HGRN_DV
RMS_EPS = 1e-6
LN_EPS = 1e-5
DEEPNORM_ALPHA = (2.0 * DEPTH) ** 0.25
LOG2E = 1.4426950408889634

LANES = 128

_ORIG_SIZES = (GLA_QK, GLA_QK, GLA_V, GLA_V, GLA_GATE_RANK, GLA_GATE_RANK,
               HG_K, HG_K, HG_K, HG_V, HG_V, D_MODEL, D_MODEL)
_ORIG_NAMES = ("a_q", "a_k", "a_v", "a_gate", "lr_f", "lr_b",
               "h_q", "h_ff", "h_fb", "h_i", "h_gate", "m_gla", "m_hgrn")
_ORIG_START = dict(zip(_ORIG_NAMES, np.cumsum((0,) + _ORIG_SIZES[:-1]).tolist()))
_SIZE = dict(zip(_ORIG_NAMES, _ORIG_SIZES))
_LR_LO = _ORIG_START["lr_f"]
_LR_HI = _ORIG_START["lr_b"] + _SIZE["lr_b"]

_OUT_ORDER = ("a_q", "a_k", "a_v", "a_gate", "g_f", "g_b",
              "h_q", "h_ff", "h_fb", "h_i", "h_gate", "m_gla", "m_hgrn")
_OUT_SIZE = dict(_SIZE, g_f=GLA_QK, g_b=GLA_QK)
COL = dict(zip(_OUT_ORDER, np.cumsum([0] + [_OUT_SIZE[nm] for nm in _OUT_ORDER[:-1]]).tolist()))
PROJ_COLS = sum(_OUT_SIZE[nm] for nm in _OUT_ORDER)

CHUNK = 64
GLA_TILING = (512, 4)
HGRN_TILING = (2048, 1)
CUMSUM_ROWS = 128
PHASE2_UNROLL = 32
PROJ_TM = 512
PROJ_SUBN = 512
WPACK_TN = 1024
MERGE_TM = 1024
MERGE_SUB = 256
MERGE_CAST_ROWS = 256
VMEM_LIMIT = 60 * 1024 * 1024


def _sigmoid(x):
    return 1.0 / (1.0 + jnp.exp2(x * (-LOG2E)))


def _lower_bound(logits, layer):
    m = jnp.max(logits, axis=0, keepdims=True)
    e = jnp.exp(logits - m)
    return jnp.sum(e[:layer + 1], axis=0, keepdims=True) / jnp.sum(e, axis=0, keepdims=True)


_LR_GAP = _LR_HI - _LR_LO
PACK_STEPS = (sum(_ORIG_SIZES) - _LR_GAP) // WPACK_TN
assert _LR_LO % WPACK_TN == 0 and _LR_LO > 0 and WPACK_TN % _LR_GAP == 0 and WPACK_TN % PROJ_SUBN == 0
assert (sum(_ORIG_SIZES) - _LR_GAP) % WPACK_TN == 0


def _pack_w_step(j, a_ref, b_ref, w_ref, wlr_ref):
    first_shifted = _LR_LO // WPACK_TN

    @pl.when(j < first_shifted)
    def _():
        w_ref[j] = a_ref[...].astype(BF16)

    @pl.when(j == first_shifted - 1)
    def _():
        padded = jnp.concatenate([b_ref[...], jnp.zeros((LANES - _LR_GAP, b_ref.shape[1]), F32)], axis=0)
        wlr_ref[...] = padded.astype(BF16)

    @pl.when(j >= first_shifted)
    def _():
        rows = jnp.concatenate([a_ref[_LR_GAP:, :], b_ref[...]], axis=0)
        w_ref[j] = rows.astype(BF16)


def _proj_kernel(a_ref, b_ref, x_ref, upf_ref, upb_ref, bias_f_ref, bias_b_ref, lf_ref, lb_ref, o_ref,
                 w_ref, wlr_ref, up_ref):
    step = pl.program_id(0)

    @pl.when(step == 0)
    def _():
        rank = upf_ref.shape[0]
        up_ref[...] = jnp.zeros(up_ref.shape, BF16)
        up_ref[:rank, :GLA_QK] = upf_ref[...].astype(BF16)
        up_ref[rank:2 * rank, GLA_QK:] = upb_ref[...].astype(BF16)

    @pl.when(step < PACK_STEPS)
    def _():
        _pack_w_step(step, a_ref, b_ref, w_ref, wlr_ref)

    @pl.when(step >= PACK_STEPS)
    def _():
        _proj_tile(x_ref, w_ref, wlr_ref, up_ref, (bias_f_ref, bias_b_ref), lf_ref, lb_ref, o_ref)


def _proj_tile(x_ref, w_ref, wlr_ref, up_ref, gbias_refs, lf_ref, lb_ref, o_ref):
    xb = x_ref[...].astype(BF16)
    mm = lambda w: lax.dot_general(xb, w, (((1,), (1,)), ((), ())), preferred_element_type=F32)
    silu = lambda a, cols: a * _sigmoid(a)
    ident = lambda a, cols: a
    squash = lambda a, cols: _sigmoid(a)
    scaled = lambda fn, c: (lambda a, cols: fn(a, cols) * c)

    def log2_forget(logits_ref):
        lb = _lower_bound(logits_ref[...], 0)
        return lambda h, cols: jnp.log2(lb[:, cols] + (1.0 - lb[:, cols]) * _sigmoid(h))

    def w_columns(name):
        lo = _ORIG_START[name] - (_LR_GAP if _ORIG_START[name] >= _LR_HI else 0)

        def product(cols):
            blk, start = divmod(lo + cols.start, WPACK_TN)
            return mm(w_ref[blk, start:start + (cols.stop - cols.start), :])
        return product

    def emit(name, fn, product=None):
        product = product or w_columns(name)
        for off in range(0, _OUT_SIZE[name], PROJ_SUBN):
            cols = slice(off, off + PROJ_SUBN)
            dst = slice(COL[name] + off, COL[name] + off + PROJ_SUBN)
            o_ref[:, dst] = fn(product(cols), cols).astype(BF16)

    def log_sigmoid_gate(z, cols):
        log_sig = jnp.minimum(z, 0.0) - jnp.log2(1.0 + jnp.exp2(jnp.abs(z) * (-LOG2E))) * (1.0 / LOG2E)
        return log_sig * (LOG2E / GLA_GATE_NORMALIZER)

    lr = mm(wlr_ref[...]).astype(BF16)
    for d, name in enumerate(("g_f", "g_b")):
        base = d * GLA_QK
        emit(name, log_sigmoid_gate, product=lambda cols, base=base, bias_ref=gbias_refs[d]: (
            jnp.dot(lr, up_ref[:, base + cols.start:base + cols.stop], preferred_element_type=F32)
            + bias_ref[:, cols]))
    emit("a_q", scaled(ident, GLA_DK ** -0.5))
    emit("a_gate", silu)
    emit("a_k", ident)
    emit("h_q", scaled(silu, HGRN_DK ** -0.5))
    emit("a_v", ident)
    emit("h_ff", log2_forget(lf_ref))
    emit("h_i", ident)
    emit("h_fb", log2_forget(lb_ref))
    emit("h_gate", silu)
    emit("m_gla", squash)
    emit("m_hgrn", squash)


def _project(x2d, w_t, up_f, up_b, bias_f, bias_b, logits_f, logits_b):
    n, d = x2d.shape
    assert n % PROJ_TM == 0 and w_t.shape == (sum(_ORIG_SIZES), d)
    assert up_f.shape == up_b.shape == (GLA_GATE_RANK, GLA_QK) and 2 * GLA_GATE_RANK == _LR_GAP
    resident = lambda a: pl.BlockSpec(a.shape, lambda i: (0,) * a.ndim,
                                      pipeline_mode=pl.Buffered(1))
    consts = (up_f, up_b, bias_f, bias_b, logits_f, logits_b)
    w_tile = lambda i: jnp.minimum(i, PACK_STEPS - 1)
    row_tile = lambda i: jnp.maximum(i - PACK_STEPS, 0)
    return pl.pallas_call(
        _proj_kernel,
        grid=(PACK_STEPS + n // PROJ_TM,),
        in_specs=[pl.BlockSpec((WPACK_TN, d), lambda i: (w_tile(i), 0)),
                  pl.BlockSpec((_LR_GAP, d), lambda i: ((w_tile(i) + 1) * (WPACK_TN // _LR_GAP), 0)),
                  pl.BlockSpec((PROJ_TM, d), lambda i: (row_tile(i), 0))]
                 + [resident(a) for a in consts],
        out_specs=pl.BlockSpec((PROJ_TM, PROJ_COLS), lambda i: (row_tile(i), 0)),
        out_shape=jax.ShapeDtypeStruct((n, PROJ_COLS), BF16),
        scratch_shapes=[pltpu.VMEM((PACK_STEPS, WPACK_TN, d), BF16), pltpu.VMEM((LANES, d), BF16),
                        pltpu.VMEM((LANES, 2 * GLA_QK), BF16)],
        compiler_params=pltpu.CompilerParams(
            dimension_semantics=("arbitrary",), vmem_limit_bytes=VMEM_LIMIT),
        name="proj",
    )(w_t, w_t, x2d, *consts)


def _dot_nt(a, b):
    return lax.dot_general(a, b, (((1,), (1,)), ((), ())), preferred_element_type=F32)


def _dot_tn(a, b):
    return lax.dot_general(a, b, (((0,), (0,)), ((), ())), preferred_element_type=F32)


def _block_diag_cumsum_mat(rows, c):
    row = lax.broadcasted_iota(jnp.int32, (rows, rows), 0)
    col = lax.broadcasted_iota(jnp.int32, (rows, rows), 1)
    return jnp.where(((row // c) == (col // c)) & (col <= row), 1.0, 0.0).astype(BF16)


def _scan_core(q_ref, k_of, g_refs, v_ref, o_ref, scratch, seq, dk, dv, tiling):
    qi_ref, upd_ref, dec_ref, oacc_ref, cum_ref, st_ref = scratch
    c = CHUNK
    pair = 2 * c
    tile, tiles_per_trip = tiling
    pairs_per_tile = tile // pair
    n_tiles = seq // tile
    n = seq // pair
    mid = c // 2
    row = lax.broadcasted_iota(jnp.int32, (pair, pair), 0)
    col = lax.broadcasted_iota(jnp.int32, (pair, pair), 1)
    masks = (col <= row, col >= row)
    cum_mat = _block_diag_cumsum_mat(CUMSUM_ROWS, c)

    def tile_rows(t):
        return pl.ds(pl.multiple_of(t * tile, tile), tile)

    def stage(t):
        g_pair = jnp.concatenate([g_refs[0][0, tile_rows(t), :], g_refs[1][0, tile_rows(t), :]], axis=1)
        for r in range(0, tile, CUMSUM_ROWS):
            cum_ref[r:r + CUMSUM_ROWS, :] = jnp.dot(cum_mat, g_pair[r:r + CUMSUM_ROWS],
                                                    preferred_element_type=F32)

    stage(0)

    def chunk_terms(d, q, k, g, cum):
        if d == 0:
            b_mid = cum[mid:mid + 1, :]
            total = cum[c - 1:c, :]
            rel, lead, trail = cum - b_mid, b_mid, total - b_mid
        else:
            e = cum - g
            e_mid = e[mid:mid + 1, :]
            total = cum[c - 1:c, :]
            rel, lead, trail = e_mid - e, total - e_mid, e_mid
        return q * jnp.exp2(rel), k * jnp.exp2(-rel), lead, trail, total

    def phase1(i, carry):
        bf = lambda a: a.astype(BF16)
        rows_of = lambda p: pl.ds(pl.multiple_of(i * tile + p * pair, pair), pair)
        v_of = lambda p: v_ref[0, rows_of(p), :]

        def scores_of(p):
            pair_rows = rows_of(p)
            q = q_ref[0, pair_rows, :].astype(F32)
            gs = tuple(g_ref[0, pair_rows, :].astype(F32) for g_ref in g_refs)
            ks = tuple(k_of(d, gs[d], pair_rows) for d in range(2))
            cum = cum_ref[p * pair:(p + 1) * pair, :]
            sa, sb = slice(0, c), slice(c, pair)
            acc, kp_pair = None, []
            for d in range(2):
                lanes = slice(d * dk, (d + 1) * dk)
                qa, ka, lead_a, trail_a, tot_a = chunk_terms(d, q[sa], ks[d][sa], gs[d][sa], cum[sa, lanes])
                qb, kb, lead_b, trail_b, tot_b = chunk_terms(d, q[sb], ks[d][sb], gs[d][sb], cum[sb, lanes])
                if d == 0:
                    keys_a = jnp.concatenate([bf(ka), bf(kb)], axis=0)
                    keys_b = jnp.concatenate([bf(ka * jnp.exp2(trail_a + lead_b)), bf(kb)], axis=0)
                    qi = jnp.concatenate([qa * jnp.exp2(lead_a), qb * jnp.exp2(lead_b + tot_a)], axis=0)
                    kp = jnp.concatenate([ka * jnp.exp2(trail_a + tot_b), kb * jnp.exp2(trail_b)], axis=0)
                else:
                    keys_a = jnp.concatenate([bf(ka), bf(kb * jnp.exp2(trail_b + lead_a))], axis=0)
                    keys_b = jnp.concatenate([bf(ka), bf(kb)], axis=0)
                    qi = jnp.concatenate([qa * jnp.exp2(lead_a + tot_b), qb * jnp.exp2(lead_b)], axis=0)
                    kp = jnp.concatenate([ka * jnp.exp2(trail_a), kb * jnp.exp2(trail_b + tot_a)], axis=0)
                scores = jnp.concatenate([_dot_nt(bf(qa), keys_a), _dot_nt(bf(qb), keys_b)], axis=0)
                scores = jnp.where(masks[d], scores, 0.0)
                acc = scores if acc is None else acc + scores
                lane_decay = jnp.broadcast_to(jnp.exp2(tot_a + tot_b), (LANES, dk)).T
                if d == 0:
                    qi_f, decay_f = bf(qi), lane_decay
                else:
                    qi_ref[pair_rows, :] = bf(qi)
                    dec_ref[i * pairs_per_tile + p] = lane_decay
                kp_pair.append(bf(kp))
            return bf(acc), jnp.concatenate(kp_pair, axis=1), qi_f, decay_f

        def increment(p, kp):
            upd = _dot_tn(kp, v_of(p))
            upd_ref[i * pairs_per_tile + p] = upd[dk:]
            return upd[:dk]

        def output(p, scores, qi_f, decay_f, upd_f):
            pair_rows = pl.ds(pl.multiple_of(i * tile + p * pair, pair), pair)
            st = st_ref[...]
            oacc_ref[pair_rows, :] = jnp.dot(
                jnp.concatenate([scores, qi_f], axis=1),
                jnp.concatenate([v_of(p), bf(st)], axis=0),
                preferred_element_type=F32)
            st_ref[...] = st * jnp.concatenate([decay_f] * (dv // LANES), axis=1) + upd_f

        done, upd_f = {}, {}
        for s in range(pairs_per_tile + 2):
            if s < pairs_per_tile:
                done[s] = scores_of(s)
            if 1 <= s <= pairs_per_tile:
                upd_f[s - 1] = increment(s - 1, done[s - 1][1])
            if s == pairs_per_tile - 1:
                stage(jnp.minimum(i + 1, n_tiles - 1))
            if s >= 2:
                scores, _, qi_f, decay_f = done[s - 2]
                output(s - 2, scores, qi_f, decay_f, upd_f[s - 2])
        return carry

    st_ref[...] = jnp.zeros_like(st_ref)
    lax.fori_loop(0, n_tiles, phase1, 0, unroll=tiles_per_trip)

    def backward(j, st):
        ci = n - 1 - j
        rows = pl.ds(pl.multiple_of(ci * pair, pair), pair)
        o = oacc_ref[rows, :] + jnp.dot(qi_ref[rows, :], st.astype(BF16), preferred_element_type=F32)
        o_ref[0, rows, :] = o.astype(o_ref.dtype)
        return st * jnp.concatenate([dec_ref[ci]] * (dv // LANES), axis=1) + upd_ref[ci]

    lax.fori_loop(0, n, backward, jnp.zeros((dk, dv), F32), unroll=PHASE2_UNROLL)


def _scan_scratch(seq, dk, dv, tile):
    n = seq // (2 * CHUNK)
    return [pltpu.VMEM((seq, dk), BF16),
            pltpu.VMEM((n, dk, dv), F32),
            pltpu.VMEM((n, dk, LANES), F32),
            pltpu.VMEM((seq, dv), F32),
            pltpu.VMEM((tile, 2 * dk), F32),
            pltpu.VMEM((dk, dv), F32)]


def _scan_call(kernel_fn, name, p3, specs, heads, dk, dv, tiling):
    bsz, seq, _ = p3.shape
    tile = tiling[0]
    assert seq % tile == 0 and tile % CUMSUM_ROWS == 0 and tile % (2 * CHUNK) == 0
    assert CUMSUM_ROWS % CHUNK == 0
    return pl.pallas_call(
        functools.partial(kernel_fn, seq=seq),
        grid=(bsz, heads),
        in_specs=specs,
        out_specs=pl.BlockSpec((1, seq, dv), lambda b, h: (b, 0, h)),
        out_shape=jax.ShapeDtypeStruct((bsz, seq, heads * dv), BF16),
        scratch_shapes=_scan_scratch(seq, dk, dv, tile),
        compiler_params=pltpu.CompilerParams(
            dimension_semantics=("arbitrary", "arbitrary"), vmem_limit_bytes=VMEM_LIMIT),
        name=name,
    )(*([p3] * len(specs)))


def _head_block(seq, name, width):
    first = COL[name] // width
    return pl.BlockSpec((1, seq, width), lambda b, h: (b, 0, first + h))


def _gla_kernel(q_ref, k_ref, gf_ref, gb_ref, v_ref, o_ref, *scratch, seq):
    k_of = lambda d, g, rows: k_ref[0, rows, :].astype(F32)
    _scan_core(q_ref, k_of, (gf_ref, gb_ref), v_ref, o_ref, scratch, seq, GLA_DK, GLA_DV, GLA_TILING)


def _gla_scan(p3):
    seq = p3.shape[1]
    specs = [_head_block(seq, "a_q", GLA_DK), _head_block(seq, "a_k", GLA_DK),
             _head_block(seq, "g_f", GLA_DK), _head_block(seq, "g_b", GLA_DK),
             _head_block(seq, "a_v", GLA_DV)]
    return _scan_call(_gla_kernel, "gla_scan", p3, specs, GLA_HEADS, GLA_DK, GLA_DV, GLA_TILING)


def _hgrn_kernel(q_ref, gf_ref, gb_ref, v_ref, o_ref, *scratch, seq):
    k_of = lambda d, g, rows: 1.0 - jnp.exp2(g)
    _scan_core(q_ref, k_of, (gf_ref, gb_ref), v_ref, o_ref, scratch, seq, HGRN_DK, HGRN_DV,
               HGRN_TILING)


def _hgrn_scan(p3):
    seq = p3.shape[1]
    specs = [_head_block(seq, "h_q", HGRN_DK), _head_block(seq, "h_ff", HGRN_DK),
             _head_block(seq, "h_fb", HGRN_DK), _head_block(seq, "h_i", HGRN_DV)]
    return _scan_call(_hgrn_kernel, "hgrn_scan", p3, specs, HGRN_HEADS, HGRN_DK, HGRN_DV, HGRN_TILING)


def _head_rmsnorm_gate(o, gain, gate, heads, dv):
    parts = []
    for h in range(heads):
        seg = o[:, h * dv:(h + 1) * dv]
        ms = jnp.mean(seg * seg, axis=-1, keepdims=True)
        parts.append(seg * lax.rsqrt(ms + RMS_EPS) * gain)
    return (jnp.concatenate(parts, axis=1) * gate).astype(BF16)


MERGE_CAST_STEPS = D_MODEL // MERGE_CAST_ROWS
assert GLA_V % MERGE_CAST_STEPS == 0 and HG_V % MERGE_CAST_STEPS == 0 and D_MODEL % MERGE_CAST_ROWS == 0


def _merge_kernel(oa_ref, oh_ref, ag_ref, hg_ref, mg_ref, mh_ref, x_ref, wa32_ref, wh32_ref, wo32_ref,
                  ga_ref, gh_ref, lng_ref, lnb_ref, out_ref, wa_ref, wh_ref, wo_ref):
    step = pl.program_id(0)

    @pl.when(step < MERGE_CAST_STEPS)
    def _():
        for src, dst in ((wa32_ref, wa_ref), (wh32_ref, wh_ref), (wo32_ref, wo_ref)):
            slab = src.shape[0]
            dst[pl.ds(pl.multiple_of(step * slab, slab), slab), :] = src[...].astype(BF16)

    @pl.when(step >= MERGE_CAST_STEPS)
    def _():
        _merge_tile(oa_ref, oh_ref, ag_ref, hg_ref, mg_ref, mh_ref, x_ref, wa_ref, wh_ref, wo_ref,
                    ga_ref, gh_ref, lng_ref, lnb_ref, out_ref)


def _merge_tile(oa_ref, oh_ref, ag_ref, hg_ref, mg_ref, mh_ref, x_ref, wa_ref, wh_ref, wo_ref,
                ga_ref, gh_ref, lng_ref, lnb_ref, out_ref):
    subs = [slice(r, r + MERGE_SUB) for r in range(0, out_ref.shape[0], MERGE_SUB)]
    mixed = []
    for sl in subs:
        na = _head_rmsnorm_gate(oa_ref[sl, :].astype(F32), ga_ref[...], ag_ref[sl, :].astype(F32),
                                GLA_HEADS, GLA_DV)
        nh = _head_rmsnorm_gate(oh_ref[sl, :].astype(F32), gh_ref[...], hg_ref[sl, :].astype(F32),
                                HGRN_HEADS, HGRN_DV)
        y_gla = jnp.dot(na, wa_ref[...], preferred_element_type=F32)
        y_hgrn = jnp.dot(nh, wh_ref[...], preferred_element_type=F32)
        mixed.append((mg_ref[sl, :].astype(F32) * y_gla
                      + mh_ref[sl, :].astype(F32) * y_hgrn).astype(BF16))
    for sl, y in zip(subs, mixed):
        r = DEEPNORM_ALPHA * x_ref[sl, :] + jnp.dot(y, wo_ref[...], preferred_element_type=F32)
        mu = jnp.mean(r, axis=-1, keepdims=True)
        rc = r - mu
        var = jnp.mean(rc * rc, axis=-1, keepdims=True)
        out_ref[sl, :] = rc * lax.rsqrt(var + LN_EPS) * lng_ref[...] + lnb_ref[...]


def _merge(o_gla, o_hgrn, p2, x2d, w_gla, w_hgrn, w_out, gain_a, gain_h, ln_g, ln_b):
    n = x2d.shape[0]
    tm = MERGE_TM
    tile = lambda i: jnp.maximum(i - MERGE_CAST_STEPS, 0)
    slab = lambda i: jnp.minimum(i, MERGE_CAST_STEPS - 1)
    wide = lambda name: pl.BlockSpec((tm, D_MODEL), lambda i, name=name: (tile(i), COL[name] // D_MODEL))
    row = lambda: pl.BlockSpec((tm, D_MODEL), lambda i: (tile(i), 0))
    full = lambda shape: pl.BlockSpec(shape, lambda i: (0,) * len(shape),
                                      pipeline_mode=pl.Buffered(1))
    weight = lambda w: pl.BlockSpec((w.shape[0] // MERGE_CAST_STEPS, w.shape[1]), lambda i: (slab(i), 0))
    return pl.pallas_call(
        _merge_kernel,
        grid=(MERGE_CAST_STEPS + n // tm,),
        in_specs=[row(), row(), wide("a_gate"), wide("h_gate"), wide("m_gla"), wide("m_hgrn"), row(),
                  weight(w_gla), weight(w_hgrn), weight(w_out),
                  full((1, GLA_DV)), full((1, HGRN_DV)), full((1, D_MODEL)), full((1, D_MODEL))],
        out_specs=row(),
        out_shape=jax.ShapeDtypeStruct((n, D_MODEL), F32),
        scratch_shapes=[pltpu.VMEM(w.shape, BF16) for w in (w_gla, w_hgrn, w_out)],
        compiler_params=pltpu.CompilerParams(
            dimension_semantics=("arbitrary",), vmem_limit_bytes=VMEM_LIMIT),
        name="merge",
    )(o_gla, o_hgrn, p2, p2, p2, p2, x2d, w_gla, w_hgrn, w_out, gain_a, gain_h, ln_g, ln_b)


def kernel(x, w_in, gla_gk_up_f, gla_gk_bias_f, gla_gk_up_b, gla_gk_bias_b, gla_norm_g,
           hgrn_lb_logits_f, hgrn_lb_logits_b, hgrn_norm_g, w_branch_gla, w_branch_hgrn,
           w_out, ln_g, ln_b):
    assert w_in.shape[0] == DEPTH == 1
    bsz, seq, d = x.shape
    assert d == D_MODEL
    x2d = x.astype(F32).reshape(bsz * seq, d)

    w_t = jnp.swapaxes(w_in, 1, 2).reshape(w_in.shape[2], w_in.shape[1]).astype(F32)
    p2 = _project(x2d, w_t, gla_gk_up_f[0].astype(F32), gla_gk_up_b[0].astype(F32),
                  gla_gk_bias_f[0].astype(F32)[None, :], gla_gk_bias_b[0].astype(F32)[None, :],
                  hgrn_lb_logits_f.astype(F32), hgrn_lb_logits_b.astype(F32))
    p3 = p2.reshape(bsz, seq, PROJ_COLS)

    o_gla = _gla_scan(p3)
    o_hgrn = _hgrn_scan(p3)

    out = _merge(
        o_gla.reshape(bsz * seq, GLA_V), o_hgrn.reshape(bsz * seq, HG_V), p2, x2d,
        w_branch_gla[0].astype(F32), w_branch_hgrn[0].astype(F32), w_out[0].astype(F32),
        gla_norm_g[0].astype(F32).reshape(1, GLA_DV), hgrn_norm_g[0].astype(F32).reshape(1, HGRN_DV),
        ln_g[0].astype(F32).reshape(1, D_MODEL), ln_b[0].astype(F32).reshape(1, D_MODEL))
    return out.reshape(bsz, seq, d).astype(x.dtype)
```
